```python
import jax, jax.numpy as jnp
from jax import lax
import numpy as np

D_MODEL = 2048
BATCH = 1
SEQ = 16384
DEPTH = 1
DEC_BATCH = 8
DEC_SEQ = 64
PAST_LEN = 1024

CHUNK = 64
Q_BLOCK = 128
ROPE_THETA = 10000.0
MLA_HEADS = 8
MLA_Q_LORA = 512
MLA_KV_LORA = 512
MLA_NOPE = 128
MLA_ROPE = 64
MLA_V = 128
DSA_HEADS = 8
DSA_KV_HEADS = 2
DSA_HEAD_DIM = 128
IDX_HEADS = 16
IDX_DIM = 64
IDX_TOPK = 256
MIX_WIDTH = MLA_HEADS * MLA_V + DSA_HEADS * DSA_HEAD_DIM
IN_SPLITS = (MLA_Q_LORA, MLA_KV_LORA, MLA_ROPE,
             DSA_HEADS * DSA_HEAD_DIM, DSA_KV_HEADS * DSA_HEAD_DIM, DSA_KV_HEADS * DSA_HEAD_DIM,
             IDX_HEADS * IDX_DIM, IDX_DIM, IDX_HEADS)
IN_COLS = sum(IN_SPLITS)
D_FF = -(-(8 * D_MODEL) // (3 * 256)) * 256
DEEPNORM_ALPHA = (2 * DEPTH) ** 0.25
DEEPNORM_BETA = (8 * DEPTH) ** -0.25
LN_EPS = 1e-5
RMS_EPS = 1e-6

kernel_name = 'hymba_mla_dsa_streaming_step'


def layer_norm(x, g, b):
    xf = x.astype(jnp.float32)
    mu = jnp.mean(xf, axis=-1, keepdims=True)
    var = jnp.mean(jnp.square(xf - mu), axis=-1, keepdims=True)
    y = (xf - mu) * lax.rsqrt(var + LN_EPS) * g.astype(jnp.float32) + b.astype(jnp.float32)
    return y.astype(x.dtype)


def rms_norm(x, g):
    xf = x.astype(jnp.float32)
    y = xf * lax.rsqrt(jnp.mean(jnp.square(xf), axis=-1, keepdims=True) + RMS_EPS) * g.astype(jnp.float32)
    return y.astype(x.dtype)


def rope(x, pos):
    d = x.shape[-1]
    half = d // 2
    inv = 1.0 / (ROPE_THETA ** (jnp.arange(half, dtype=jnp.float32) / half))
    ang = pos.astype(jnp.float32)[:, None] * inv[None, :]
    cos = jnp.cos(ang)[None, :, None, :]
    sin = jnp.sin(ang)[None, :, None, :]
    xf = x.astype(jnp.float32)
    x1, x2 = xf[..., :half], xf[..., half:]
    return jnp.concatenate([x1 * cos - x2 * sin, x2 * cos + x1 * sin], axis=-1).astype(x.dtype)


def chunk_mask(q_pos, k_pos):
    return (k_pos // CHUNK)[None, :] <= (q_pos // CHUNK)[:, None]


def to_blocks(a):
    b, t = a.shape[:2]
    return jnp.moveaxis(a.reshape(b, t // Q_BLOCK, Q_BLOCK, *a.shape[2:]), 1, 0)


def from_blocks(a):
    nb, b, qb = a.shape[:3]
    return jnp.moveaxis(a, 0, 1).reshape(b, nb * qb, *a.shape[3:])


def project(x, pos, w_in, w_uq, q_norm_g, kv_norm_g):
    b, t, _ = x.shape
    offsets = np.cumsum(IN_SPLITS)[:-1].tolist()
    c_q, c_kv, k_r, q_d, k_d, v_d, q_i, k_i, w_i = jnp.split(x @ w_in, offsets, axis=-1)
    q = jnp.einsum('btr,rhe->bthe', rms_norm(c_q, q_norm_g), w_uq)
    q_nope = q[..., :MLA_NOPE]
    q_pe = rope(q[..., MLA_NOPE:], pos)
    ckv = rms_norm(c_kv, kv_norm_g)
    kpe = rope(k_r[:, :, None, :], pos)[:, :, 0]
    qd = rope(q_d.reshape(b, t, DSA_HEADS, DSA_HEAD_DIM), pos)
    kd = rope(k_d.reshape(b, t, DSA_KV_HEADS, DSA_HEAD_DIM), pos)
    vd = v_d.reshape(b, t, DSA_KV_HEADS, DSA_HEAD_DIM)
    qi = rope(q_i.reshape(b, t, IDX_HEADS, IDX_DIM), pos)
    ki = rope(k_i[:, :, None, :], pos)[:, :, 0]
    return q_nope, q_pe, ckv, kpe, qd, kd, vd, qi, ki, w_i


def mla_attend(q_nope, q_pe, k_nope, k_pe, v, q_pos, k_pos):
    s = (jnp.einsum('bqhd,bshd->bhqs', q_nope, k_nope, preferred_element_type=jnp.float32)
         + jnp.einsum('bqhd,bsd->bhqs', q_pe, k_pe, preferred_element_type=jnp.float32))
    s = s * (MLA_NOPE + MLA_ROPE) ** -0.5
    s = jnp.where(chunk_mask(q_pos, k_pos)[None, None], s, -jnp.inf)
    p = jax.nn.softmax(s, axis=-1).astype(v.dtype)
    return jnp.einsum('bhqs,bshd->bqhd', p, v)


def dsa_attend(qd, qi, wi, q_pos, kd, vd, ki, k_pos, topk):
    logits = jnp.einsum('bqhd,bsd->bqhs', qi, ki, preferred_element_type=jnp.float32) * IDX_DIM ** -0.5
    score = jnp.einsum('bqh,bqhs->bqs', wi.astype(jnp.float32) * IDX_HEADS ** -0.5, jax.nn.relu(logits))
    score = jnp.where(chunk_mask(q_pos, k_pos)[None], score, -jnp.inf)
    top_val, top_idx = lax.top_k(score, topk)
    valid = jnp.isfinite(top_val)
    gather = jax.vmap(lambda a, i: a[i])
    k_sel = gather(kd, top_idx)
    v_sel = gather(vd, top_idx)
    b, tq = qd.shape[:2]
    qg = qd.reshape(b, tq, DSA_KV_HEADS, DSA_HEADS // DSA_KV_HEADS, DSA_HEAD_DIM)
    s = jnp.einsum('bqcgd,bqkcd->bqcgk', qg, k_sel, preferred_element_type=jnp.float32) * DSA_HEAD_DIM ** -0.5
    s = jnp.where(valid[:, :, None, None, :], s, -jnp.inf)
    p = jax.nn.softmax(s, axis=-1).astype(vd.dtype)
    o = jnp.einsum('bqcgk,bqkcd->bqcgd', p, v_sel)
    return o.reshape(b, tq, DSA_HEADS * DSA_HEAD_DIM)


def trunk_layer(x, q_pos, k_pos, past, params, blocked):
    (w_in, w_uq, q_norm_g, w_ukv, kv_norm_g, w_o, ln1_g, ln1_b,
     w_gate, w_up, w_down, ln2_g, ln2_b) = params
    b, t, _ = x.shape
    q_nope, q_pe, ckv, kpe, qd, kd, vd, qi, ki, wi = project(x, q_pos, w_in, w_uq, q_norm_g, kv_norm_g)
    new_rows = (ckv, kpe, kd, vd, ki)
    if past is None:
        ckv_all, kpe_all, kd_all, vd_all, ki_all = new_rows
    else:
        ckv_all, kpe_all, kd_all, vd_all, ki_all = tuple(
            jnp.concatenate([c, n], axis=1) for c, n in zip(past, new_rows))
    kv = jnp.einsum('bsr,rhe->bshe', ckv_all, w_ukv)
    k_nope, v_mla = kv[..., :MLA_NOPE], kv[..., MLA_NOPE:]
    topk = min(IDX_TOPK, k_pos.shape[0] // 4)

    def attend(qn, qp, qdb, qib, wib, qpb):
        return (mla_attend(qn, qp, k_nope, kpe_all, v_mla, qpb, k_pos),
                dsa_attend(qdb, qib, wib, qpb, kd_all, vd_all, ki_all, k_pos, topk))

    if blocked:
        mla_o, dsa_o = lax.map(lambda a: attend(*a),
                               (to_blocks(q_nope), to_blocks(q_pe), to_blocks(qd), to_blocks(qi),
                                to_blocks(wi), q_pos.reshape(-1, Q_BLOCK)))
        mla_o, dsa_o = from_blocks(mla_o), from_blocks(dsa_o)
    else:
        mla_o, dsa_o = attend(q_nope, q_pe, qd, qi, wi, q_pos)
    mixed = jnp.concatenate([mla_o.reshape(b, t, MLA_HEADS * MLA_V), dsa_o], axis=-1)
    x = layer_norm(DEEPNORM_ALPHA * x + mixed @ w_o, ln1_g, ln1_b)
    ffn = (jax.nn.silu(x @ w_gate) * (x @ w_up)) @ w_down
    x = layer_norm(DEEPNORM_ALPHA * x + ffn, ln2_g, ln2_b)
    return x, new_rows


def setup_inputs(seed: int = 0) -> dict:
    key = jax.random.key(seed)
    ks = jax.random.split(key, 22)

    def nrm(k, shape, scale):
        return jax.random.normal(k, shape, jnp.float32) * scale

    return {
        'x_prompt': nrm(ks[0], (BATCH, SEQ, D_MODEL), 1.0),
        'x_sample': nrm(ks[1], (DEC_BATCH, DEC_SEQ, D_MODEL), 1.0),
        'cache_mla_ckv': nrm(ks[2], (DEPTH, DEC_BATCH, PAST_LEN, MLA_KV_LORA), 1.0),
        'cache_mla_kpe': nrm(ks[3], (DEPTH, DEC_BATCH, PAST_LEN, MLA_ROPE), 1.0),
        'cache_dsa_k': nrm(ks[4], (DEPTH, DEC_BATCH, PAST_LEN, DSA_KV_HEADS, DSA_HEAD_DIM), 1.0),
        'cache_dsa_v': nrm(ks[5], (DEPTH, DEC_BATCH, PAST_LEN, DSA_KV_HEADS, DSA_HEAD_DIM), 1.0),
        'cache_idx_k': nrm(ks[6], (DEPTH, DEC_BATCH, PAST_LEN, IDX_DIM), 1.0),
        'w_in': nrm(ks[7], (DEPTH, D_MODEL, IN_COLS), D_MODEL ** -0.5),
        'w_uq': nrm(ks[8], (DEPTH, MLA_Q_LORA, MLA_HEADS, MLA_NOPE + MLA_ROPE), MLA_Q_LORA ** -0.5),
        'mla_q_norm_g': 1.0 + nrm(ks[9], (DEPTH, MLA_Q_LORA), 0.01),
        'w_ukv': nrm(ks[10], (DEPTH, MLA_KV_LORA, MLA_HEADS, MLA_NOPE + MLA_V), MLA_KV_LORA ** -0.5),
        'mla_kv_norm_g': 1.0 + nrm(ks[11], (DEPTH, MLA_KV_LORA), 0.01),
        'w_o': nrm(ks[12], (DEPTH, MIX_WIDTH, D_MODEL), MIX_WIDTH ** -0.5 * DEEPNORM_BETA),
        'ln1_g': 1.0 + nrm(ks[13], (DEPTH, D_MODEL), 0.01),
        'ln1_b': nrm(ks[14], (DEPTH, D_MODEL), 0.01),
        'w_gate': nrm(ks[15], (DEPTH, D_MODEL, D_FF), D_MODEL ** -0.5),
        'w_up': nrm(ks[16], (DEPTH, D_MODEL, D_FF), D_MODEL ** -0.5),
        'w_down': nrm(ks[17], (DEPTH, D_FF, D_MODEL), D_FF ** -0.5 * DEEPNORM_BETA),
        'ln2_g': 1.0 + nrm(ks[18], (DEPTH, D_MODEL), 0.01),
        'ln2_b': nrm(ks[19], (DEPTH, D_MODEL), 0.01),
    }


def reference(x_prompt, x_sample, cache_mla_ckv, cache_mla_kpe, cache_dsa_k, cache_dsa_v, cache_idx_k,
              w_in, w_uq, mla_q_norm_g, w_ukv, mla_kv_norm_g, w_o, ln1_g, ln1_b,
              w_gate, w_up, w_down, ln2_g, ln2_b):
    t_p = x_prompt.shape[1]
    t_s = x_sample.shape[1]
    past_len = cache_mla_ckv.shape[2]
    pos_p = jnp.arange(t_p, dtype=jnp.int32)
    pos_s = past_len + jnp.arange(t_s, dtype=jnp.int32)
    kpos_s = jnp.arange(past_len + t_s, dtype=jnp.int32)
    y_p, y_s = x_prompt, x_sample
    rows_p, rows_s = [], []
    for l in range(DEPTH):
        params = (w_in[l], w_uq[l], mla_q_norm_g[l], w_ukv[l], mla_kv_norm_g[l], w_o[l],
                  ln1_g[l], ln1_b[l], w_gate[l], w_up[l], w_down[l], ln2_g[l], ln2_b[l])
        y_p, r_p = trunk_layer(y_p, pos_p, pos_p, None, params, True)
        past = (cache_mla_ckv[l], cache_mla_kpe[l], cache_dsa_k[l], cache_dsa_v[l], cache_idx_k[l])
        y_s, r_s = trunk_layer(y_s, pos_s, kpos_s, past, params, False)
        rows_p.append(r_p)
        rows_s.append(r_s)

    def stack(rows, i):
        return jnp.stack([r[i] for r in rows], axis=0)

    return (y_p, y_s,
            stack(rows_p, 0), stack(rows_p, 1), stack(rows_p, 2), stack(rows_p, 3), stack(rows_p, 4),
            stack(rows_s, 0), stack(rows_s, 1), stack(rows_s, 2), stack(rows_s, 3), stack(rows_s, 4))
```

```python
import functools

import numpy as np
import jax
import jax.numpy as jnp
from jax import lax
from jax.experimental import pallas as pl
from jax.experimental.pallas import tpu as pltpu

CHUNK = 64
CHUNK_SHIFT = 6
ROPE_THETA = 10000.0
MLA_HEADS = 8
MLA_Q_LORA = 512
MLA_KV_LORA = 512
MLA_NOPE = 128
MLA_ROPE = 64
MLA_V = 128
DSA_HEADS = 8
DSA_KV_HEADS = 2
DSA_GROUP = DSA_HEADS // DSA_KV_HEADS
DSA_HEAD_DIM = 128
IDX_HEADS = 16
IDX_DIM = 64
IDX_TOPK = 256
LN_EPS = 1e-5
RMS_EPS = 1e-6

LANES = 128
MASKED = -1e30
INT_MIN = -2 ** 31
VMEM_LIMIT = 56 * 1024 * 1024

_C_CQ = 0
_C_CKV = 512
_C_QD = 1024
_C_KD = 2048
_C_VD = 2304
_C_QI = 2560
_C_KR = 3584
_C_KILO = 3712
_C_KIHI = 3840
_C_WI = 3968
_IN_COLS_P = 4096

F32 = jnp.float32
BF16 = jnp.bfloat16


def _dot(a, b):
    return jnp.dot(a, b, preferred_element_type=F32)


def _dot_nt(a, b):
    return lax.dot_general(a, b, (((1,), (1,)), ((), ())), preferred_element_type=F32)


def _params(n_axes, vmem=VMEM_LIMIT):
    return pltpu.CompilerParams(dimension_semantics=("arbitrary",) * n_axes, vmem_limit_bytes=vmem)


def _resident(shape, index_map):
    return pl.BlockSpec(shape, index_map, pipeline_mode=pl.Buffered(1))


def _rope128(x, cos, sin_signed):
    return x * cos + pltpu.roll(x, 64, 1) * sin_signed


def _rope64(x, cos, sin_signed, first_half):
    rot = jnp.where(first_half, pltpu.roll(x, 96, 1), pltpu.roll(x, 32, 1))
    return x * cos + rot * sin_signed


def _rope_tables(pos):
    pos = pos.astype(F32)[:, None]
    lane = np.arange(LANES)

    def tables(dim):
        half = dim // 2
        inv = 1.0 / (ROPE_THETA ** (jnp.arange(half, dtype=F32) / half))
        ang = pos * inv[None, :]
        cos, sin = jnp.cos(ang), jnp.sin(ang)
        idx = lane % half
        sign = np.where((lane % dim) < half, -1.0, 1.0).astype(np.float32)
        return cos[:, idx], sin[:, idx] * sign[None, :]

    c128, s128 = tables(128)
    c64, s64 = tables(64)
    return c128, s128, c64, s64


def _proj_kernel(x_ref, w_ref, wuq_ref, gq_ref, gkv_ref, c128_ref, s128_ref, c64_ref, s64_ref,
                 ckv_ref, kpe_ref, kd_ref, vd_ref, ki_ref,
                 q_ref, qd_ref, qi_ref, wi_ref, kpeb_ref, kdb_ref, vdb_ref, kilo_ref, kihi_ref):
    xb = x_ref[...].astype(BF16)
    c128, s128 = c128_ref[...], s128_ref[...]
    c64, s64 = c64_ref[...], s64_ref[...]
    first_half = (lax.broadcasted_iota(jnp.int32, c64.shape, 1) % 64) < 32

    def seg(a, b):
        return _dot(xb, w_ref[:, a:b])

    def rms(v, g):
        return v * lax.rsqrt(jnp.mean(v * v, axis=-1, keepdims=True) + RMS_EPS) * g

    qn = rms(seg(_C_CQ, _C_CQ + MLA_Q_LORA), gq_ref[...]).astype(BF16)
    for h in range(MLA_HEADS):
        qh = _dot(qn, wuq_ref[:, h * 256:(h + 1) * 256])
        q_ref[h, :, 0:128] = qh[:, 0:128].astype(BF16)
        q_ref[h, :, 128:256] = _rope64(qh[:, 128:256], c64, s64, first_half).astype(BF16)

    ckv_ref[...] = rms(seg(_C_CKV, _C_CKV + MLA_KV_LORA), gkv_ref[...])

    kr = _rope64(seg(_C_KR, _C_KR + 128), c64, s64, first_half)
    kpe_ref[...] = kr[:, 0:MLA_ROPE]
    kpeb_ref[...] = kr.astype(BF16)

    for h in range(DSA_HEADS):
        a = _C_QD + h * 128
        qd_ref[h] = _rope128(seg(a, a + 128), c128, s128).astype(BF16)
    for c in range(DSA_KV_HEADS):
        a = _C_KD + c * 128
        kdc = _rope128(seg(a, a + 128), c128, s128)
        kd_ref[:, c * 128:(c + 1) * 128] = kdc
        kdb_ref[:, c * 128:(c + 1) * 128] = kdc.astype(BF16)
    vd = seg(_C_VD, _C_VD + 256)
    vd_ref[...] = vd
    vdb_ref[...] = vd.astype(BF16)

    for hp in range(IDX_HEADS // 2):
        a = _C_QI + hp * 128
        qi_ref[hp] = _rope64(seg(a, a + 128), c64, s64, first_half).astype(BF16)
    kilo = _rope64(seg(_C_KILO, _C_KILO + 128), c64, s64, first_half)
    ki_ref[...] = kilo[:, 0:IDX_DIM]
    kilo_ref[...] = kilo.astype(BF16)
    kihi_ref[...] = _rope64(seg(_C_KIHI, _C_KIHI + 128), c64, s64, first_half).astype(BF16)
    wi_ref[...] = seg(_C_WI, _C_WI + 128) * (IDX_DIM ** -0.5 * IDX_HEADS ** -0.5)


def _proj(x2d, pos, wp, tm):
    n, d = x2d.shape
    c128, s128, c64, s64 = _rope_tables(pos)
    row = lambda w: pl.BlockSpec((tm, w), lambda i: (i, 0))
    heads = lambda nh, w: pl.BlockSpec((nh, tm, w), lambda i: (0, i, 0))
    out_shapes = (
        jax.ShapeDtypeStruct((n, MLA_KV_LORA), F32),
        jax.ShapeDtypeStruct((n, MLA_ROPE), F32),
        jax.ShapeDtypeStruct((n, 256), F32),
        jax.ShapeDtypeStruct((n, 256), F32),
        jax.ShapeDtypeStruct((n, IDX_DIM), F32),
        jax.ShapeDtypeStruct((MLA_HEADS, n, 256), BF16),
        jax.ShapeDtypeStruct((DSA_HEADS, n, 128), BF16),
        jax.ShapeDtypeStruct((IDX_HEADS // 2, n, 128), BF16),
        jax.ShapeDtypeStruct((n, 128), F32),
        jax.ShapeDtypeStruct((n, 128), BF16),
        jax.ShapeDtypeStruct((n, 256), BF16),
        jax.ShapeDtypeStruct((n, 256), BF16),
        jax.ShapeDtypeStruct((n, 128), BF16),
        jax.ShapeDtypeStruct((n, 128), BF16),
    )
    out_specs = (row(MLA_KV_LORA), row(MLA_ROPE), row(256), row(256), row(IDX_DIM),
                 heads(MLA_HEADS, 256), heads(DSA_HEADS, 128), heads(IDX_HEADS // 2, 128),
                 row(128), row(128), row(256), row(256), row(128), row(128))
    in_specs = [row(d),
                _resident((d, _IN_COLS_P), lambda i: (0, 0)),
                _resident((MLA_Q_LORA, MLA_HEADS * 256), lambda i: (0, 0)),
                _resident((1, MLA_Q_LORA), lambda i: (0, 0)),
                _resident((1, MLA_KV_LORA), lambda i: (0, 0)),
                row(128), row(128), row(128), row(128)]
    return pl.pallas_call(
        _proj_kernel, grid=(n // tm,), in_specs=in_specs, out_specs=out_specs, out_shape=out_shapes,
        compiler_params=_params(1), name="proj",
    )(x2d, wp["w_in"], wp["w_uq"], wp["g_q"], wp["g_kv"], c128, s128, c64, s64)


def _kvup_kernel(ckv_ref, kpeb_ref, wk_ref, wv_ref, k_ref, v_ref):
    cb = ckv_ref[...].astype(BF16)
    kpe = kpeb_ref[...]
    for h in range(MLA_HEADS):
        k_ref[h, :, 0:128] = _dot(cb, wk_ref[:, h * 128:(h + 1) * 128]).astype(BF16)
        k_ref[h, :, 128:256] = kpe
        v_ref[h] = _dot(cb, wv_ref[:, h * 128:(h + 1) * 128]).astype(BF16)


def _kv_up(ckv2d, kpeb2d, wp, tm):
    n = ckv2d.shape[0]
    return pl.pallas_call(
        _kvup_kernel, grid=(n // tm,),
        in_specs=[pl.BlockSpec((tm, MLA_KV_LORA), lambda i: (i, 0)),
                  pl.BlockSpec((tm, 128), lambda i: (i, 0)),
                  _resident((MLA_KV_LORA, MLA_HEADS * MLA_NOPE), lambda i: (0, 0)),
                  _resident((MLA_KV_LORA, MLA_HEADS * MLA_V), lambda i: (0, 0))],
        out_specs=(pl.BlockSpec((MLA_HEADS, tm, 256), lambda i: (0, i, 0)),
                   pl.BlockSpec((MLA_HEADS, tm, 128), lambda i: (0, i, 0))),
        out_shape=(jax.ShapeDtypeStruct((MLA_HEADS, n, 256), BF16),
                   jax.ShapeDtypeStruct((MLA_HEADS, n, 128), BF16)),
        compiler_params=_params(1), name="kv_up",
    )(ckv2d, kpeb2d, wp["w_uk"], wp["w_uv"])


def _softmax_update(s, v, m_scr, l_scr, acc_scr):
    reps = s.shape[1] // LANES
    m_prev = m_scr[...]
    m_next = jnp.maximum(m_prev, jnp.max(s, axis=1, keepdims=True))
    p = jnp.exp(s - jnp.concatenate([m_next] * reps, axis=1))
    alpha = jnp.exp(m_prev - m_next)
    l_scr[...] = alpha * l_scr[...] + jnp.sum(p, axis=1, keepdims=True)
    acc_scr[...] = acc_scr[...] * alpha + _dot(p.astype(BF16), v)
    m_scr[...] = m_next


def _chunk_mask(q0, kb0, tq, tk):
    qch = (q0 + lax.broadcasted_iota(jnp.int32, (tq, 1), 0)) >> CHUNK_SHIFT
    kch = (kb0 + lax.broadcasted_iota(jnp.int32, (1, tk), 1)) >> CHUNK_SHIFT
    return kch <= qch


def _mla_kernel(q_ref, k_ref, v_ref, o_ref, m_scr, l_scr, acc_scr, *, tq, tk, q_pos0, scale):
    q0 = q_pos0 + pl.program_id(2) * tq
    n_full = (q0 + CHUNK) // tk
    n_blocks = (q0 + tq + tk - 1) // tk
    m_scr[...] = jnp.full(m_scr.shape, MASKED, F32)
    l_scr[...] = jnp.zeros(l_scr.shape, F32)
    acc_scr[...] = jnp.zeros(acc_scr.shape, F32)
    q = q_ref[0, 0]

    def step(kb, masked):
        kb0 = pl.multiple_of(kb * tk, tk)
        s = _dot_nt(q, k_ref[0, 0, pl.ds(kb0, tk), :]) * scale
        if masked:
            s = jnp.where(_chunk_mask(q0, kb0, tq, tk), s, MASKED)
        _softmax_update(s, v_ref[0, 0, pl.ds(kb0, tk), :], m_scr, l_scr, acc_scr)

    lax.fori_loop(0, n_full, lambda kb, c: (step(kb, False), c)[1], 0)
    lax.fori_loop(n_full, n_blocks, lambda kb, c: (step(kb, True), c)[1], 0)
    o_ref[0] = (acc_scr[...] / l_scr[...]).astype(o_ref.dtype)


def _mla_attention(q, k, v, q_pos0, tq, tk):
    nh, b, t, _ = q.shape
    s = k.shape[2]
    kern = functools.partial(_mla_kernel, tq=tq, tk=tk, q_pos0=q_pos0,
                             scale=(MLA_NOPE + MLA_ROPE) ** -0.5)
    return pl.pallas_call(
        kern, grid=(b, nh, t // tq),
        in_specs=[pl.BlockSpec((1, 1, tq, 256), lambda bi, h, i: (h, bi, i, 0)),
                  pl.BlockSpec((1, 1, s, 256), lambda bi, h, i: (h, bi, 0, 0)),
                  pl.BlockSpec((1, 1, s, 128), lambda bi, h, i: (h, bi, 0, 0))],
        out_specs=pl.BlockSpec((1, tq, 128), lambda bi, h, i: (bi, i, h)),
        out_shape=jax.ShapeDtypeStruct((b, t, nh * MLA_V), BF16),
        scratch_shapes=[pltpu.VMEM((tq, LANES), F32), pltpu.VMEM((tq, LANES), F32),
                        pltpu.VMEM((tq, MLA_V), F32)],
        compiler_params=_params(3), name="mla_attn",
    )(q, k, v)


def _dsa_kernel(qd_ref, qi_ref, wi_ref, kd_ref, vd_ref, kilo_ref, kihi_ref, o_ref,
                key_scr, c_scr, cnt_scr, cand_scr, m_scr, l_scr, acc_scr,
                *, tq, tk, rg, q_pos0, s_real, topk, idx_bits, scale):
    q0 = q_pos0 + pl.program_id(1) * tq
    n_full = (q0 + CHUNK) // tk
    n_blocks = (q0 + tq + tk - 1) // tk
    sub = tk // LANES

    qi_all = qi_ref[:, 0].reshape(IDX_HEADS // 2 * tq, LANES)
    wi = wi_ref[0]

    def score_block(kb, masked):
        kb0 = pl.multiple_of(kb * tk, tk)
        lo = _dot_nt(qi_all, kilo_ref[0, pl.ds(kb0, tk), :])
        hi = _dot_nt(qi_all, kihi_ref[0, pl.ds(kb0, tk), :])
        score = jnp.zeros((tq, tk), F32)
        for hp in range(IDX_HEADS // 2):
            rows = slice(hp * tq, (hp + 1) * tq)
            score = score + wi[:, 2 * hp:2 * hp + 1] * jnp.maximum(lo[rows], 0.0)
            score = score + wi[:, 2 * hp + 1:2 * hp + 2] * jnp.maximum(hi[rows], 0.0)
        bits = pltpu.bitcast(score, jnp.int32)
        key = bits ^ ((bits >> 31) & 0x7FFFFFFF)
        if masked:
            key = jnp.where(_chunk_mask(q0, kb0, tq, tk), key, INT_MIN)
        key_scr[:, pl.ds(kb0, tk)] = key

    lax.fori_loop(0, n_full, lambda kb, c: (score_block(kb, False), c)[1], 0)
    lax.fori_loop(n_full, n_blocks, lambda kb, c: (score_block(kb, True), c)[1], 0)

    qpos = q0 + lax.broadcasted_iota(jnp.int32, (tq, LANES), 0)
    visible = jnp.minimum(((qpos >> CHUNK_SHIFT) + 1) * CHUNK, s_real)
    k_target = jnp.minimum(visible, topk).astype(F32)

    def count(mode):
        for r in range(tq // rg):
            rows = slice(r * rg, (r + 1) * rg)
            cand = cand_scr[rows, :]
            cval = c_scr[rows, :]

            def body(kb, acc):
                kb0 = pl.multiple_of(kb * tk, tk)
                for u in range(sub):
                    blk = key_scr[rows, pl.ds(kb0 + u * LANES, LANES)]
                    if mode == "ge":
                        hit = blk >= cand
                    else:
                        idx = kb0 + u * LANES + lax.broadcasted_iota(jnp.int32, (rg, LANES), 1)
                        hit = jnp.where(blk == cval, idx, jnp.int32(2 ** 30)) < cand
                    acc = acc + jnp.where(hit, 1.0, 0.0)
                return acc

            acc = lax.fori_loop(0, n_blocks, body, jnp.zeros((rg, LANES), F32))
            cnt_scr[rows, :] = jnp.broadcast_to(jnp.sum(acc, axis=1, keepdims=True), (rg, LANES))

    c_scr[...] = jnp.full((tq, LANES), INT_MIN, jnp.int32)
    cntc0 = jnp.full((tq, LANES), 1.0, F32) * (n_blocks * tk).astype(F32)

    def unsettled(cnt_c):
        return jnp.max(jnp.abs(cnt_c - k_target)) > 0.0

    def bit_cond(carry):
        b, pending, _ = carry
        return jnp.logical_and(b >= 0, pending)

    def bit_body(carry):
        b, _, cnt_c = carry
        cand = c_scr[...] + jnp.left_shift(jnp.int32(1), b)
        cand_scr[...] = cand
        count("ge")
        cnt = cnt_scr[...]
        take = cnt >= k_target
        c_scr[...] = jnp.where(take, cand, c_scr[...])
        cnt_c = jnp.where(take, cnt, cnt_c)
        return b - 1, unsettled(cnt_c), cnt_c

    _, ties, _ = lax.while_loop(bit_cond, bit_body, (jnp.int32(31), unsettled(cntc0), cntc0))

    @pl.when(ties)
    def _():
        cand_scr[...] = c_scr[...] + 1
        count("ge")
        need = k_target - cnt_scr[...]
        x = jnp.zeros((tq, LANES), jnp.int32)
        for bit in range(idx_bits - 1, -1, -1):
            cand_scr[...] = x + (1 << bit)
            count("eq_lt")
            x = jnp.where(cnt_scr[...] < need, x + (1 << bit), x)
        cand_scr[...] = x

        def demote(kb, carry):
            kb0 = pl.multiple_of(kb * tk, tk)
            cval = jnp.concatenate([c_scr[...]] * sub, axis=1)
            last = jnp.concatenate([cand_scr[...]] * sub, axis=1)
            idx = kb0 + lax.broadcasted_iota(jnp.int32, (tq, tk), 1)
            blk = key_scr[:, pl.ds(kb0, tk)]
            drop = jnp.where(blk == cval, idx, jnp.int32(-1)) > last
            key_scr[:, pl.ds(kb0, tk)] = jnp.where(drop, cval - 1, blk)
            return carry

        lax.fori_loop(0, n_blocks, demote, 0)

    m_scr[...] = jnp.full(m_scr.shape, MASKED, F32)
    l_scr[...] = jnp.zeros(l_scr.shape, F32)
    acc_scr[...] = jnp.zeros(acc_scr.shape, F32)

    def attend(kb, carry):
        kb0 = pl.multiple_of(kb * tk, tk)
        cval = jnp.concatenate([c_scr[...]] * sub, axis=1)
        bias = jnp.where(key_scr[:, pl.ds(kb0, tk)] >= cval, 0.0, MASKED)
        bias = jnp.concatenate([bias] * DSA_GROUP, axis=0)
        for c in range(DSA_KV_HEADS):
            qg = qd_ref[c * DSA_GROUP:(c + 1) * DSA_GROUP, 0].reshape(DSA_GROUP * tq, DSA_HEAD_DIM)
            cols = slice(c * DSA_HEAD_DIM, (c + 1) * DSA_HEAD_DIM)
            s = _dot_nt(qg, kd_ref[0, pl.ds(kb0, tk), cols]) * scale + bias
            _softmax_update(s, vd_ref[0, pl.ds(kb0, tk), cols], m_scr.at[c], l_scr.at[c], acc_scr.at[c])
        return carry

    lax.fori_loop(0, n_blocks, attend, 0)
    for c in range(DSA_KV_HEADS):
        o = acc_scr[c] / l_scr[c]
        for g in range(DSA_GROUP):
            h = c * DSA_GROUP + g
            o_ref[0, :, h * DSA_HEAD_DIM:(h + 1) * DSA_HEAD_DIM] = o[g * tq:(g + 1) * tq].astype(o_ref.dtype)


def _dsa_attention(qd, qi, wi, kd, vd, kilo, kihi, q_pos0, s_real, tq, tk):
    _, b, t, _ = qd.shape
    s = kd.shape[1]
    rg = min(tq, 32)
    kern = functools.partial(
        _dsa_kernel, tq=tq, tk=tk, rg=rg, q_pos0=q_pos0, s_real=s_real,
        topk=min(IDX_TOPK, s_real // 4), idx_bits=int(s).bit_length(), scale=DSA_HEAD_DIM ** -0.5)
    heads = lambda: pl.BlockSpec((8, 1, tq, 128), lambda bi, i: (0, bi, i, 0))
    keys = lambda w: _resident((1, s, w), lambda bi, i: (bi, 0, 0))
    return pl.pallas_call(
        kern, grid=(b, t // tq),
        in_specs=[heads(), heads(), pl.BlockSpec((1, tq, 128), lambda bi, i: (bi, i, 0)),
                  keys(256), keys(256), keys(128), keys(128)],
        out_specs=pl.BlockSpec((1, tq, DSA_HEADS * DSA_HEAD_DIM), lambda bi, i: (bi, i, 0)),
        out_shape=jax.ShapeDtypeStruct((b, t, DSA_HEADS * DSA_HEAD_DIM), BF16),
        scratch_shapes=[pltpu.VMEM((tq, s), jnp.int32),
                        pltpu.VMEM((tq, LANES), jnp.int32), pltpu.VMEM((tq, LANES), F32),
                        pltpu.VMEM((tq, LANES), jnp.int32),
                        pltpu.VMEM((DSA_KV_HEADS, DSA_GROUP * tq, LANES), F32),
                        pltpu.VMEM((DSA_KV_HEADS, DSA_GROUP * tq, LANES), F32),
                        pltpu.VMEM((DSA_KV_HEADS, DSA_GROUP * tq, DSA_HEAD_DIM), F32)],
        compiler_params=_params(2), name="dsa_attn",
    )(qd, qi, wi, kd, vd, kilo, kihi)


def _layer_norm(v, g, b):
    mu = jnp.mean(v, axis=-1, keepdims=True)
    d = v - mu
    var = jnp.mean(d * d, axis=-1, keepdims=True)
    return d * lax.rsqrt(var + LN_EPS) * g + b


def _outln_kernel(mla_ref, dsa_ref, x_ref, wo_ref, g_ref, b_ref, o_ref, *, alpha, half):
    a = _dot(mla_ref[...], wo_ref[0:half, :]) + _dot(dsa_ref[...], wo_ref[half:, :])
    o_ref[...] = _layer_norm(alpha * x_ref[...] + a, g_ref[...], b_ref[...])


def _out_ln(mla_o, dsa_o, x2d, wp, alpha, tm):
    n, d = x2d.shape
    half = mla_o.shape[1]
    row = lambda w: pl.BlockSpec((tm, w), lambda i: (i, 0))
    return pl.pallas_call(
        functools.partial(_outln_kernel, alpha=alpha, half=half), grid=(n // tm,),
        in_specs=[row(half), row(dsa_o.shape[1]), row(d),
                  _resident(wp["w_o"].shape, lambda i: (0, 0)),
                  _resident((1, d), lambda i: (0, 0)), _resident((1, d), lambda i: (0, 0))],
        out_specs=row(d), out_shape=jax.ShapeDtypeStruct((n, d), F32),
        compiler_params=_params(1), name="out_ln",
    )(mla_o, dsa_o, x2d, wp["w_o"], wp["ln1_g"], wp["ln1_b"])


def _ffn_kernel(x_ref, wg_ref, wu_ref, wd_ref, g_ref, b_ref, o_ref, xb_scr, acc_scr, *, alpha):
    j = pl.program_id(1)

    @pl.when(j == 0)
    def _():
        xb_scr[...] = x_ref[...].astype(BF16)
        acc_scr[...] = jnp.zeros(acc_scr.shape, F32)

    xb = xb_scr[...]
    gate = _dot(xb, wg_ref[...])
    up = _dot(xb, wu_ref[...])
    hidden = gate * (1.0 / (1.0 + jnp.exp(-gate))) * up
    acc_scr[...] += _dot(hidden.astype(BF16), wd_ref[...])

    @pl.when(j == pl.num_programs(1) - 1)
    def _():
        o_ref[...] = _layer_norm(alpha * x_ref[...] + acc_scr[...], g_ref[...], b_ref[...])


def _ffn_ln(x2d, wp, alpha, tm, tf):
    n, d = x2d.shape
    dff = wp["w_gate"].shape[1]
    return pl.pallas_call(
        functools.partial(_ffn_kernel, alpha=alpha), grid=(n // tm, dff // tf),
        in_specs=[pl.BlockSpec((tm, d), lambda i, j: (i, 0)),
                  pl.BlockSpec((d, tf), lambda i, j: (0, j)),
                  pl.BlockSpec((d, tf), lambda i, j: (0, j)),
                  pl.BlockSpec((tf, d), lambda i, j: (j, 0)),
                  _resident((1, d), lambda i, j: (0, 0)), _resident((1, d), lambda i, j: (0, 0))],
        out_specs=pl.BlockSpec((tm, d), lambda i, j: (i, 0)),
        out_shape=jax.ShapeDtypeStruct((n, d), F32),
        scratch_shapes=[pltpu.VMEM((tm, d), BF16), pltpu.VMEM((tm, d), F32)],
        compiler_params=_params(2), name="ffn_ln",
    )(x2d, wp["w_gate"], wp["w_up"], wp["w_down"], wp["ln2_g"], wp["ln2_b"])


def _pack_weights(w_in, w_uq, g_q, w_ukv, g_kv, w_o, ln1_g, ln1_b, w_gate, w_up, w_down, ln2_g, ln2_b):
    d = w_in.shape[0]
    splits = (MLA_Q_LORA, MLA_KV_LORA, MLA_ROPE, DSA_HEADS * DSA_HEAD_DIM, DSA_KV_HEADS * DSA_HEAD_DIM,
              DSA_KV_HEADS * DSA_HEAD_DIM, IDX_HEADS * IDX_DIM, IDX_DIM, IDX_HEADS)
    offs = np.cumsum(splits)[:-1].tolist()
    c_q, c_kv, k_r, q_d, k_d, v_d, q_i, k_i, w_i = jnp.split(w_in, offs, axis=1)
    z = lambda n: jnp.zeros((d, n), w_in.dtype)
    w_in_p = jnp.concatenate(
        [c_q, c_kv, q_d, k_d, v_d, q_i, k_r, z(64), k_i, z(64), z(64), k_i, w_i, z(128 - IDX_HEADS)], axis=1)
    assert w_in_p.shape[1] == _IN_COLS_P
    w_uq_p = jnp.pad(w_uq, ((0, 0), (0, 0), (0, 256 - MLA_NOPE - MLA_ROPE)))
    return {
        "w_in": w_in_p.astype(BF16),
        "w_uq": w_uq_p.reshape(MLA_Q_LORA, MLA_HEADS * 256).astype(BF16),
        "g_q": g_q.reshape(1, -1), "g_kv": g_kv.reshape(1, -1),
        "w_uk": w_ukv[:, :, :MLA_NOPE].reshape(MLA_KV_LORA, MLA_HEADS * MLA_NOPE).astype(BF16),
        "w_uv": w_ukv[:, :, MLA_NOPE:].reshape(MLA_KV_LORA, MLA_HEADS * MLA_V).astype(BF16),
        "w_o": w_o.astype(BF16),
        "ln1_g": ln1_g.reshape(1, -1), "ln1_b": ln1_b.reshape(1, -1),
        "w_gate": w_gate.astype(BF16), "w_up": w_up.astype(BF16), "w_down": w_down.astype(BF16),
        "ln2_g": ln2_g.reshape(1, -1), "ln2_b": ln2_b.reshape(1, -1),
    }


def _round_up(n, m):
    return -(-n // m) * m


def _trunk_layer(x, q_pos0, past, wp, alpha, cfg):
    b, t, d = x.shape
    n = b * t
    x2d = x.reshape(n, d)
    pos = jnp.tile(q_pos0 + jnp.arange(t, dtype=jnp.int32), b)
    (ckv, kpe, kd, vd, ki, q_mla, qd, qi, wi, kpeb, kdb, vdb, kilo, kihi) = _proj(x2d, pos, wp, cfg["tm_proj"])
    new_rows = (ckv.reshape(b, t, -1), kpe.reshape(b, t, -1),
                kd.reshape(b, t, DSA_KV_HEADS, DSA_HEAD_DIM), vd.reshape(b, t, DSA_KV_HEADS, DSA_HEAD_DIM),
                ki.reshape(b, t, -1))

    per_b = lambda a: a.reshape(b, t, a.shape[-1])
    ckv_all, kpeb_all, kdb_all, vdb_all, kilo_all, kihi_all = map(per_b, (ckv, kpeb, kdb, vdb, kilo, kihi))
    s_real = t
    if past is not None:
        p_ckv, p_kpe, p_kd, p_vd, p_ki = past
        s_real = p_ckv.shape[1] + t
        z64 = jnp.zeros(p_kpe.shape, BF16)
        cat = lambda c, nw: jnp.concatenate([c, nw], axis=1)
        ckv_all = cat(p_ckv, ckv_all)
        kpeb_all = cat(jnp.concatenate([p_kpe.astype(BF16), z64], axis=-1), kpeb_all)
        kdb_all = cat(p_kd.reshape(b, -1, 256).astype(BF16), kdb_all)
        vdb_all = cat(p_vd.reshape(b, -1, 256).astype(BF16), vdb_all)
        kilo_all = cat(jnp.concatenate([p_ki.astype(BF16), z64], axis=-1), kilo_all)
        kihi_all = cat(jnp.concatenate([z64, p_ki.astype(BF16)], axis=-1), kihi_all)
    s_pad = _round_up(s_real, cfg["s_align"])
    if s_pad != s_real:
        padk = lambda a: jnp.pad(a, ((0, 0), (0, s_pad - s_real), (0, 0)))
        ckv_all, kpeb_all, kdb_all, vdb_all, kilo_all, kihi_all = map(
            padk, (ckv_all, kpeb_all, kdb_all, vdb_all, kilo_all, kihi_all))

    k_mla, v_mla = _kv_up(ckv_all.reshape(b * s_pad, -1), kpeb_all.reshape(b * s_pad, -1), wp, cfg["tm_kv"])
    k_mla = k_mla.reshape(MLA_HEADS, b, s_pad, 256)
    v_mla = v_mla.reshape(MLA_HEADS, b, s_pad, 128)
    mla_o = _mla_attention(q_mla.reshape(MLA_HEADS, b, t, 256), k_mla, v_mla, q_pos0,
                           cfg["tq_mla"], cfg["tk_mla"])
    dsa_o = _dsa_attention(qd.reshape(DSA_HEADS, b, t, 128), qi.reshape(IDX_HEADS // 2, b, t, 128),
                           wi.reshape(b, t, 128), kdb_all, vdb_all, kilo_all, kihi_all,
                           q_pos0, s_real, cfg["tq_dsa"], cfg["tk_dsa"])
    x1 = _out_ln(mla_o.reshape(n, -1), dsa_o.reshape(n, -1), x2d, wp, alpha, cfg["tm_out"])
    y = _ffn_ln(x1, wp, alpha, cfg["tm_ffn"], cfg["tf_ffn"])
    return y.reshape(b, t, d), new_rows


_PROMPT_CFG = dict(tm_proj=256, tm_kv=256, tq_mla=512, tk_mla=256, tq_dsa=128, tk_dsa=256,
                   s_align=256, tm_out=256, tm_ffn=512, tf_ffn=512)
_SAMPLE_CFG = dict(tm_proj=256, tm_kv=256, tq_mla=64, tk_mla=256, tq_dsa=64, tk_dsa=256,
                   s_align=256, tm_out=256, tm_ffn=512, tf_ffn=512)


def kernel(x_prompt, x_sample, cache_mla_ckv, cache_mla_kpe, cache_dsa_k, cache_dsa_v, cache_idx_k, w_in, w_uq, mla_q_norm_g, w_ukv, mla_kv_norm_g, w_o, ln1_g, ln1_b, w_gate, w_up, w_down, ln2_g, ln2_b):
    depth = w_in.shape[0]
    alpha = (2 * depth) ** 0.25
    past_len = cache_mla_ckv.shape[2]
    y_p, y_s = x_prompt, x_sample
    rows_p, rows_s = [], []
    for l in range(depth):
        wp = _pack_weights(w_in[l], w_uq[l], mla_q_norm_g[l], w_ukv[l], mla_kv_norm_g[l], w_o[l],
                           ln1_g[l], ln1_b[l], w_gate[l], w_up[l], w_down[l], ln2_g[l], ln2_b[l])
        y_p, r_p = _trunk_layer(y_p, 0, None, wp, alpha, _PROMPT_CFG)
        past = (cache_mla_ckv[l], cache_mla_kpe[l], cache_dsa_k[l], cache_dsa_v[l], cache_idx_k[l])
        y_s, r_s = _trunk_layer(y_s, past_len, past, wp, alpha, _SAMPLE_CFG)
        rows_p.append(r_p)
        rows_s.append(r_s)
    stack = lambda rows, i: jnp.stack([r[i] for r in rows], axis=0)
    return (y_p, y_s,
            stack(rows_p, 0), stack(rows_p, 1), stack(rows_p, 2), stack(rows_p, 3), stack(rows_p, 4),
            stack(rows_s, 0), stack(rows_s, 1), stack(rows_s, 2), stack(rows_s, 3), stack(rows_s, 4))
```

```python
import functools

import numpy as np
import jax
import jax.numpy as jnp
from jax import lax
from jax.experimental import pallas as pl
from jax.experimental.pallas import tpu as pltpu

CHUNK = 64
CHUNK_SHIFT = 6
ROPE_THETA = 10000.0
MLA_HEADS = 8
MLA_Q_LORA = 512
MLA_KV_LORA = 512
MLA_NOPE = 128
MLA_ROPE = 64
MLA_V = 128
DSA_HEADS = 8
DSA_KV_HEADS = 2
DSA_GROUP = DSA_HEADS // DSA_KV_HEADS
DSA_HEAD_DIM = 128
IDX_HEADS = 16
IDX_DIM = 64
IDX_TOPK = 256
LN_EPS = 1e-5
RMS_EPS = 1e-6

LANES = 128
MASKED = -1e30
INT_MIN = -2 ** 31
VMEM_LIMIT = 56 * 1024 * 1024

_C_CQ = 0
_C_CKV = 512
_C_QD = 1024
_C_KD = 2048
_C_VD = 2304
_C_QI = 2560
_C_KR = 3584
_C_KILO = 3712
_C_KIHI = 3840
_C_WI = 3968
_IN_COLS_P = 4096

F32 = jnp.float32
BF16 = jnp.bfloat16


def _dot(a, b):
    return jnp.dot(a, b, preferred_element_type=F32)


def _dot_nt(a, b):
    return lax.dot_general(a, b, (((1,), (1,)), ((), ())), preferred_element_type=F32)


def _params(n_axes, vmem=VMEM_LIMIT):
    return pltpu.CompilerParams(dimension_semantics=("arbitrary",) * n_axes, vmem_limit_bytes=vmem)


def _round_up(n, m):
    return -(-n // m) * m


def _resident(shape, index_map):
    return pl.BlockSpec(shape, index_map, pipeline_mode=pl.Buffered(1))


def _rope128(x, cos, sin_signed):
    return x * cos + pltpu.roll(x, 64, 1) * sin_signed


def _rope64(x, cos, sin_signed, first_half):
    rot = jnp.where(first_half, pltpu.roll(x, 96, 1), pltpu.roll(x, 32, 1))
    return x * cos + rot * sin_signed


def _rope_tables(pos):
    pos = pos.astype(F32)[:, None]
    lane = np.arange(LANES)

    def tables(dim):
        half = dim // 2
        inv = 1.0 / (ROPE_THETA ** (jnp.arange(half, dtype=F32) / half))
        ang = pos * inv[None, :]
        cos, sin = jnp.cos(ang), jnp.sin(ang)
        idx = lane % half
        sign = np.where((lane % dim) < half, -1.0, 1.0).astype(np.float32)
        return cos[:, idx], sin[:, idx] * sign[None, :]

    c128, s128 = tables(128)
    c64, s64 = tables(64)
    return c128, s128, c64, s64


def _proj_kernel(x_ref, w_ref, wuq_ref, gq_ref, gkv_ref, c128_ref, s128_ref, c64_ref, s64_ref,
                 ckv_ref, kpe_ref, kd_ref, vd_ref, ki_ref,
                 q_ref, qd_ref, qi_ref, wi_ref, kpeb_ref, kdb_ref, vdb_ref, kilo_ref, kihi_ref):
    xb = x_ref[...].astype(BF16)
    c128, s128 = c128_ref[...], s128_ref[...]
    c64, s64 = c64_ref[...], s64_ref[...]
    first_half = (lax.broadcasted_iota(jnp.int32, c64.shape, 1) % 64) < 32

    def seg(a, b):
        return _dot(xb, w_ref[:, a:b])

    def rms(v, g):
        return v * lax.rsqrt(jnp.mean(v * v, axis=-1, keepdims=True) + RMS_EPS) * g

    qn = rms(seg(_C_CQ, _C_CQ + MLA_Q_LORA), gq_ref[...]).astype(BF16)
    for h in range(MLA_HEADS):
        qh = _dot(qn, wuq_ref[:, h * 256:(h + 1) * 256])
        q_ref[h, :, 0:128] = qh[:, 0:128].astype(BF16)
        q_ref[h, :, 128:256] = _rope64(qh[:, 128:256], c64, s64, first_half).astype(BF16)

    ckv_ref[...] = rms(seg(_C_CKV, _C_CKV + MLA_KV_LORA), gkv_ref[...])

    kr = _rope64(seg(_C_KR, _C_KR + 128), c64, s64, first_half)
    kpe_ref[...] = kr[:, 0:MLA_ROPE]
    kpeb_ref[...] = kr.astype(BF16)

    for h in range(DSA_HEADS):
        a = _C_QD + h * 128
        qd_ref[h] = _rope128(seg(a, a + 128), c128, s128).astype(BF16)
    for c in range(DSA_KV_HEADS):
        a = _C_KD + c * 128
        kdc = _rope128(seg(a, a + 128), c128, s128)
        kd_ref[:, c * 128:(c + 1) * 128] = kdc
        kdb_ref[:, c * 128:(c + 1) * 128] = kdc.astype(BF16)
    vd = seg(_C_VD, _C_VD + 256)
    vd_ref[...] = vd
    vdb_ref[...] = vd.astype(BF16)

    for hp in range(IDX_HEADS // 2):
        a = _C_QI + hp * 128
        qi_ref[hp] = _rope64(seg(a, a + 128), c64, s64, first_half).astype(BF16)
    kilo = _rope64(seg(_C_KILO, _C_KILO + 128), c64, s64, first_half)
    ki_ref[...] = kilo[:, 0:IDX_DIM]
    kilo_ref[...] = kilo.astype(BF16)
    kihi_ref[...] = _rope64(seg(_C_KIHI, _C_KIHI + 128), c64, s64, first_half).astype(BF16)
    wi_ref[...] = seg(_C_WI, _C_WI + 128) * (IDX_DIM ** -0.5 * IDX_HEADS ** -0.5)


def _proj(x2d, pos, wp, tm):
    n, d = x2d.shape
    c128, s128, c64, s64 = _rope_tables(pos)
    row = lambda w: pl.BlockSpec((tm, w), lambda i: (i, 0))
    heads = lambda nh, w: pl.BlockSpec((nh, tm, w), lambda i: (0, i, 0))
    out_shapes = (
        jax.ShapeDtypeStruct((n, MLA_KV_LORA), F32),
        jax.ShapeDtypeStruct((n, MLA_ROPE), F32),
        jax.ShapeDtypeStruct((n, 256), F32),
        jax.ShapeDtypeStruct((n, 256), F32),
        jax.ShapeDtypeStruct((n, IDX_DIM), F32),
        jax.ShapeDtypeStruct((MLA_HEADS, n, 256), BF16),
        jax.ShapeDtypeStruct((DSA_HEADS, n, 128), BF16),
        jax.ShapeDtypeStruct((IDX_HEADS // 2, n, 128), BF16),
        jax.ShapeDtypeStruct((n, 128), F32),
        jax.ShapeDtypeStruct((n, 128), BF16),
        jax.ShapeDtypeStruct((n, 256), BF16),
        jax.ShapeDtypeStruct((n, 256), BF16),
        jax.ShapeDtypeStruct((n, 128), BF16),
        jax.ShapeDtypeStruct((n, 128), BF16),
    )
    out_specs = (row(MLA_KV_LORA), row(MLA_ROPE), row(256), row(256), row(IDX_DIM),
                 heads(MLA_HEADS, 256), heads(DSA_HEADS, 128), heads(IDX_HEADS // 2, 128),
                 row(128), row(128), row(256), row(256), row(128), row(128))
    in_specs = [row(d),
                _resident((d, _IN_COLS_P), lambda i: (0, 0)),
                _resident((MLA_Q_LORA, MLA_HEADS * 256), lambda i: (0, 0)),
                _resident((1, MLA_Q_LORA), lambda i: (0, 0)),
                _resident((1, MLA_KV_LORA), lambda i: (0, 0)),
                row(128), row(128), row(128), row(128)]
    return pl.pallas_call(
        _proj_kernel, grid=(n // tm,), in_specs=in_specs, out_specs=out_specs, out_shape=out_shapes,
        compiler_params=_params(1), name="proj",
    )(x2d, wp["w_in"], wp["w_uq"], wp["g_q"], wp["g_kv"], c128, s128, c64, s64)


def _kvup_kernel(ckv_ref, kpeb_ref, wk_ref, wv_ref, k_ref, v_ref):
    cb = ckv_ref[...].astype(BF16)
    kpe = kpeb_ref[...]
    for h in range(MLA_HEADS):
        k_ref[h, :, 0:128] = _dot(cb, wk_ref[:, h * 128:(h + 1) * 128]).astype(BF16)
        k_ref[h, :, 128:256] = kpe
        v_ref[h] = _dot(cb, wv_ref[:, h * 128:(h + 1) * 128]).astype(BF16)


def _kv_up(ckv2d, kpeb2d, wp, tm):
    n = ckv2d.shape[0]
    return pl.pallas_call(
        _kvup_kernel, grid=(n // tm,),
        in_specs=[pl.BlockSpec((tm, MLA_KV_LORA), lambda i: (i, 0)),
                  pl.BlockSpec((tm, 128), lambda i: (i, 0)),
                  _resident((MLA_KV_LORA, MLA_HEADS * MLA_NOPE), lambda i: (0, 0)),
                  _resident((MLA_KV_LORA, MLA_HEADS * MLA_V), lambda i: (0, 0))],
        out_specs=(pl.BlockSpec((MLA_HEADS, tm, 256), lambda i: (0, i, 0)),
                   pl.BlockSpec((MLA_HEADS, tm, 128), lambda i: (0, i, 0))),
        out_shape=(jax.ShapeDtypeStruct((MLA_HEADS, n, 256), BF16),
                   jax.ShapeDtypeStruct((MLA_HEADS, n, 128), BF16)),
        compiler_params=_params(1), name="kv_up",
    )(ckv2d, kpeb2d, wp["w_uk"], wp["w_uv"])


def _softmax_update(s, v, m_scr, l_scr, acc_scr):
    reps = s.shape[1] // LANES
    m_prev = m_scr[...]
    m_next = jnp.maximum(m_prev, jnp.max(s, axis=1, keepdims=True))
    p = jnp.exp(s - jnp.concatenate([m_next] * reps, axis=1))
    alpha = jnp.exp(m_prev - m_next)
    l_scr[...] = alpha * l_scr[...] + jnp.sum(p, axis=1, keepdims=True)
    acc_scr[...] = acc_scr[...] * alpha + _dot(p.astype(BF16), v)
    m_scr[...] = m_next


def _chunk_mask(q0, kb0, tq, tk):
    qch = (q0 + lax.broadcasted_iota(jnp.int32, (tq, 1), 0)) >> CHUNK_SHIFT
    kch = (kb0 + lax.broadcasted_iota(jnp.int32, (1, tk), 1)) >> CHUNK_SHIFT
    return kch <= qch


def _diag_width(tq):
    return _round_up(tq, 256)


def _visible_key_steps(q0, tq, big, fn):
    td = _diag_width(tq)
    d0 = (q0 // td) * td
    tmid = min(big, 512 if td % 512 == 0 else 256)
    n_big = d0 // big

    def run(lo, hi, width):
        lax.fori_loop(lo, hi, lambda i, c: (fn(pl.multiple_of(i * width, width), width, False), c)[1], 0)

    run(0, n_big, big)
    if tmid < big:
        run(n_big * (big // tmid), d0 // tmid, tmid)
    fn(pl.multiple_of(d0, td), td, True)
    return d0 + td


def _mla_kernel(q_ref, k_ref, v_ref, o_ref, m_scr, l_scr, acc_scr, *, tq, big, q_pos0, scale):
    q0 = q_pos0 + pl.program_id(2) * tq
    m_scr[...] = jnp.full(m_scr.shape, MASKED, F32)
    l_scr[...] = jnp.zeros(l_scr.shape, F32)
    acc_scr[...] = jnp.zeros(acc_scr.shape, F32)
    q = q_ref[0, 0]

    def step(k0, width, masked):
        s = _dot_nt(q, k_ref[0, 0, pl.ds(k0, width), :]) * scale
        if masked:
            s = jnp.where(_chunk_mask(q0, k0, tq, width), s, MASKED)
        _softmax_update(s, v_ref[0, 0, pl.ds(k0, width), :], m_scr, l_scr, acc_scr)

    _visible_key_steps(q0, tq, big, step)
    o_ref[0] = (acc_scr[...] / l_scr[...]).astype(o_ref.dtype)


def _mla_attention(q, k, v, q_pos0, tq, big):
    nh, b, t, _ = q.shape
    s = k.shape[2]
    kern = functools.partial(_mla_kernel, tq=tq, big=big, q_pos0=q_pos0,
                             scale=(MLA_NOPE + MLA_ROPE) ** -0.5)
    return pl.pallas_call(
        kern, grid=(b, nh, t // tq),
        in_specs=[pl.BlockSpec((1, 1, tq, 256), lambda bi, h, i: (h, bi, i, 0)),
                  pl.BlockSpec((1, 1, s, 256), lambda bi, h, i: (h, bi, 0, 0)),
                  pl.BlockSpec((1, 1, s, 128), lambda bi, h, i: (h, bi, 0, 0))],
        out_specs=pl.BlockSpec((1, tq, 128), lambda bi, h, i: (bi, i, h)),
        out_shape=jax.ShapeDtypeStruct((b, t, nh * MLA_V), BF16),
        scratch_shapes=[pltpu.VMEM((tq, LANES), F32), pltpu.VMEM((tq, LANES), F32),
                        pltpu.VMEM((tq, MLA_V), F32)],
        compiler_params=_params(3), name="mla_attn",
    )(q, k, v)


def _dsa_kernel(qd_ref, qi_ref, wi_ref, kd_ref, vd_ref, kilo_ref, kihi_ref, o_ref,
                key_scr, c_scr, cnt_scr, cand_scr, m_scr, l_scr, acc_scr,
                *, tq, big, q_pos0, s_real, topk, idx_bits, scale):
    q0 = q_pos0 + pl.program_id(1) * tq
    sub = big // LANES

    qi_all = qi_ref[:, 0].reshape(IDX_HEADS // 2 * tq, LANES)
    wi = wi_ref[0]

    def score_step(k0, width, masked):
        lo = _dot_nt(qi_all, kilo_ref[0, pl.ds(k0, width), :])
        hi = _dot_nt(qi_all, kihi_ref[0, pl.ds(k0, width), :])
        score = jnp.zeros((tq, width), F32)
        for hp in range(IDX_HEADS // 2):
            rows = slice(hp * tq, (hp + 1) * tq)
            score = score + wi[:, 2 * hp:2 * hp + 1] * jnp.maximum(lo[rows], 0.0)
            score = score + wi[:, 2 * hp + 1:2 * hp + 2] * jnp.maximum(hi[rows], 0.0)
        bits = pltpu.bitcast(score, jnp.int32)
        key = bits ^ ((bits >> 31) & 0x7FFFFFFF)
        if masked:
            key = jnp.where(_chunk_mask(q0, k0, tq, width), key, INT_MIN)
        key_scr[:, pl.ds(k0, width)] = key

    k_end = _visible_key_steps(q0, tq, big, score_step)
    n_sb = (k_end + big - 1) // big

    def fill(i, carry):
        key_scr[:, pl.ds(pl.multiple_of(i * 256, 256), 256)] = jnp.full((tq, 256), INT_MIN, jnp.int32)
        return carry

    lax.fori_loop(k_end // 256, n_sb * (big // 256), fill, 0)

    qpos = q0 + lax.broadcasted_iota(jnp.int32, (tq, LANES), 0)
    visible = jnp.minimum(((qpos >> CHUNK_SHIFT) + 1) * CHUNK, s_real)
    k_target = jnp.minimum(visible, topk).astype(F32)

    def count(mode):
        cand = cand_scr[...]
        cval = c_scr[...]

        def body(sb, acc):
            k0 = pl.multiple_of(sb * big, big)
            for u in range(sub):
                blk = key_scr[:, pl.ds(k0 + u * LANES, LANES)]
                if mode == "ge":
                    hit = blk >= cand
                else:
                    idx = k0 + u * LANES + lax.broadcasted_iota(jnp.int32, (tq, LANES), 1)
                    hit = jnp.where(blk == cval, idx, jnp.int32(2 ** 30)) < cand
                acc = acc + jnp.where(hit, 1.0, 0.0)
            return acc

        acc = lax.fori_loop(0, n_sb, body, jnp.zeros((tq, LANES), F32))
        cnt_scr[...] = jnp.broadcast_to(jnp.sum(acc, axis=1, keepdims=True), (tq, LANES))

    c_scr[...] = jnp.full((tq, LANES), INT_MIN, jnp.int32)
    cntc0 = jnp.full((tq, LANES), 1.0, F32) * (n_sb * big).astype(F32)

    def unsettled(cnt_c):
        return jnp.max(jnp.abs(cnt_c - k_target)) > 0.0

    def bit_cond(carry):
        b, pending, _ = carry
        return jnp.logical_and(b >= 0, pending)

    def bit_body(carry):
        b, _, cnt_c = carry
        cand = c_scr[...] + jnp.left_shift(jnp.int32(1), b)
        cand_scr[...] = cand
        count("ge")
        cnt = cnt_scr[...]
        take = cnt >= k_target
        c_scr[...] = jnp.where(take, cand, c_scr[...])
        cnt_c = jnp.where(take, cnt, cnt_c)
        return b - 1, unsettled(cnt_c), cnt_c

    _, ties, _ = lax.while_loop(bit_cond, bit_body, (jnp.int32(31), unsettled(cntc0), cntc0))

    @pl.when(ties)
    def _():
        cand_scr[...] = c_scr[...] + 1
        count("ge")
        need = k_target - cnt_scr[...]
        x = jnp.zeros((tq, LANES), jnp.int32)
        for bit in range(idx_bits - 1, -1, -1):
            cand_scr[...] = x + (1 << bit)
            count("eq_lt")
            x = jnp.where(cnt_scr[...] < need, x + (1 << bit), x)
        cand_scr[...] = x

        def demote(kb, carry):
            kb0 = pl.multiple_of(kb * 256, 256)
            cval = jnp.concatenate([c_scr[...]] * 2, axis=1)
            last = jnp.concatenate([cand_scr[...]] * 2, axis=1)
            idx = kb0 + lax.broadcasted_iota(jnp.int32, (tq, 256), 1)
            blk = key_scr[:, pl.ds(kb0, 256)]
            drop = jnp.where(blk == cval, idx, jnp.int32(-1)) > last
            key_scr[:, pl.ds(kb0, 256)] = jnp.where(drop, cval - 1, blk)
            return carry

        lax.fori_loop(0, k_end // 256, demote, 0)

    m_scr[...] = jnp.full(m_scr.shape, MASKED, F32)
    l_scr[...] = jnp.zeros(l_scr.shape, F32)
    acc_scr[...] = jnp.zeros(acc_scr.shape, F32)

    def attend(k0, width, masked):
        del masked
        cval = jnp.concatenate([c_scr[...]] * (width // LANES), axis=1)
        bias = jnp.where(key_scr[:, pl.ds(k0, width)] >= cval, 0.0, MASKED)
        bias = jnp.concatenate([bias] * DSA_GROUP, axis=0)
        for c in range(DSA_KV_HEADS):
            qg = qd_ref[c * DSA_GROUP:(c + 1) * DSA_GROUP, 0].reshape(DSA_GROUP * tq, DSA_HEAD_DIM)
            cols = slice(c * DSA_HEAD_DIM, (c + 1) * DSA_HEAD_DIM)
            s = _dot_nt(qg, kd_ref[0, pl.ds(k0, width), cols]) * scale + bias
            _softmax_update(s, vd_ref[0, pl.ds(k0, width), cols], m_scr.at[c], l_scr.at[c], acc_scr.at[c])

    _visible_key_steps(q0, tq, big, attend)
    for c in range(DSA_KV_HEADS):
        o = acc_scr[c] / l_scr[c]
        for g in range(DSA_GROUP):
            h = c * DSA_GROUP + g
            o_ref[0, :, h * DSA_HEAD_DIM:(h + 1) * DSA_HEAD_DIM] = o[g * tq:(g + 1) * tq].astype(o_ref.dtype)


def _dsa_attention(qd, qi, wi, kd, vd, kilo, kihi, q_pos0, s_real, tq, big):
    _, b, t, _ = qd.shape
    s = kd.shape[1]
    kern = functools.partial(
        _dsa_kernel, tq=tq, big=big, q_pos0=q_pos0, s_real=s_real,
        topk=min(IDX_TOPK, s_real // 4), idx_bits=int(s).bit_length(), scale=DSA_HEAD_DIM ** -0.5)
    heads = lambda: pl.BlockSpec((8, 1, tq, 128), lambda bi, i: (0, bi, i, 0))
    keys = lambda w: _resident((1, s, w), lambda bi, i: (bi, 0, 0))
    return pl.pallas_call(
        kern, grid=(b, t // tq),
        in_specs=[heads(), heads(), pl.BlockSpec((1, tq, 128), lambda bi, i: (bi, i, 0)),
                  keys(256), keys(256), keys(128), keys(128)],
        out_specs=pl.BlockSpec((1, tq, DSA_HEADS * DSA_HEAD_DIM), lambda bi, i: (bi, i, 0)),
        out_shape=jax.ShapeDtypeStruct((b, t, DSA_HEADS * DSA_HEAD_DIM), BF16),
        scratch_shapes=[pltpu.VMEM((tq, s), jnp.int32),
                        pltpu.VMEM((tq, LANES), jnp.int32), pltpu.VMEM((tq, LANES), F32),
                        pltpu.VMEM((tq, LANES), jnp.int32),
                        pltpu.VMEM((DSA_KV_HEADS, DSA_GROUP * tq, LANES), F32),
                        pltpu.VMEM((DSA_KV_HEADS, DSA_GROUP * tq, LANES), F32),
                        pltpu.VMEM((DSA_KV_HEADS, DSA_GROUP * tq, DSA_HEAD_DIM), F32)],
        compiler_params=_params(2), name="dsa_attn",
    )(qd, qi, wi, kd, vd, kilo, kihi)


def _layer_norm(v, g, b):
    mu = jnp.mean(v, axis=-1, keepdims=True)
    d = v - mu
    var = jnp.mean(d * d, axis=-1, keepdims=True)
    return d * lax.rsqrt(var + LN_EPS) * g + b


def _outln_kernel(mla_ref, dsa_ref, x_ref, wo_ref, g_ref, b_ref, o_ref, *, alpha, half):
    a = _dot(mla_ref[...], wo_ref[0:half, :]) + _dot(dsa_ref[...], wo_ref[half:, :])
    o_ref[...] = _layer_norm(alpha * x_ref[...] + a, g_ref[...], b_ref[...])


def _out_ln(mla_o, dsa_o, x2d, wp, alpha, tm):
    n, d = x2d.shape
    half = mla_o.shape[1]
    row = lambda w: pl.BlockSpec((tm, w), lambda i: (i, 0))
    return pl.pallas_call(
        functools.partial(_outln_kernel, alpha=alpha, half=half), grid=(n // tm,),
        in_specs=[row(half), row(dsa_o.shape[1]), row(d),
                  _resident(wp["w_o"].shape, lambda i: (0, 0)),
                  _resident((1, d), lambda i: (0, 0)), _resident((1, d), lambda i: (0, 0))],
        out_specs=row(d), out_shape=jax.ShapeDtypeStruct((n, d), F32),
        compiler_params=_params(1), name="out_ln",
    )(mla_o, dsa_o, x2d, wp["w_o"], wp["ln1_g"], wp["ln1_b"])


def _ffn_kernel(x_ref, wg_ref, wu_ref, wd_ref, g_ref, b_ref, o_ref, xb_scr, acc_scr, *, alpha):
    j = pl.program_id(1)

    @pl.when(j == 0)
    def _():
        xb_scr[...] = x_ref[...].astype(BF16)
        acc_scr[...] = jnp.zeros(acc_scr.shape, F32)

    xb = xb_scr[...]
    gate = _dot(xb, wg_ref[...])
    up = _dot(xb, wu_ref[...])
    hidden = gate * (1.0 / (1.0 + jnp.exp(-gate))) * up
    acc_scr[...] += _dot(hidden.astype(BF16), wd_ref[...])

    @pl.when(j == pl.num_programs(1) - 1)
    def _():
        o_ref[...] = _layer_norm(alpha * x_ref[...] + acc_scr[...], g_ref[...], b_ref[...])


def _ffn_ln(x2d, wp, alpha, tm, tf):
    n, d = x2d.shape
    dff = wp["w_gate"].shape[1]
    return pl.pallas_call(
        functools.partial(_ffn_kernel, alpha=alpha), grid=(n // tm, dff // tf),
        in_specs=[pl.BlockSpec((tm, d), lambda i, j: (i, 0)),
                  pl.BlockSpec((d, tf), lambda i, j: (0, j)),
                  pl.BlockSpec((d, tf), lambda i, j: (0, j)),
                  pl.BlockSpec((tf, d), lambda i, j: (j, 0)),
                  _resident((1, d), lambda i, j: (0, 0)), _resident((1, d), lambda i, j: (0, 0))],
        out_specs=pl.BlockSpec((tm, d), lambda i, j: (i, 0)),
        out_shape=jax.ShapeDtypeStruct((n, d), F32),
        scratch_shapes=[pltpu.VMEM((tm, d), BF16), pltpu.VMEM((tm, d), F32)],
        compiler_params=_params(2), name="ffn_ln",
    )(x2d, wp["w_gate"], wp["w_up"], wp["w_down"], wp["ln2_g"], wp["ln2_b"])


def _pack_weights(w_in, w_uq, g_q, w_ukv, g_kv, w_o, ln1_g, ln1_b, w_gate, w_up, w_down, ln2_g, ln2_b):
    d = w_in.shape[0]
    splits = (MLA_Q_LORA, MLA_KV_LORA, MLA_ROPE, DSA_HEADS * DSA_HEAD_DIM, DSA_KV_HEADS * DSA_HEAD_DIM,
              DSA_KV_HEADS * DSA_HEAD_DIM, IDX_HEADS * IDX_DIM, IDX_DIM, IDX_HEADS)
    offs = np.cumsum(splits)[:-1].tolist()
    c_q, c_kv, k_r, q_d, k_d, v_d, q_i, k_i, w_i = jnp.split(w_in, offs, axis=1)
    z = lambda n: jnp.zeros((d, n), w_in.dtype)
    w_in_p = jnp.concatenate(
        [c_q, c_kv, q_d, k_d, v_d, q_i, k_r, z(64), k_i, z(64), z(64), k_i, w_i, z(128 - IDX_HEADS)], axis=1)
    assert w_in_p.shape[1] == _IN_COLS_P
    w_uq_p = jnp.pad(w_uq, ((0, 0), (0, 0), (0, 256 - MLA_NOPE - MLA_ROPE)))
    return {
        "w_in": w_in_p.astype(BF16),
        "w_uq": w_uq_p.reshape(MLA_Q_LORA, MLA_HEADS * 256).astype(BF16),
        "g_q": g_q.reshape(1, -1), "g_kv": g_kv.reshape(1, -1),
        "w_uk": w_ukv[:, :, :MLA_NOPE].reshape(MLA_KV_LORA, MLA_HEADS * MLA_NOPE).astype(BF16),
        "w_uv": w_ukv[:, :, MLA_NOPE:].reshape(MLA_KV_LORA, MLA_HEADS * MLA_V).astype(BF16),
        "w_o": w_o.astype(BF16),
        "ln1_g": ln1_g.reshape(1, -1), "ln1_b": ln1_b.reshape(1, -1),
        "w_gate": w_gate.astype(BF16), "w_up": w_up.astype(BF16), "w_down": w_down.astype(BF16),
        "ln2_g": ln2_g.reshape(1, -1), "ln2_b": ln2_b.reshape(1, -1),
    }


def _trunk_layer(x, q_pos0, past, wp, alpha, cfg):
    b, t, d = x.shape
    n = b * t
    x2d = x.reshape(n, d)
    pos = jnp.tile(q_pos0 + jnp.arange(t, dtype=jnp.int32), b)
    (ckv, kpe, kd, vd, ki, q_mla, qd, qi, wi, kpeb, kdb, vdb, kilo, kihi) = _proj(x2d, pos, wp, cfg["tm_proj"])
    new_rows = (ckv.reshape(b, t, -1), kpe.reshape(b, t, -1),
                kd.reshape(b, t, DSA_KV_HEADS, DSA_HEAD_DIM), vd.reshape(b, t, DSA_KV_HEADS, DSA_HEAD_DIM),
                ki.reshape(b, t, -1))

    per_b = lambda a: a.reshape(b, t, a.shape[-1])
    ckv_all, kpeb_all, kdb_all, vdb_all, kilo_all, kihi_all = map(per_b, (ckv, kpeb, kdb, vdb, kilo, kihi))
    s_real = t
    if past is not None:
        p_ckv, p_kpe, p_kd, p_vd, p_ki = past
        s_real = p_ckv.shape[1] + t
        z64 = jnp.zeros(p_kpe.shape, BF16)
        cat = lambda c, nw: jnp.concatenate([c, nw], axis=1)
        ckv_all = cat(p_ckv, ckv_all)
        kpeb_all = cat(jnp.concatenate([p_kpe.astype(BF16), z64], axis=-1), kpeb_all)
        kdb_all = cat(p_kd.reshape(b, -1, 256).astype(BF16), kdb_all)
        vdb_all = cat(p_vd.reshape(b, -1, 256).astype(BF16), vdb_all)
        kilo_all = cat(jnp.concatenate([p_ki.astype(BF16), z64], axis=-1), kilo_all)
        kihi_all = cat(jnp.concatenate([z64, p_ki.astype(BF16)], axis=-1), kihi_all)
    s_pad = _round_up(s_real, cfg["big"])
    if s_pad != s_real:
        padk = lambda a: jnp.pad(a, ((0, 0), (0, s_pad - s_real), (0, 0)))
        ckv_all, kpeb_all, kdb_all, vdb_all, kilo_all, kihi_all = map(
            padk, (ckv_all, kpeb_all, kdb_all, vdb_all, kilo_all, kihi_all))

    k_mla, v_mla = _kv_up(ckv_all.reshape(b * s_pad, -1), kpeb_all.reshape(b * s_pad, -1), wp, cfg["tm_kv"])
    k_mla = k_mla.reshape(MLA_HEADS, b, s_pad, 256)
    v_mla = v_mla.reshape(MLA_HEADS, b, s_pad, 128)
    mla_o = _mla_attention(q_mla.reshape(MLA_HEADS, b, t, 256), k_mla, v_mla, q_pos0,
                           cfg["tq_mla"], cfg["big"])
    dsa_o = _dsa_attention(qd.reshape(DSA_HEADS, b, t, 128), qi.reshape(IDX_HEADS // 2, b, t, 128),
                           wi.reshape(b, t, 128), kdb_all, vdb_all, kilo_all, kihi_all,
                           q_pos0, s_real, cfg["tq_dsa"], cfg["big"])
    x1 = _out_ln(mla_o.reshape(n, -1), dsa_o.reshape(n, -1), x2d, wp, alpha, cfg["tm_out"])
    y = _ffn_ln(x1, wp, alpha, cfg["tm_ffn"], cfg["tf_ffn"])
    return y.reshape(b, t, d), new_rows


_PROMPT_CFG = dict(tm_proj=256, tm_kv=256, tq_mla=512, tq_dsa=128, big=1024,
                   tm_out=256, tm_ffn=512, tf_ffn=512)
_SAMPLE_CFG = dict(tm_proj=256, tm_kv=256, tq_mla=64, tq_dsa=64, big=256,
                   tm_out=256, tm_ffn=512, tf_ffn=512)


def kernel(x_prompt, x_sample, cache_mla_ckv, cache_mla_kpe, cache_dsa_k, cache_dsa_v, cache_idx_k, w_in, w_uq, mla_q_norm_g, w_ukv, mla_kv_norm_g, w_o, ln1_g, ln1_b, w_gate, w_up, w_down, ln2_g, ln2_b):
    depth = w_in.shape[0]
    alpha = (2 * depth) ** 0.25
    past_len = cache_mla_ckv.shape[2]
    y_p, y_s = x_prompt, x_sample
    rows_p, rows_s = [], []
    for l in range(depth):
        wp = _pack_weights(w_in[l], w_uq[l], mla_q_norm_g[l], w_ukv[l], mla_kv_norm_g[l], w_o[l],
                           ln1_g[l], ln1_b[l], w_gate[l], w_up[l], w_down[l], ln2_g[l], ln2_b[l])
        y_p, r_p = _trunk_layer(y_p, 0, None, wp, alpha, _PROMPT_CFG)
        past = (cache_mla_ckv[l], cache_mla_kpe[l], cache_dsa_k[l], cache_dsa_v[l], cache_idx_k[l])
        y_s, r_s = _trunk_layer(y_s, past_len, past, wp, alpha, _SAMPLE_CFG)
        rows_p.append(r_p)
        rows_s.append(r_s)
    stack = lambda rows, i: jnp.stack([r[i] for r in rows], axis=0)
    return (y_p, y_s,
            stack(rows_p, 0), stack(rows_p, 1), stack(rows_p, 2), stack(rows_p, 3), stack(rows_p, 4),
            stack(rows_s, 0), stack(rows_s, 1), stack(rows_s, 2), stack(rows_s, 3), stack(rows_s, 4))
```

```python
import functools

import numpy as np
import jax
import jax.numpy as jnp
from jax import lax
from jax.experimental import pallas as pl
from jax.experimental.pallas import tpu as pltpu

CHUNK = 64
CHUNK_SHIFT = 6
ROPE_THETA = 10000.0
MLA_HEADS = 8
MLA_Q_LORA = 512
MLA_KV_LORA = 512
MLA_NOPE = 128
MLA_ROPE = 64
MLA_V = 128
DSA_HEADS = 8
DSA_KV_HEADS = 2
DSA_GROUP = DSA_HEADS // DSA_KV_HEADS
DSA_HEAD_DIM = 128
IDX_HEADS = 16
IDX_DIM = 64
IDX_TOPK = 256
LN_EPS = 1e-5
RMS_EPS = 1e-6

LANES = 128
MASKED = -1e30
INT_MIN = -2 ** 31
VMEM_LIMIT = 56 * 1024 * 1024

_C_CQ = 0
_C_CKV = 512
_C_QD = 1024
_C_KD = 2048
_C_VD = 2304
_C_QI = 2560
_C_KR = 3584
_C_KILO = 3712
_C_KIHI = 3840
_C_WI = 3968
_IN_COLS_P = 4096

F32 = jnp.float32
BF16 = jnp.bfloat16


def _dot(a, b):
    return jnp.dot(a, b, preferred_element_type=F32)


def _dot_nt(a, b):
    return lax.dot_general(a, b, (((1,), (1,)), ((), ())), preferred_element_type=F32)


def _params(n_axes, vmem=VMEM_LIMIT):
    return pltpu.CompilerParams(dimension_semantics=("arbitrary",) * n_axes, vmem_limit_bytes=vmem)


def _round_up(n, m):
    return -(-n // m) * m


def _resident(shape, index_map):
    return pl.BlockSpec(shape, index_map, pipeline_mode=pl.Buffered(1))


def _rope128(x, cos, sin_signed):
    return x * cos + pltpu.roll(x, 64, 1) * sin_signed


def _rope64(x, cos, sin_signed, first_half):
    rot = jnp.where(first_half, pltpu.roll(x, 96, 1), pltpu.roll(x, 32, 1))
    return x * cos + rot * sin_signed


def _rope_tables(pos):
    pos = pos.astype(F32)[:, None]
    lane = np.arange(LANES)

    def tables(dim):
        half = dim // 2
        inv = 1.0 / (ROPE_THETA ** (jnp.arange(half, dtype=F32) / half))
        ang = pos * inv[None, :]
        cos, sin = jnp.cos(ang), jnp.sin(ang)
        idx = lane % half
        sign = np.where((lane % dim) < half, -1.0, 1.0).astype(np.float32)
        return cos[:, idx], sin[:, idx] * sign[None, :]

    c128, s128 = tables(128)
    c64, s64 = tables(64)
    return c128, s128, c64, s64


def _proj_kernel(x_ref, w_ref, wuq_ref, gq_ref, gkv_ref, c128_ref, s128_ref, c64_ref, s64_ref,
                 ckv_ref, kpe_ref, kd_ref, vd_ref, ki_ref,
                 q_ref, qd_ref, qi_ref, wi_ref, kpeb_ref, kdb_ref, vdb_ref, kilo_ref, kihi_ref):
    xb = x_ref[...].astype(BF16)
    c128, s128 = c128_ref[...], s128_ref[...]
    c64, s64 = c64_ref[...], s64_ref[...]
    first_half = (lax.broadcasted_iota(jnp.int32, c64.shape, 1) % 64) < 32

    def seg(a, b):
        return _dot(xb, w_ref[:, a:b])

    def rms(v, g):
        return v * lax.rsqrt(jnp.mean(v * v, axis=-1, keepdims=True) + RMS_EPS) * g

    qn = rms(seg(_C_CQ, _C_CQ + MLA_Q_LORA), gq_ref[...]).astype(BF16)
    for h in range(MLA_HEADS):
        qh = _dot(qn, wuq_ref[:, h * 256:(h + 1) * 256])
        q_ref[h, :, 0:128] = qh[:, 0:128].astype(BF16)
        q_ref[h, :, 128:256] = _rope64(qh[:, 128:256], c64, s64, first_half).astype(BF16)

    ckv_ref[...] = rms(seg(_C_CKV, _C_CKV + MLA_KV_LORA), gkv_ref[...])

    kr = _rope64(seg(_C_KR, _C_KR + 128), c64, s64, first_half)
    kpe_ref[...] = kr[:, 0:MLA_ROPE]
    kpeb_ref[...] = kr.astype(BF16)

    for h in range(DSA_HEADS):
        a = _C_QD + h * 128
        qd_ref[h] = _rope128(seg(a, a + 128), c128, s128).astype(BF16)
    for c in range(DSA_KV_HEADS):
        a = _C_KD + c * 128
        kdc = _rope128(seg(a, a + 128), c128, s128)
        kd_ref[:, c * 128:(c + 1) * 128] = kdc
        kdb_ref[:, c * 128:(c + 1) * 128] = kdc.astype(BF16)
    vd = seg(_C_VD, _C_VD + 256)
    vd_ref[...] = vd
    vdb_ref[...] = vd.astype(BF16)

    for hp in range(IDX_HEADS // 2):
        a = _C_QI + hp * 128
        qi_ref[hp] = _rope64(seg(a, a + 128), c64, s64, first_half).astype(BF16)
    kilo = _rope64(seg(_C_KILO, _C_KILO + 128), c64, s64, first_half)
    ki_ref[...] = kilo[:, 0:IDX_DIM]
    kilo_ref[...] = kilo.astype(BF16)
    kihi_ref[...] = _rope64(seg(_C_KIHI, _C_KIHI + 128), c64, s64, first_half).astype(BF16)
    wi_ref[...] = seg(_C_WI, _C_WI + 128) * (IDX_DIM ** -0.5 * IDX_HEADS ** -0.5)


def _proj(x2d, pos, wp, tm):
    n, d = x2d.shape
    c128, s128, c64, s64 = _rope_tables(pos)
    row = lambda w: pl.BlockSpec((tm, w), lambda i: (i, 0))
    heads = lambda nh, w: pl.BlockSpec((nh, tm, w), lambda i: (0, i, 0))
    out_shapes = (
        jax.ShapeDtypeStruct((n, MLA_KV_LORA), F32),
        jax.ShapeDtypeStruct((n, MLA_ROPE), F32),
        jax.ShapeDtypeStruct((n, 256), F32),
        jax.ShapeDtypeStruct((n, 256), F32),
        jax.ShapeDtypeStruct((n, IDX_DIM), F32),
        jax.ShapeDtypeStruct((MLA_HEADS, n, 256), BF16),
        jax.ShapeDtypeStruct((DSA_HEADS, n, 128), BF16),
        jax.ShapeDtypeStruct((IDX_HEADS // 2, n, 128), BF16),
        jax.ShapeDtypeStruct((n, 128), F32),
        jax.ShapeDtypeStruct((n, 128), BF16),
        jax.ShapeDtypeStruct((n, 256), BF16),
        jax.ShapeDtypeStruct((n, 256), BF16),
        jax.ShapeDtypeStruct((n, 128), BF16),
        jax.ShapeDtypeStruct((n, 128), BF16),
    )
    out_specs = (row(MLA_KV_LORA), row(MLA_ROPE), row(256), row(256), row(IDX_DIM),
                 heads(MLA_HEADS, 256), heads(DSA_HEADS, 128), heads(IDX_HEADS // 2, 128),
                 row(128), row(128), row(256), row(256), row(128), row(128))
    in_specs = [row(d),
                _resident((d, _IN_COLS_P), lambda i: (0, 0)),
                _resident((MLA_Q_LORA, MLA_HEADS * 256), lambda i: (0, 0)),
                _resident((1, MLA_Q_LORA), lambda i: (0, 0)),
                _resident((1, MLA_KV_LORA), lambda i: (0, 0)),
                row(128), row(128), row(128), row(128)]
    return pl.pallas_call(
        _proj_kernel, grid=(n // tm,), in_specs=in_specs, out_specs=out_specs, out_shape=out_shapes,
        compiler_params=_params(1), name="proj",
    )(x2d, wp["w_in"], wp["w_uq"], wp["g_q"], wp["g_kv"], c128, s128, c64, s64)


def _kvup_kernel(ckv_ref, kpeb_ref, wk_ref, wv_ref, k_ref, v_ref):
    cb = ckv_ref[...].astype(BF16)
    kpe = kpeb_ref[...]
    for h in range(MLA_HEADS):
        k_ref[h, :, 0:128] = _dot(cb, wk_ref[:, h * 128:(h + 1) * 128]).astype(BF16)
        k_ref[h, :, 128:256] = kpe
        v_ref[h] = _dot(cb, wv_ref[:, h * 128:(h + 1) * 128]).astype(BF16)


def _kv_up(ckv2d, kpeb2d, wp, tm):
    n = ckv2d.shape[0]
    return pl.pallas_call(
        _kvup_kernel, grid=(n // tm,),
        in_specs=[pl.BlockSpec((tm, MLA_KV_LORA), lambda i: (i, 0)),
                  pl.BlockSpec((tm, 128), lambda i: (i, 0)),
                  _resident((MLA_KV_LORA, MLA_HEADS * MLA_NOPE), lambda i: (0, 0)),
                  _resident((MLA_KV_LORA, MLA_HEADS * MLA_V), lambda i: (0, 0))],
        out_specs=(pl.BlockSpec((MLA_HEADS, tm, 256), lambda i: (0, i, 0)),
                   pl.BlockSpec((MLA_HEADS, tm, 128), lambda i: (0, i, 0))),
        out_shape=(jax.ShapeDtypeStruct((MLA_HEADS, n, 256), BF16),
                   jax.ShapeDtypeStruct((MLA_HEADS, n, 128), BF16)),
        compiler_params=_params(1), name="kv_up",
    )(ckv2d, kpeb2d, wp["w_uk"], wp["w_uv"])


LOG2E = 1.4426950408889634


def _row_max(s):
    return jnp.broadcast_to(jnp.max(s, axis=1, keepdims=True), (s.shape[0], LANES))


KCH = 256


def _softmax_update(s, smax, v, m_scr, l_scr, acc_scr, scale):
    coef = scale * LOG2E
    m_prev = m_scr[...]
    m_next = m_prev
    for sm in smax:
        m_next = jnp.maximum(m_next, sm)
    m_wide = jnp.concatenate([m_next] * (KCH // LANES), axis=1)
    alpha = jnp.exp2((m_prev - m_next) * coef)
    l_sum = pv = None
    for s_ch, v_ch in zip(s, v):
        p = jnp.exp2((s_ch - m_wide) * coef)
        p_sum = jnp.sum(p, axis=1, keepdims=True)
        pv_ch = _dot(p.astype(BF16), v_ch)
        l_sum = p_sum if l_sum is None else l_sum + p_sum
        pv = pv_ch if pv is None else pv + pv_ch
    l_scr[...] = alpha * l_scr[...] + l_sum
    acc_scr[...] = acc_scr[...] * alpha + pv
    m_scr[...] = m_next


def _chunk_mask(q0, kb0, tq, tk):
    qch = (q0 + lax.broadcasted_iota(jnp.int32, (tq, 1), 0)) >> CHUNK_SHIFT
    kch = (kb0 + lax.broadcasted_iota(jnp.int32, (1, tk), 1)) >> CHUNK_SHIFT
    return kch <= qch


def _pipelined_key_steps(q0, w, produce, consume):
    d0 = (q0 // w) * w
    n_full = d0 // w
    produce(0, 0)

    def pair(j, carry):
        k0 = pl.multiple_of(j * 2 * w, 2 * w)
        produce(k0 + w, 1)
        consume(0, k0, False)
        produce(k0 + 2 * w, 0)
        consume(1, k0 + w, False)
        return carry

    lax.fori_loop(0, n_full // 2, pair, 0)
    kd = pl.multiple_of(d0, w)

    @pl.when(n_full % 2 == 1)
    def _():
        produce(kd, 1)
        consume(0, kd - w, False)
        consume(1, kd, True)

    @pl.when(n_full % 2 == 0)
    def _():
        consume(0, kd, True)

    return d0 + w


def _direct_key_steps(q0, w, fn):
    d0 = (q0 // w) * w
    lax.fori_loop(0, d0 // (2 * w),
                  lambda i, c: (fn(pl.multiple_of(i * 2 * w, 2 * w), 2 * w, False), c)[1], 0)
    kd = pl.multiple_of(d0, w)

    @pl.when((d0 // w) % 2 == 1)
    def _():
        fn(kd - w, w, False)

    fn(kd, w, True)
    return d0 + w


def _mla_kernel(q_ref, k_ref, v_ref, o_ref, s_scr, smax_scr, m_scr, l_scr, acc_scr,
                *, tq, w, q_pos0, scale):
    q0 = q_pos0 + pl.program_id(2) * tq
    m_scr[...] = jnp.full(m_scr.shape, MASKED, F32)
    l_scr[...] = jnp.zeros(l_scr.shape, F32)
    acc_scr[...] = jnp.zeros(acc_scr.shape, F32)
    q = q_ref[0, 0]

    nch = w // KCH

    def produce(k0, buf):
        for ch in range(nch):
            s = _dot_nt(q, k_ref[0, 0, pl.ds(k0 + ch * KCH, KCH), :])
            s_scr[buf, ch] = s
            smax_scr[buf, ch] = _row_max(s)

    def consume(buf, k0, masked):
        s = [s_scr[buf, ch] for ch in range(nch)]
        if masked:
            s = [jnp.where(_chunk_mask(q0, k0 + ch * KCH, tq, KCH), s[ch], MASKED) for ch in range(nch)]
            smax = [_row_max(s_ch) for s_ch in s]
        else:
            smax = [smax_scr[buf, ch] for ch in range(nch)]
        v = [v_ref[0, 0, pl.ds(k0 + ch * KCH, KCH), :] for ch in range(nch)]
        _softmax_update(s, smax, v, m_scr, l_scr, acc_scr, scale)

    _pipelined_key_steps(q0, w, produce, consume)
    o_ref[0] = (acc_scr[...] / l_scr[...]).astype(o_ref.dtype)


def _mla_attention(q, k, v, q_pos0, tq, w):
    nh, b, t, _ = q.shape
    s = k.shape[2]
    assert w >= tq and s % w == 0 and q_pos0 % CHUNK == 0 and tq % CHUNK == 0
    kern = functools.partial(_mla_kernel, tq=tq, w=w, q_pos0=q_pos0,
                             scale=(MLA_NOPE + MLA_ROPE) ** -0.5)
    return pl.pallas_call(
        kern, grid=(b, nh, t // tq),
        in_specs=[pl.BlockSpec((1, 1, tq, 256), lambda bi, h, i: (h, bi, i, 0)),
                  pl.BlockSpec((1, 1, s, 256), lambda bi, h, i: (h, bi, 0, 0)),
                  pl.BlockSpec((1, 1, s, 128), lambda bi, h, i: (h, bi, 0, 0))],
        out_specs=pl.BlockSpec((1, tq, 128), lambda bi, h, i: (bi, i, h)),
        out_shape=jax.ShapeDtypeStruct((b, t, nh * MLA_V), BF16),
        scratch_shapes=[pltpu.VMEM((2, w // KCH, tq, KCH), F32), pltpu.VMEM((2, w // KCH, tq, LANES), F32),
                        pltpu.VMEM((tq, LANES), F32), pltpu.VMEM((tq, LANES), F32),
                        pltpu.VMEM((tq, MLA_V), F32)],
        compiler_params=_params(3), name="mla_attn",
    )(q, k, v)


def _dsa_kernel(qd_ref, qi_ref, wi_ref, kd_ref, vd_ref, kilo_ref, kihi_ref, o_ref,
                mm_scr, smax_scr, key_scr, c_scr, cnt_scr, cand_scr, m_scr, l_scr, acc_scr,
                *, tq, w, q_pos0, s_real, topk, idx_bits, scale):
    q0 = q_pos0 + pl.program_id(1) * tq
    sub = w // LANES

    qi_all = qi_ref[:, 0].reshape(IDX_HEADS // 2 * tq, LANES)
    wi = wi_ref[0]

    def score_step(k0, width, masked):
        lo = _dot_nt(qi_all, kilo_ref[0, pl.ds(k0, width), :])
        hi = _dot_nt(qi_all, kihi_ref[0, pl.ds(k0, width), :])
        score = jnp.zeros((tq, width), F32)
        for hp in range(IDX_HEADS // 2):
            rows = slice(hp * tq, (hp + 1) * tq)
            score = score + wi[:, 2 * hp:2 * hp + 1] * jnp.maximum(lo[rows], 0.0)
            score = score + wi[:, 2 * hp + 1:2 * hp + 2] * jnp.maximum(hi[rows], 0.0)
        bits = pltpu.bitcast(score, jnp.int32)
        key = bits ^ ((bits >> 31) & 0x7FFFFFFF)
        if masked:
            key = jnp.where(_chunk_mask(q0, k0, tq, width), key, INT_MIN)
        key_scr[:, pl.ds(k0, width)] = key

    k_end = _direct_key_steps(q0, w, score_step)
    n_sb = k_end // w

    qpos = q0 + lax.broadcasted_iota(jnp.int32, (tq, LANES), 0)
    visible = jnp.minimum(((qpos >> CHUNK_SHIFT) + 1) * CHUNK, s_real)
    k_target = jnp.minimum(visible, topk).astype(F32)

    def count(mode):
        cand = cand_scr[...]
        cval = c_scr[...]

        def span(k0, nblk, acc):
            for u in range(nblk):
                blk = key_scr[:, pl.ds(k0 + u * LANES, LANES)]
                if mode == "ge":
                    hit = blk >= cand
                else:
                    idx = k0 + u * LANES + lax.broadcasted_iota(jnp.int32, (tq, LANES), 1)
                    hit = jnp.where(blk == cval, idx, jnp.int32(2 ** 30)) < cand
                acc = acc + jnp.where(hit, 1.0, 0.0)
            return acc

        acc = lax.fori_loop(0, n_sb // 2,
                            lambda i, a: span(pl.multiple_of(i * 2 * w, 2 * w), 2 * sub, a),
                            jnp.zeros((tq, LANES), F32))
        acc = lax.fori_loop(2 * (n_sb // 2), n_sb,
                            lambda i, a: span(pl.multiple_of(i * w, w), sub, a), acc)
        cnt_scr[...] = jnp.broadcast_to(jnp.sum(acc, axis=1, keepdims=True), (tq, LANES))

    c_scr[...] = jnp.full((tq, LANES), INT_MIN, jnp.int32)
    cntc0 = jnp.full((tq, LANES), 1.0, F32) * k_end.astype(F32)

    def unsettled(cnt_c):
        return jnp.max(jnp.abs(cnt_c - k_target)) > 0.0

    def bit_cond(carry):
        b, pending, _ = carry
        return jnp.logical_and(b >= 0, pending)

    def bit_body(carry):
        b, _, cnt_c = carry
        cand = c_scr[...] + jnp.left_shift(jnp.int32(1), b)
        cand_scr[...] = cand
        count("ge")
        cnt = cnt_scr[...]
        take = cnt >= k_target
        c_scr[...] = jnp.where(take, cand, c_scr[...])
        cnt_c = jnp.where(take, cnt, cnt_c)
        return b - 1, unsettled(cnt_c), cnt_c

    _, ties, _ = lax.while_loop(bit_cond, bit_body, (jnp.int32(31), unsettled(cntc0), cntc0))

    @pl.when(ties)
    def _():
        cand_scr[...] = c_scr[...] + 1
        count("ge")
        need = k_target - cnt_scr[...]
        x = jnp.zeros((tq, LANES), jnp.int32)
        for bit in range(idx_bits - 1, -1, -1):
            cand_scr[...] = x + (1 << bit)
            count("eq_lt")
            x = jnp.where(cnt_scr[...] < need, x + (1 << bit), x)
        cand_scr[...] = x

        def demote(kb, carry):
            kb0 = pl.multiple_of(kb * 256, 256)
            cval = jnp.concatenate([c_scr[...]] * 2, axis=1)
            last = jnp.concatenate([cand_scr[...]] * 2, axis=1)
            idx = kb0 + lax.broadcasted_iota(jnp.int32, (tq, 256), 1)
            blk = key_scr[:, pl.ds(kb0, 256)]
            drop = jnp.where(blk == cval, idx, jnp.int32(-1)) > last
            key_scr[:, pl.ds(kb0, 256)] = jnp.where(drop, cval - 1, blk)
            return carry

        lax.fori_loop(0, k_end // 256, demote, 0)

    m_scr[...] = jnp.full(m_scr.shape, MASKED, F32)
    l_scr[...] = jnp.zeros(l_scr.shape, F32)
    acc_scr[...] = jnp.zeros(acc_scr.shape, F32)

    rows_c = DSA_GROUP * tq
    nch = w // KCH

    def qk_matmuls(k0, buf):
        cval = jnp.concatenate([c_scr[...]] * (KCH // LANES), axis=1)
        for ch in range(nch):
            kc = k0 + ch * KCH
            bias = jnp.where(key_scr[:, pl.ds(kc, KCH)] >= cval, 0.0, MASKED)
            bias = jnp.concatenate([bias] * DSA_GROUP, axis=0)
            for c in range(DSA_KV_HEADS):
                qg = qd_ref[c * DSA_GROUP:(c + 1) * DSA_GROUP, 0].reshape(rows_c, DSA_HEAD_DIM)
                cols = slice(c * DSA_HEAD_DIM, (c + 1) * DSA_HEAD_DIM)
                s = _dot_nt(qg, kd_ref[0, pl.ds(kc, KCH), cols]) + bias
                mm_scr[buf, c, ch] = s
                smax_scr[buf, c, ch] = _row_max(s)

    def attend(buf, k0, masked):
        del masked
        for c in range(DSA_KV_HEADS):
            cols = slice(c * DSA_HEAD_DIM, (c + 1) * DSA_HEAD_DIM)
            _softmax_update([mm_scr[buf, c, ch] for ch in range(nch)],
                            [smax_scr[buf, c, ch] for ch in range(nch)],
                            [vd_ref[0, pl.ds(k0 + ch * KCH, KCH), cols] for ch in range(nch)],
                            m_scr.at[c], l_scr.at[c], acc_scr.at[c], scale)

    _pipelined_key_steps(q0, w, qk_matmuls, attend)
    for c in range(DSA_KV_HEADS):
        o = acc_scr[c] / l_scr[c]
        for g in range(DSA_GROUP):
            h = c * DSA_GROUP + g
            o_ref[0, :, h * DSA_HEAD_DIM:(h + 1) * DSA_HEAD_DIM] = o[g * tq:(g + 1) * tq].astype(o_ref.dtype)


def _dsa_attention(qd, qi, wi, kd, vd, kilo, kihi, q_pos0, s_real, tq, w):
    _, b, t, _ = qd.shape
    s = kd.shape[1]
    assert w >= tq and s % w == 0 and q_pos0 % CHUNK == 0 and tq % CHUNK == 0
    kern = functools.partial(
        _dsa_kernel, tq=tq, w=w, q_pos0=q_pos0, s_real=s_real,
        topk=min(IDX_TOPK, s_real // 4), idx_bits=int(s).bit_length(), scale=DSA_HEAD_DIM ** -0.5)
    heads = lambda: pl.BlockSpec((8, 1, tq, 128), lambda bi, i: (0, bi, i, 0))
    keys = lambda width: _resident((1, s, width), lambda bi, i: (bi, 0, 0))
    return pl.pallas_call(
        kern, grid=(b, t // tq),
        in_specs=[heads(), heads(), pl.BlockSpec((1, tq, 128), lambda bi, i: (bi, i, 0)),
                  keys(256), keys(256), keys(128), keys(128)],
        out_specs=pl.BlockSpec((1, tq, DSA_HEADS * DSA_HEAD_DIM), lambda bi, i: (bi, i, 0)),
        out_shape=jax.ShapeDtypeStruct((b, t, DSA_HEADS * DSA_HEAD_DIM), BF16),
        scratch_shapes=[pltpu.VMEM((2, DSA_KV_HEADS, w // KCH, DSA_GROUP * tq, KCH), F32),
                        pltpu.VMEM((2, DSA_KV_HEADS, w // KCH, DSA_GROUP * tq, LANES), F32),
                        pltpu.VMEM((tq, s), jnp.int32),
                        pltpu.VMEM((tq, LANES), jnp.int32), pltpu.VMEM((tq, LANES), F32),
                        pltpu.VMEM((tq, LANES), jnp.int32),
                        pltpu.VMEM((DSA_KV_HEADS, DSA_GROUP * tq, LANES), F32),
                        pltpu.VMEM((DSA_KV_HEADS, DSA_GROUP * tq, LANES), F32),
                        pltpu.VMEM((DSA_KV_HEADS, DSA_GROUP * tq, DSA_HEAD_DIM), F32)],
        compiler_params=_params(2), name="dsa_attn",
    )(qd, qi, wi, kd, vd, kilo, kihi)


def _layer_norm(v, g, b):
    mu = jnp.mean(v, axis=-1, keepdims=True)
    d = v - mu
    var = jnp.mean(d * d, axis=-1, keepdims=True)
    return d * lax.rsqrt(var + LN_EPS) * g + b


def _outln_kernel(mla_ref, dsa_ref, x_ref, wo_ref, g_ref, b_ref, o_ref, *, alpha, half):
    a = _dot(mla_ref[...], wo_ref[0:half, :]) + _dot(dsa_ref[...], wo_ref[half:, :])
    o_ref[...] = _layer_norm(alpha * x_ref[...] + a, g_ref[...], b_ref[...])


def _out_ln(mla_o, dsa_o, x2d, wp, alpha, tm):
    n, d = x2d.shape
    half = mla_o.shape[1]
    row = lambda w: pl.BlockSpec((tm, w), lambda i: (i, 0))
    return pl.pallas_call(
        functools.partial(_outln_kernel, alpha=alpha, half=half), grid=(n // tm,),
        in_specs=[row(half), row(dsa_o.shape[1]), row(d),
                  _resident(wp["w_o"].shape, lambda i: (0, 0)),
                  _resident((1, d), lambda i: (0, 0)), _resident((1, d), lambda i: (0, 0))],
        out_specs=row(d), out_shape=jax.ShapeDtypeStruct((n, d), F32),
        compiler_params=_params(1), name="out_ln",
    )(mla_o, dsa_o, x2d, wp["w_o"], wp["ln1_g"], wp["ln1_b"])


def _ffn_kernel(x_ref, wg_ref, wu_ref, wd_ref, g_ref, b_ref, o_ref, xb_scr, acc_scr, *, alpha):
    j = pl.program_id(1)

    @pl.when(j == 0)
    def _():
        xb_scr[...] = x_ref[...].astype(BF16)
        acc_scr[...] = jnp.zeros(acc_scr.shape, F32)

    xb = xb_scr[...]
    gate = _dot(xb, wg_ref[...])
    up = _dot(xb, wu_ref[...])
    hidden = gate * (1.0 / (1.0 + jnp.exp(-gate))) * up
    acc_scr[...] += _dot(hidden.astype(BF16), wd_ref[...])

    @pl.when(j == pl.num_programs(1) - 1)
    def _():
        o_ref[...] = _layer_norm(alpha * x_ref[...] + acc_scr[...], g_ref[...], b_ref[...])


def _ffn_ln(x2d, wp, alpha, tm, tf):
    n, d = x2d.shape
    dff = wp["w_gate"].shape[1]
    return pl.pallas_call(
        functools.partial(_ffn_kernel, alpha=alpha), grid=(n // tm, dff // tf),
        in_specs=[pl.BlockSpec((tm, d), lambda i, j: (i, 0)),
                  pl.BlockSpec((d, tf), lambda i, j: (0, j)),
                  pl.BlockSpec((d, tf), lambda i, j: (0, j)),
                  pl.BlockSpec((tf, d), lambda i, j: (j, 0)),
                  _resident((1, d), lambda i, j: (0, 0)), _resident((1, d), lambda i, j: (0, 0))],
        out_specs=pl.BlockSpec((tm, d), lambda i, j: (i, 0)),
        out_shape=jax.ShapeDtypeStruct((n, d), F32),
        scratch_shapes=[pltpu.VMEM((tm, d), BF16), pltpu.VMEM((tm, d), F32)],
        compiler_params=_params(2), name="ffn_ln",
    )(x2d, wp["w_gate"], wp["w_up"], wp["w_down"], wp["ln2_g"], wp["ln2_b"])


def _pack_weights(w_in, w_uq, g_q, w_ukv, g_kv, w_o, ln1_g, ln1_b, w_gate, w_up, w_down, ln2_g, ln2_b):
    d = w_in.shape[0]
    splits = (MLA_Q_LORA, MLA_KV_LORA, MLA_ROPE, DSA_HEADS * DSA_HEAD_DIM, DSA_KV_HEADS * DSA_HEAD_DIM,
              DSA_KV_HEADS * DSA_HEAD_DIM, IDX_HEADS * IDX_DIM, IDX_DIM, IDX_HEADS)
    offs = np.cumsum(splits)[:-1].tolist()
    c_q, c_kv, k_r, q_d, k_d, v_d, q_i, k_i, w_i = jnp.split(w_in, offs, axis=1)
    z = lambda n: jnp.zeros((d, n), w_in.dtype)
    w_in_p = jnp.concatenate(
        [c_q, c_kv, q_d, k_d, v_d, q_i, k_r, z(64), k_i, z(64), z(64), k_i, w_i, z(128 - IDX_HEADS)], axis=1)
    assert w_in_p.shape[1] == _IN_COLS_P
    w_uq_p = jnp.pad(w_uq, ((0, 0), (0, 0), (0, 256 - MLA_NOPE - MLA_ROPE)))
    return {
        "w_in": w_in_p.astype(BF16),
        "w_uq": w_uq_p.reshape(MLA_Q_LORA, MLA_HEADS * 256).astype(BF16),
        "g_q": g_q.reshape(1, -1), "g_kv": g_kv.reshape(1, -1),
        "w_uk": w_ukv[:, :, :MLA_NOPE].reshape(MLA_KV_LORA, MLA_HEADS * MLA_NOPE).astype(BF16),
        "w_uv": w_ukv[:, :, MLA_NOPE:].reshape(MLA_KV_LORA, MLA_HEADS * MLA_V).astype(BF16),
        "w_o": w_o.astype(BF16),
        "ln1_g": ln1_g.reshape(1, -1), "ln1_b": ln1_b.reshape(1, -1),
        "w_gate": w_gate.astype(BF16), "w_up": w_up.astype(BF16), "w_down": w_down.astype(BF16),
        "ln2_g": ln2_g.reshape(1, -1), "ln2_b": ln2_b.reshape(1, -1),
    }


def _trunk_layer(x, q_pos0, past, wp, alpha, cfg):
    b, t, d = x.shape
    n = b * t
    x2d = x.reshape(n, d)
    pos = jnp.tile(q_pos0 + jnp.arange(t, dtype=jnp.int32), b)
    (ckv, kpe, kd, vd, ki, q_mla, qd, qi, wi, kpeb, kdb, vdb, kilo, kihi) = _proj(x2d, pos, wp, cfg["tm_proj"])
    new_rows = (ckv.reshape(b, t, -1), kpe.reshape(b, t, -1),
                kd.reshape(b, t, DSA_KV_HEADS, DSA_HEAD_DIM), vd.reshape(b, t, DSA_KV_HEADS, DSA_HEAD_DIM),
                ki.reshape(b, t, -1))

    per_b = lambda a: a.reshape(b, t, a.shape[-1])
    ckv_all, kpeb_all, kdb_all, vdb_all, kilo_all, kihi_all = map(per_b, (ckv, kpeb, kdb, vdb, kilo, kihi))
    s_real = t
    if past is not None:
        p_ckv, p_kpe, p_kd, p_vd, p_ki = past
        s_real = p_ckv.shape[1] + t
        z64 = jnp.zeros(p_kpe.shape, BF16)
        cat = lambda c, nw: jnp.concatenate([c, nw], axis=1)
        ckv_all = cat(p_ckv, ckv_all)
        kpeb_all = cat(jnp.concatenate([p_kpe.astype(BF16), z64], axis=-1), kpeb_all)
        kdb_all = cat(p_kd.reshape(b, -1, 256).astype(BF16), kdb_all)
        vdb_all = cat(p_vd.reshape(b, -1, 256).astype(BF16), vdb_all)
        kilo_all = cat(jnp.concatenate([p_ki.astype(BF16), z64], axis=-1), kilo_all)
        kihi_all = cat(jnp.concatenate([z64, p_ki.astype(BF16)], axis=-1), kihi_all)
    s_pad = _round_up(s_real, cfg["w"])
    if s_pad != s_real:
        padk = lambda a: jnp.pad(a, ((0, 0), (0, s_pad - s_real), (0, 0)))
        ckv_all, kpeb_all, kdb_all, vdb_all, kilo_all, kihi_all = map(
            padk, (ckv_all, kpeb_all, kdb_all, vdb_all, kilo_all, kihi_all))

    k_mla, v_mla = _kv_up(ckv_all.reshape(b * s_pad, -1), kpeb_all.reshape(b * s_pad, -1), wp, cfg["tm_kv"])
    k_mla = k_mla.reshape(MLA_HEADS, b, s_pad, 256)
    v_mla = v_mla.reshape(MLA_HEADS, b, s_pad, 128)
    mla_o = _mla_attention(q_mla.reshape(MLA_HEADS, b, t, 256), k_mla, v_mla, q_pos0,
                           cfg["tq_mla"], cfg["w"])
    dsa_o = _dsa_attention(qd.reshape(DSA_HEADS, b, t, 128), qi.reshape(IDX_HEADS // 2, b, t, 128),
                           wi.reshape(b, t, 128), kdb_all, vdb_all, kilo_all, kihi_all,
                           q_pos0, s_real, cfg["tq_dsa"], cfg["w"])
    x1 = _out_ln(mla_o.reshape(n, -1), dsa_o.reshape(n, -1), x2d, wp, alpha, cfg["tm_out"])
    y = _ffn_ln(x1, wp, alpha, cfg["tm_ffn"], cfg["tf_ffn"])
    return y.reshape(b, t, d), new_rows


_PROMPT_CFG = dict(tm_proj=256, tm_kv=256, tq_mla=512, tq_dsa=128, w=512,
                   tm_out=256, tm_ffn=512, tf_ffn=512)
_SAMPLE_CFG = dict(tm_proj=256, tm_kv=256, tq_mla=64, tq_dsa=64, w=256,
                   tm_out=256, tm_ffn=512, tf_ffn=512)


def kernel(x_prompt, x_sample, cache_mla_ckv, cache_mla_kpe, cache_dsa_k, cache_dsa_v, cache_idx_k, w_in, w_uq, mla_q_norm_g, w_ukv, mla_kv_norm_g, w_o, ln1_g, ln1_b, w_gate, w_up, w_down, ln2_g, ln2_b):
    depth = w_in.shape[0]
    alpha = (2 * depth) ** 0.25
    past_len = cache_mla_ckv.shape[2]
    y_p, y_s = x_prompt, x_sample
    rows_p, rows_s = [], []
    for l in range(depth):
        wp = _pack_weights(w_in[l], w_uq[l], mla_q_norm_g[l], w_ukv[l], mla_kv_norm_g[l], w_o[l],
                           ln1_g[l], ln1_b[l], w_gate[l], w_up[l], w_down[l], ln2_g[l], ln2_b[l])
        y_p, r_p = _trunk_layer(y_p, 0, None, wp, alpha, _PROMPT_CFG)
        past = (cache_mla_ckv[l], cache_mla_kpe[l], cache_dsa_k[l], cache_dsa_v[l], cache_idx_k[l])
        y_s, r_s = _trunk_layer(y_s, past_len, past, wp, alpha, _SAMPLE_CFG)
        rows_p.append(r_p)
        rows_s.append(r_s)
    stack = lambda rows, i: jnp.stack([r[i] for r in rows], axis=0)
    return (y_p, y_s,
            stack(rows_p, 0), stack(rows_p, 1), stack(rows_p, 2), stack(rows_p, 3), stack(rows_p, 4),
            stack(rows_s, 0), stack(rows_s, 1), stack(rows_s, 2), stack(rows_s, 3), stack(rows_s, 4))
```

```python
import functools

import numpy as np
import jax
import jax.numpy as jnp
from jax import lax
from jax.experimental import pallas as pl
from jax.experimental.pallas import tpu as pltpu

CHUNK = 64
CHUNK_SHIFT = 6
ROPE_THETA = 10000.0
MLA_HEADS = 8
MLA_Q_LORA = 512
MLA_KV_LORA = 512
MLA_NOPE = 128
MLA_ROPE = 64
MLA_V = 128
DSA_HEADS = 8
DSA_KV_HEADS = 2
DSA_GROUP = DSA_HEADS // DSA_KV_HEADS
DSA_HEAD_DIM = 128
IDX_HEADS = 16
IDX_DIM = 64
IDX_TOPK = 256
LN_EPS = 1e-5
RMS_EPS = 1e-6

LANES = 128
MASKED = -1e30
INT_MIN = -2 ** 31
VMEM_LIMIT = 56 * 1024 * 1024

_C_CQ = 0
_C_CKV = 512
_C_QD = 1024
_C_KD = 2048
_C_VD = 2304
_C_QI = 2560
_C_KR = 3584
_C_KILO = 3712
_C_KIHI = 3840
_C_WI = 3968
_IN_COLS_P = 4096

F32 = jnp.float32
BF16 = jnp.bfloat16


def _dot(a, b):
    return jnp.dot(a, b, preferred_element_type=F32)


def _dot_nt(a, b):
    return lax.dot_general(a, b, (((1,), (1,)), ((), ())), preferred_element_type=F32)


def _params(n_axes, vmem=VMEM_LIMIT):
    return pltpu.CompilerParams(dimension_semantics=("arbitrary",) * n_axes, vmem_limit_bytes=vmem)


def _round_up(n, m):
    return -(-n // m) * m


def _resident(shape, index_map):
    return pl.BlockSpec(shape, index_map, pipeline_mode=pl.Buffered(1))


def _rope128(x, cos, sin_signed):
    return x * cos + pltpu.roll(x, 64, 1) * sin_signed


def _rope64(x, cos, sin_signed, first_half):
    rot = jnp.where(first_half, pltpu.roll(x, 96, 1), pltpu.roll(x, 32, 1))
    return x * cos + rot * sin_signed


def _rope_tables(pos):
    pos = pos.astype(F32)[:, None]
    lane = np.arange(LANES)

    def tables(dim):
        half = dim // 2
        inv = 1.0 / (ROPE_THETA ** (jnp.arange(half, dtype=F32) / half))
        ang = pos * inv[None, :]
        cos, sin = jnp.cos(ang), jnp.sin(ang)
        idx = lane % half
        sign = np.where((lane % dim) < half, -1.0, 1.0).astype(np.float32)
        return cos[:, idx], sin[:, idx] * sign[None, :]

    c128, s128 = tables(128)
    c64, s64 = tables(64)
    return c128, s128, c64, s64


def _proj_kernel(x_ref, w_ref, wuq_ref, gq_ref, gkv_ref, c128_ref, s128_ref, c64_ref, s64_ref,
                 ckv_ref, kpe_ref, kd_ref, vd_ref, ki_ref,
                 q_ref, qd_ref, qi_ref, wi_ref, kpeb_ref, kdb_ref, vdb_ref, kilo_ref, kihi_ref):
    xb = x_ref[...].astype(BF16)
    c128, s128 = c128_ref[...], s128_ref[...]
    c64, s64 = c64_ref[...], s64_ref[...]
    first_half = (lax.broadcasted_iota(jnp.int32, c64.shape, 1) % 64) < 32

    def seg(a, b):
        return _dot(xb, w_ref[:, a:b])

    def rms(v, g):
        return v * lax.rsqrt(jnp.mean(v * v, axis=-1, keepdims=True) + RMS_EPS) * g

    qn = rms(seg(_C_CQ, _C_CQ + MLA_Q_LORA), gq_ref[...]).astype(BF16)
    for h in range(MLA_HEADS):
        qh = _dot(qn, wuq_ref[:, h * 256:(h + 1) * 256])
        q_ref[h, :, 0:128] = qh[:, 0:128].astype(BF16)
        q_ref[h, :, 128:256] = _rope64(qh[:, 128:256], c64, s64, first_half).astype(BF16)

    ckv_ref[...] = rms(seg(_C_CKV, _C_CKV + MLA_KV_LORA), gkv_ref[...])

    kr = _rope64(seg(_C_KR, _C_KR + 128), c64, s64, first_half)
    kpe_ref[...] = kr[:, 0:MLA_ROPE]
    kpeb_ref[...] = kr.astype(BF16)

    for h in range(DSA_HEADS):
        a = _C_QD + h * 128
        qd_ref[h] = _rope128(seg(a, a + 128), c128, s128).astype(BF16)
    for c in range(DSA_KV_HEADS):
        a = _C_KD + c * 128
        kdc = _rope128(seg(a, a + 128), c128, s128)
        kd_ref[:, c * 128:(c + 1) * 128] = kdc
        kdb_ref[:, c * 128:(c + 1) * 128] = kdc.astype(BF16)
    vd = seg(_C_VD, _C_VD + 256)
    vd_ref[...] = vd
    vdb_ref[...] = vd.astype(BF16)

    for hp in range(IDX_HEADS // 2):
        a = _C_QI + hp * 128
        qi_ref[hp] = _rope64(seg(a, a + 128), c64, s64, first_half).astype(BF16)
    kilo = _rope64(seg(_C_KILO, _C_KILO + 128), c64, s64, first_half)
    ki_ref[...] = kilo[:, 0:IDX_DIM]
    kilo_ref[...] = kilo.astype(BF16)
    kihi_ref[...] = _rope64(seg(_C_KIHI, _C_KIHI + 128), c64, s64, first_half).astype(BF16)
    wi_ref[...] = seg(_C_WI, _C_WI + 128) * (IDX_DIM ** -0.5 * IDX_HEADS ** -0.5)


def _proj(x2d, pos, wp, tm):
    n, d = x2d.shape
    c128, s128, c64, s64 = _rope_tables(pos)
    row = lambda w: pl.BlockSpec((tm, w), lambda i: (i, 0))
    heads = lambda nh, w: pl.BlockSpec((nh, tm, w), lambda i: (0, i, 0))
    out_shapes = (
        jax.ShapeDtypeStruct((n, MLA_KV_LORA), F32),
        jax.ShapeDtypeStruct((n, MLA_ROPE), F32),
        jax.ShapeDtypeStruct((n, 256), F32),
        jax.ShapeDtypeStruct((n, 256), F32),
        jax.ShapeDtypeStruct((n, IDX_DIM), F32),
        jax.ShapeDtypeStruct((MLA_HEADS, n, 256), BF16),
        jax.ShapeDtypeStruct((DSA_HEADS, n, 128), BF16),
        jax.ShapeDtypeStruct((IDX_HEADS // 2, n, 128), BF16),
        jax.ShapeDtypeStruct((n, 128), F32),
        jax.ShapeDtypeStruct((n, 128), BF16),
        jax.ShapeDtypeStruct((n, 256), BF16),
        jax.ShapeDtypeStruct((n, 256), BF16),
        jax.ShapeDtypeStruct((n, 128), BF16),
        jax.ShapeDtypeStruct((n, 128), BF16),
    )
    out_specs = (row(MLA_KV_LORA), row(MLA_ROPE), row(256), row(256), row(IDX_DIM),
                 heads(MLA_HEADS, 256), heads(DSA_HEADS, 128), heads(IDX_HEADS // 2, 128),
                 row(128), row(128), row(256), row(256), row(128), row(128))
    in_specs = [row(d),
                _resident((d, _IN_COLS_P), lambda i: (0, 0)),
                _resident((MLA_Q_LORA, MLA_HEADS * 256), lambda i: (0, 0)),
                _resident((1, MLA_Q_LORA), lambda i: (0, 0)),
                _resident((1, MLA_KV_LORA), lambda i: (0, 0)),
                row(128), row(128), row(128), row(128)]
    return pl.pallas_call(
        _proj_kernel, grid=(n // tm,), in_specs=in_specs, out_specs=out_specs, out_shape=out_shapes,
        compiler_params=_params(1), name="proj",
    )(x2d, wp["w_in"], wp["w_uq"], wp["g_q"], wp["g_kv"], c128, s128, c64, s64)


def _kvup_kernel(ckv_ref, kpeb_ref, wk_ref, wv_ref, k_ref, v_ref):
    cb = ckv_ref[...].astype(BF16)
    kpe = kpeb_ref[...]
    for h in range(MLA_HEADS):
        k_ref[h, :, 0:128] = _dot(cb, wk_ref[:, h * 128:(h + 1) * 128]).astype(BF16)
        k_ref[h, :, 128:256] = kpe
        v_ref[h] = _dot(cb, wv_ref[:, h * 128:(h + 1) * 128]).astype(BF16)


def _kv_up(ckv2d, kpeb2d, wp, tm):
    n = ckv2d.shape[0]
    return pl.pallas_call(
        _kvup_kernel, grid=(n // tm,),
        in_specs=[pl.BlockSpec((tm, MLA_KV_LORA), lambda i: (i, 0)),
                  pl.BlockSpec((tm, 128), lambda i: (i, 0)),
                  _resident((MLA_KV_LORA, MLA_HEADS * MLA_NOPE), lambda i: (0, 0)),
                  _resident((MLA_KV_LORA, MLA_HEADS * MLA_V), lambda i: (0, 0))],
        out_specs=(pl.BlockSpec((MLA_HEADS, tm, 256), lambda i: (0, i, 0)),
                   pl.BlockSpec((MLA_HEADS, tm, 128), lambda i: (0, i, 0))),
        out_shape=(jax.ShapeDtypeStruct((MLA_HEADS, n, 256), BF16),
                   jax.ShapeDtypeStruct((MLA_HEADS, n, 128), BF16)),
        compiler_params=_params(1), name="kv_up",
    )(ckv2d, kpeb2d, wp["w_uk"], wp["w_uv"])


LOG2E = 1.4426950408889634


def _row_max(s):
    return jnp.broadcast_to(jnp.max(s, axis=1, keepdims=True), (s.shape[0], LANES))


KCH = 256


def _softmax_probs(s, smax, m_scr, l_scr, scale):
    coef = scale * LOG2E
    m_prev = m_scr[...]
    m_next = m_prev
    for sm in smax:
        m_next = jnp.maximum(m_next, sm)
    m_wide = jnp.concatenate([m_next] * (KCH // LANES), axis=1)
    alpha = jnp.exp2((m_prev - m_next) * coef)
    l_sum, probs = None, []
    for s_ch in s:
        p = jnp.exp2((s_ch - m_wide) * coef)
        p_sum = jnp.sum(p, axis=1, keepdims=True)
        l_sum = p_sum if l_sum is None else l_sum + p_sum
        probs.append(p.astype(BF16))
    l_scr[...] = alpha * l_scr[...] + l_sum
    m_scr[...] = m_next
    return alpha, probs


def _accumulate_pv(alpha, probs, v, acc_scr):
    pv = None
    for p_ch, v_ch in zip(probs, v):
        pv_ch = _dot(p_ch, v_ch)
        pv = pv_ch if pv is None else pv + pv_ch
    acc_scr[...] = acc_scr[...] * alpha + pv


def _chunk_mask(q0, kb0, tq, tk):
    qch = (q0 + lax.broadcasted_iota(jnp.int32, (tq, 1), 0)) >> CHUNK_SHIFT
    kch = (kb0 + lax.broadcasted_iota(jnp.int32, (1, tk), 1)) >> CHUNK_SHIFT
    return kch <= qch


def _three_stage_key_steps(q0, w, qk, sm, pv):
    n = q0 // w
    at = lambda i: pl.multiple_of(i * w, w)

    @pl.when(n == 0)
    def _():
        qk(0, 0)
        sm(0, 0, True)
        pv(0, 0)

    @pl.when(n >= 1)
    def _():
        qk(0, 0)
        qk(w, 1)
        sm(0, 0, False)
        pairs = (n - 1) // 2

        def body(t, carry):
            i = 2 * t
            qk(at(i + 2), 0)
            sm(1, at(i + 1), False)
            pv(0, at(i))
            qk(at(i + 3), 1)
            sm(0, at(i + 2), False)
            pv(1, at(i + 1))
            return carry

        lax.fori_loop(0, pairs, body, 0)
        i = 2 * pairs

        @pl.when(n - i == 1)
        def _():
            sm(1, at(i + 1), True)
            pv(0, at(i))
            pv(1, at(i + 1))

        @pl.when(n - i == 2)
        def _():
            qk(at(i + 2), 0)
            sm(1, at(i + 1), False)
            pv(0, at(i))
            sm(0, at(i + 2), True)
            pv(1, at(i + 1))
            pv(0, at(i + 2))

    return (n + 1) * w


def _direct_key_steps(q0, w, fn, wide=2):
    n = q0 // w
    ww = wide * w
    lax.fori_loop(0, n // wide, lambda i, c: (fn(pl.multiple_of(i * ww, ww), ww, False), c)[1], 0)
    lax.fori_loop((n // wide) * wide, n, lambda i, c: (fn(pl.multiple_of(i * w, w), w, False), c)[1], 0)
    fn(pl.multiple_of(n * w, w), w, True)
    return (n + 1) * w


def _mla_kernel(q_ref, k_ref, v_ref, o_ref, s_scr, smax_scr, p_scr, alpha_scr, m_scr, l_scr, acc_scr,
                *, tq, w, q_pos0, scale):
    q0 = q_pos0 + pl.program_id(2) * tq
    m_scr[...] = jnp.full(m_scr.shape, MASKED, F32)
    l_scr[...] = jnp.zeros(l_scr.shape, F32)
    acc_scr[...] = jnp.zeros(acc_scr.shape, F32)
    q = q_ref[0, 0]

    nch = w // KCH

    def qk(k0, buf):
        for ch in range(nch):
            s = _dot_nt(q, k_ref[0, 0, pl.ds(k0 + ch * KCH, KCH), :])
            s_scr[buf, ch] = s
            smax_scr[buf, ch] = _row_max(s)

    def sm(buf, k0, masked):
        s = [s_scr[buf, ch] for ch in range(nch)]
        if masked:
            s = [jnp.where(_chunk_mask(q0, k0 + ch * KCH, tq, KCH), s[ch], MASKED) for ch in range(nch)]
            smax = [_row_max(s_ch) for s_ch in s]
        else:
            smax = [smax_scr[buf, ch] for ch in range(nch)]
        alpha, probs = _softmax_probs(s, smax, m_scr, l_scr, scale)
        alpha_scr[buf] = alpha
        for ch in range(nch):
            p_scr[buf, ch] = probs[ch]

    def pv(buf, k0):
        _accumulate_pv(alpha_scr[buf], [p_scr[buf, ch] for ch in range(nch)],
                       [v_ref[0, 0, pl.ds(k0 + ch * KCH, KCH), :] for ch in range(nch)], acc_scr)

    _three_stage_key_steps(q0, w, qk, sm, pv)
    o_ref[0] = (acc_scr[...] / l_scr[...]).astype(o_ref.dtype)


def _mla_attention(q, k, v, q_pos0, tq, w):
    nh, b, t, _ = q.shape
    s = k.shape[2]
    assert w >= tq and s % w == 0 and q_pos0 % CHUNK == 0 and tq % CHUNK == 0
    kern = functools.partial(_mla_kernel, tq=tq, w=w, q_pos0=q_pos0,
                             scale=(MLA_NOPE + MLA_ROPE) ** -0.5)
    return pl.pallas_call(
        kern, grid=(b, nh, t // tq),
        in_specs=[pl.BlockSpec((1, 1, tq, 256), lambda bi, h, i: (h, bi, i, 0)),
                  pl.BlockSpec((1, 1, s, 256), lambda bi, h, i: (h, bi, 0, 0)),
                  pl.BlockSpec((1, 1, s, 128), lambda bi, h, i: (h, bi, 0, 0))],
        out_specs=pl.BlockSpec((1, tq, 128), lambda bi, h, i: (bi, i, h)),
        out_shape=jax.ShapeDtypeStruct((b, t, nh * MLA_V), BF16),
        scratch_shapes=[pltpu.VMEM((2, w // KCH, tq, KCH), F32), pltpu.VMEM((2, w // KCH, tq, LANES), F32),
                        pltpu.VMEM((2, w // KCH, tq, KCH), BF16), pltpu.VMEM((2, tq, LANES), F32),
                        pltpu.VMEM((tq, LANES), F32), pltpu.VMEM((tq, LANES), F32),
                        pltpu.VMEM((tq, MLA_V), F32)],
        compiler_params=_params(3), name="mla_attn",
    )(q, k, v)


def _dsa_kernel(qd_ref, qi_ref, wi_ref, kd_ref, vd_ref, kilo_ref, kihi_ref, o_ref,
                key_scr, c_scr, cnt_scr, cand_scr, m_scr, l_scr, acc_scr,
                *, tq, w, wide_c, q_pos0, s_real, topk, idx_bits, scale):
    q0 = q_pos0 + pl.program_id(1) * tq
    sub = w // LANES

    qi_all = qi_ref[:, 0].reshape(IDX_HEADS // 2 * tq, LANES)
    wi = wi_ref[0]

    def score_step(k0, width, masked):
        lo = _dot_nt(qi_all, kilo_ref[0, pl.ds(k0, width), :])
        hi = _dot_nt(qi_all, kihi_ref[0, pl.ds(k0, width), :])
        score = jnp.zeros((tq, width), F32)
        for hp in range(IDX_HEADS // 2):
            rows = slice(hp * tq, (hp + 1) * tq)
            score = score + wi[:, 2 * hp:2 * hp + 1] * jnp.maximum(lo[rows], 0.0)
            score = score + wi[:, 2 * hp + 1:2 * hp + 2] * jnp.maximum(hi[rows], 0.0)
        bits = pltpu.bitcast(score, jnp.int32)
        key = bits ^ ((bits >> 31) & 0x7FFFFFFF)
        if masked:
            key = jnp.where(_chunk_mask(q0, k0, tq, width), key, INT_MIN)
        key_scr[:, pl.ds(k0, width)] = key

    k_end = _direct_key_steps(q0, w, score_step)
    n_sb = k_end // w

    qpos = q0 + lax.broadcasted_iota(jnp.int32, (tq, LANES), 0)
    visible = jnp.minimum(((qpos >> CHUNK_SHIFT) + 1) * CHUNK, s_real)
    k_target = jnp.minimum(visible, topk).astype(F32)

    def count(mode):
        cand = cand_scr[...]
        cval = c_scr[...]

        def span(k0, nblk, acc):
            for u in range(nblk):
                blk = key_scr[:, pl.ds(k0 + u * LANES, LANES)]
                if mode == "ge":
                    hit = blk >= cand
                else:
                    idx = k0 + u * LANES + lax.broadcasted_iota(jnp.int32, (tq, LANES), 1)
                    hit = jnp.where(blk == cval, idx, jnp.int32(2 ** 30)) < cand
                acc = acc + jnp.where(hit, 1.0, 0.0)
            return acc

        acc = lax.fori_loop(0, n_sb // 4,
                            lambda i, a: span(pl.multiple_of(i * 4 * w, 4 * w), 4 * sub, a),
                            jnp.zeros((tq, LANES), F32))
        acc = lax.fori_loop(4 * (n_sb // 4), n_sb,
                            lambda i, a: span(pl.multiple_of(i * w, w), sub, a), acc)
        cnt_scr[...] = jnp.broadcast_to(jnp.sum(acc, axis=1, keepdims=True), (tq, LANES))

    c_scr[...] = jnp.full((tq, LANES), INT_MIN, jnp.int32)
    cntc0 = jnp.full((tq, LANES), 1.0, F32) * k_end.astype(F32)

    def unsettled(cnt_c):
        return jnp.max(jnp.abs(cnt_c - k_target)) > 0.0

    def bit_cond(carry):
        b, pending, _ = carry
        return jnp.logical_and(b >= 0, pending)

    def bit_body(carry):
        b, _, cnt_c = carry
        cand = c_scr[...] + jnp.left_shift(jnp.int32(1), b)
        cand_scr[...] = cand
        count("ge")
        cnt = cnt_scr[...]
        take = cnt >= k_target
        c_scr[...] = jnp.where(take, cand, c_scr[...])
        cnt_c = jnp.where(take, cnt, cnt_c)
        return b - 1, unsettled(cnt_c), cnt_c

    _, ties, _ = lax.while_loop(bit_cond, bit_body, (jnp.int32(31), unsettled(cntc0), cntc0))

    @pl.when(ties)
    def _():
        cand_scr[...] = c_scr[...] + 1
        count("ge")
        need = k_target - cnt_scr[...]
        x = jnp.zeros((tq, LANES), jnp.int32)
        for bit in range(idx_bits - 1, -1, -1):
            cand_scr[...] = x + (1 << bit)
            count("eq_lt")
            x = jnp.where(cnt_scr[...] < need, x + (1 << bit), x)
        cand_scr[...] = x

        def demote(kb, carry):
            kb0 = pl.multiple_of(kb * 256, 256)
            cval = jnp.concatenate([c_scr[...]] * 2, axis=1)
            last = jnp.concatenate([cand_scr[...]] * 2, axis=1)
            idx = kb0 + lax.broadcasted_iota(jnp.int32, (tq, 256), 1)
            blk = key_scr[:, pl.ds(kb0, 256)]
            drop = jnp.where(blk == cval, idx, jnp.int32(-1)) > last
            key_scr[:, pl.ds(kb0, 256)] = jnp.where(drop, cval - 1, blk)
            return carry

        lax.fori_loop(0, k_end // 256, demote, 0)

    m_scr[...] = jnp.full(m_scr.shape, MASKED, F32)
    l_scr[...] = jnp.zeros(l_scr.shape, F32)
    acc_scr[...] = jnp.zeros(acc_scr.shape, F32)

    rows_c = DSA_GROUP * tq

    def attend(k0, width, masked):
        del masked
        chunks = range(width // KCH)
        cval = jnp.concatenate([c_scr[...]] * (KCH // LANES), axis=1)
        bias = [jnp.where(key_scr[:, pl.ds(k0 + ch * KCH, KCH)] >= cval, 0.0, MASKED) for ch in chunks]
        bias = [jnp.concatenate([b] * DSA_GROUP, axis=0) for b in bias]
        for c in range(DSA_KV_HEADS):
            qg = qd_ref[c * DSA_GROUP:(c + 1) * DSA_GROUP, 0].reshape(rows_c, DSA_HEAD_DIM)
            cols = slice(c * DSA_HEAD_DIM, (c + 1) * DSA_HEAD_DIM)
            s = [_dot_nt(qg, kd_ref[0, pl.ds(k0 + ch * KCH, KCH), cols]) + bias[ch] for ch in chunks]
            alpha, probs = _softmax_probs(s, [_row_max(s_ch) for s_ch in s],
                                          m_scr.at[c], l_scr.at[c], scale)
            _accumulate_pv(alpha, probs, [vd_ref[0, pl.ds(k0 + ch * KCH, KCH), cols] for ch in chunks],
                           acc_scr.at[c])

    _direct_key_steps(q0, w, attend, wide=wide_c)
    for c in range(DSA_KV_HEADS):
        o = acc_scr[c] / l_scr[c]
        for g in range(DSA_GROUP):
            h = c * DSA_GROUP + g
            o_ref[0, :, h * DSA_HEAD_DIM:(h + 1) * DSA_HEAD_DIM] = o[g * tq:(g + 1) * tq].astype(o_ref.dtype)


def _dsa_attention(qd, qi, wi, kd, vd, kilo, kihi, q_pos0, s_real, tq, w, wide_c):
    _, b, t, _ = qd.shape
    s = kd.shape[1]
    assert w >= tq and s % w == 0 and q_pos0 % CHUNK == 0 and tq % CHUNK == 0
    kern = functools.partial(
        _dsa_kernel, tq=tq, w=w, wide_c=wide_c, q_pos0=q_pos0, s_real=s_real,
        topk=min(IDX_TOPK, s_real // 4), idx_bits=int(s).bit_length(), scale=DSA_HEAD_DIM ** -0.5)
    heads = lambda: pl.BlockSpec((8, 1, tq, 128), lambda bi, i: (0, bi, i, 0))
    keys = lambda width: _resident((1, s, width), lambda bi, i: (bi, 0, 0))
    return pl.pallas_call(
        kern, grid=(b, t // tq),
        in_specs=[heads(), heads(), pl.BlockSpec((1, tq, 128), lambda bi, i: (bi, i, 0)),
                  keys(256), keys(256), keys(128), keys(128)],
        out_specs=pl.BlockSpec((1, tq, DSA_HEADS * DSA_HEAD_DIM), lambda bi, i: (bi, i, 0)),
        out_shape=jax.ShapeDtypeStruct((b, t, DSA_HEADS * DSA_HEAD_DIM), BF16),
        scratch_shapes=[pltpu.VMEM((tq, s), jnp.int32),
                        pltpu.VMEM((tq, LANES), jnp.int32), pltpu.VMEM((tq, LANES), F32),
                        pltpu.VMEM((tq, LANES), jnp.int32),
                        pltpu.VMEM((DSA_KV_HEADS, DSA_GROUP * tq, LANES), F32),
                        pltpu.VMEM((DSA_KV_HEADS, DSA_GROUP * tq, LANES), F32),
                        pltpu.VMEM((DSA_KV_HEADS, DSA_GROUP * tq, DSA_HEAD_DIM), F32)],
        compiler_params=_params(2), name="dsa_attn",
    )(qd, qi, wi, kd, vd, kilo, kihi)


def _layer_norm(v, g, b):
    mu = jnp.mean(v, axis=-1, keepdims=True)
    d = v - mu
    var = jnp.mean(d * d, axis=-1, keepdims=True)
    return d * lax.rsqrt(var + LN_EPS) * g + b


def _outln_kernel(mla_ref, dsa_ref, x_ref, wo_ref, g_ref, b_ref, o_ref, *, alpha, half):
    a = _dot(mla_ref[...], wo_ref[0:half, :]) + _dot(dsa_ref[...], wo_ref[half:, :])
    o_ref[...] = _layer_norm(alpha * x_ref[...] + a, g_ref[...], b_ref[...])


def _out_ln(mla_o, dsa_o, x2d, wp, alpha, tm):
    n, d = x2d.shape
    half = mla_o.shape[1]
    row = lambda w: pl.BlockSpec((tm, w), lambda i: (i, 0))
    return pl.pallas_call(
        functools.partial(_outln_kernel, alpha=alpha, half=half), grid=(n // tm,),
        in_specs=[row(half), row(dsa_o.shape[1]), row(d),
                  _resident(wp["w_o"].shape, lambda i: (0, 0)),
                  _resident((1, d), lambda i: (0, 0)), _resident((1, d), lambda i: (0, 0))],
        out_specs=row(d), out_shape=jax.ShapeDtypeStruct((n, d), F32),
        compiler_params=_params(1), name="out_ln",
    )(mla_o, dsa_o, x2d, wp["w_o"], wp["ln1_g"], wp["ln1_b"])


def _ffn_kernel(x_ref, wg_ref, wu_ref, wd_ref, g_ref, b_ref, o_ref, xb_scr, acc_scr, *, alpha):
    j = pl.program_id(1)

    @pl.when(j == 0)
    def _():
        xb_scr[...] = x_ref[...].astype(BF16)
        acc_scr[...] = jnp.zeros(acc_scr.shape, F32)

    xb = xb_scr[...]
    gate = _dot(xb, wg_ref[...])
    up = _dot(xb, wu_ref[...])
    hidden = gate * (1.0 / (1.0 + jnp.exp(-gate))) * up
    acc_scr[...] += _dot(hidden.astype(BF16), wd_ref[...])

    @pl.when(j == pl.num_programs(1) - 1)
    def _():
        o_ref[...] = _layer_norm(alpha * x_ref[...] + acc_scr[...], g_ref[...], b_ref[...])


def _ffn_ln(x2d, wp, alpha, tm, tf):
    n, d = x2d.shape
    dff = wp["w_gate"].shape[1]
    return pl.pallas_call(
        functools.partial(_ffn_kernel, alpha=alpha), grid=(n // tm, dff // tf),
        in_specs=[pl.BlockSpec((tm, d), lambda i, j: (i, 0)),
                  pl.BlockSpec((d, tf), lambda i, j: (0, j)),
                  pl.BlockSpec((d, tf), lambda i, j: (0, j)),
                  pl.BlockSpec((tf, d), lambda i, j: (j, 0)),
                  _resident((1, d), lambda i, j: (0, 0)), _resident((1, d), lambda i, j: (0, 0))],
        out_specs=pl.BlockSpec((tm, d), lambda i, j: (i, 0)),
        out_shape=jax.ShapeDtypeStruct((n, d), F32),
        scratch_shapes=[pltpu.VMEM((tm, d), BF16), pltpu.VMEM((tm, d), F32)],
        compiler_params=_params(2), name="ffn_ln",
    )(x2d, wp["w_gate"], wp["w_up"], wp["w_down"], wp["ln2_g"], wp["ln2_b"])


def _pack_weights(w_in, w_uq, g_q, w_ukv, g_kv, w_o, ln1_g, ln1_b, w_gate, w_up, w_down, ln2_g, ln2_b):
    d = w_in.shape[0]
    splits = (MLA_Q_LORA, MLA_KV_LORA, MLA_ROPE, DSA_HEADS * DSA_HEAD_DIM, DSA_KV_HEADS * DSA_HEAD_DIM,
              DSA_KV_HEADS * DSA_HEAD_DIM, IDX_HEADS * IDX_DIM, IDX_DIM, IDX_HEADS)
    offs = np.cumsum(splits)[:-1].tolist()
    c_q, c_kv, k_r, q_d, k_d, v_d, q_i, k_i, w_i = jnp.split(w_in, offs, axis=1)
    z = lambda n: jnp.zeros((d, n), w_in.dtype)
    w_in_p = jnp.concatenate(
        [c_q, c_kv, q_d, k_d, v_d, q_i, k_r, z(64), k_i, z(64), z(64), k_i, w_i, z(128 - IDX_HEADS)], axis=1)
    assert w_in_p.shape[1] == _IN_COLS_P
    w_uq_p = jnp.pad(w_uq, ((0, 0), (0, 0), (0, 256 - MLA_NOPE - MLA_ROPE)))
    return {
        "w_in": w_in_p.astype(BF16),
        "w_uq": w_uq_p.reshape(MLA_Q_LORA, MLA_HEADS * 256).astype(BF16),
        "g_q": g_q.reshape(1, -1), "g_kv": g_kv.reshape(1, -1),
        "w_uk": w_ukv[:, :, :MLA_NOPE].reshape(MLA_KV_LORA, MLA_HEADS * MLA_NOPE).astype(BF16),
        "w_uv": w_ukv[:, :, MLA_NOPE:].reshape(MLA_KV_LORA, MLA_HEADS * MLA_V).astype(BF16),
        "w_o": w_o.astype(BF16),
        "ln1_g": ln1_g.reshape(1, -1), "ln1_b": ln1_b.reshape(1, -1),
        "w_gate": w_gate.astype(BF16), "w_up": w_up.astype(BF16), "w_down": w_down.astype(BF16),
        "ln2_g": ln2_g.reshape(1, -1), "ln2_b": ln2_b.reshape(1, -1),
    }


def _trunk_layer(x, q_pos0, past, wp, alpha, cfg):
    b, t, d = x.shape
    n = b * t
    x2d = x.reshape(n, d)
    pos = jnp.tile(q_pos0 + jnp.arange(t, dtype=jnp.int32), b)
    (ckv, kpe, kd, vd, ki, q_mla, qd, qi, wi, kpeb, kdb, vdb, kilo, kihi) = _proj(x2d, pos, wp, cfg["tm_proj"])
    new_rows = (ckv.reshape(b, t, -1), kpe.reshape(b, t, -1),
                kd.reshape(b, t, DSA_KV_HEADS, DSA_HEAD_DIM), vd.reshape(b, t, DSA_KV_HEADS, DSA_HEAD_DIM),
                ki.reshape(b, t, -1))

    per_b = lambda a: a.reshape(b, t, a.shape[-1])
    ckv_all, kpeb_all, kdb_all, vdb_all, kilo_all, kihi_all = map(per_b, (ckv, kpeb, kdb, vdb, kilo, kihi))
    s_real = t
    if past is not None:
        p_ckv, p_kpe, p_kd, p_vd, p_ki = past
        s_real = p_ckv.shape[1] + t
        z64 = jnp.zeros(p_kpe.shape, BF16)
        cat = lambda c, nw: jnp.concatenate([c, nw], axis=1)
        ckv_all = cat(p_ckv, ckv_all)
        kpeb_all = cat(jnp.concatenate([p_kpe.astype(BF16), z64], axis=-1), kpeb_all)
        kdb_all = cat(p_kd.reshape(b, -1, 256).astype(BF16), kdb_all)
        vdb_all = cat(p_vd.reshape(b, -1, 256).astype(BF16), vdb_all)
        kilo_all = cat(jnp.concatenate([p_ki.astype(BF16), z64], axis=-1), kilo_all)
        kihi_all = cat(jnp.concatenate([z64, p_ki.astype(BF16)], axis=-1), kihi_all)
    s_pad = _round_up(s_real, max(cfg["w_mla"], cfg["w_dsa"]))
    if s_pad != s_real:
        padk = lambda a: jnp.pad(a, ((0, 0), (0, s_pad - s_real), (0, 0)))
        ckv_all, kpeb_all, kdb_all, vdb_all, kilo_all, kihi_all = map(
            padk, (ckv_all, kpeb_all, kdb_all, vdb_all, kilo_all, kihi_all))

    k_mla, v_mla = _kv_up(ckv_all.reshape(b * s_pad, -1), kpeb_all.reshape(b * s_pad, -1), wp, cfg["tm_kv"])
    k_mla = k_mla.reshape(MLA_HEADS, b, s_pad, 256)
    v_mla = v_mla.reshape(MLA_HEADS, b, s_pad, 128)
    mla_o = _mla_attention(q_mla.reshape(MLA_HEADS, b, t, 256), k_mla, v_mla, q_pos0,
                           cfg["tq_mla"], cfg["w_mla"])
    dsa_o = _dsa_attention(qd.reshape(DSA_HEADS, b, t, 128), qi.reshape(IDX_HEADS // 2, b, t, 128),
                           wi.reshape(b, t, 128), kdb_all, vdb_all, kilo_all, kihi_all,
                           q_pos0, s_real, cfg["tq_dsa"], cfg["w_dsa"], cfg["wide_dsa"])
    x1 = _out_ln(mla_o.reshape(n, -1), dsa_o.reshape(n, -1), x2d, wp, alpha, cfg["tm_out"])
    y = _ffn_ln(x1, wp, alpha, cfg["tm_ffn"], cfg["tf_ffn"])
    return y.reshape(b, t, d), new_rows


_PROMPT_CFG = dict(tm_proj=256, tm_kv=256, tq_mla=512, tq_dsa=128, w_mla=512, w_dsa=512, wide_dsa=4,
                   tm_out=256, tm_ffn=512, tf_ffn=512)
_SAMPLE_CFG = dict(tm_proj=256, tm_kv=256, tq_mla=64, tq_dsa=64, w_mla=256, w_dsa=256, wide_dsa=2,
                   tm_out=256, tm_ffn=512, tf_ffn=512)


def kernel(x_prompt, x_sample, cache_mla_ckv, cache_mla_kpe, cache_dsa_k, cache_dsa_v, cache_idx_k, w_in, w_uq, mla_q_norm_g, w_ukv, mla_kv_norm_g, w_o, ln1_g, ln1_b, w_gate, w_up, w_down, ln2_g, ln2_b):
    depth = w_in.shape[0]
    alpha = (2 * depth) ** 0.25
    past_len = cache_mla_ckv.shape[2]
    y_p, y_s = x_prompt, x_sample
    rows_p, rows_s = [], []
    for l in range(depth):
        wp = _pack_weights(w_in[l], w_uq[l], mla_q_norm_g[l], w_ukv[l], mla_kv_norm_g[l], w_o[l],
                           ln1_g[l], ln1_b[l], w_gate[l], w_up[l], w_down[l], ln2_g[l], ln2_b[l])
        y_p, r_p = _trunk_layer(y_p, 0, None, wp, alpha, _PROMPT_CFG)
        past = (cache_mla_ckv[l], cache_mla_kpe[l], cache_dsa_k[l], cache_dsa_v[l], cache_idx_k[l])
        y_s, r_s = _trunk_layer(y_s, past_len, past, wp, alpha, _SAMPLE_CFG)
        rows_p.append(r_p)
        rows_s.append(r_s)
    stack = lambda rows, i: jnp.stack([r[i] for r in rows], axis=0)
    return (y_p, y_s,
            stack(rows_p, 0), stack(rows_p, 1), stack(rows_p, 2), stack(rows_p, 3), stack(rows_p, 4),
            stack(rows_s, 0), stack(rows_s, 1), stack(rows_s, 2), stack(rows_s, 3), stack(rows_s, 4))
```

```python
import functools

import numpy as np
import jax
import jax.numpy as jnp
from jax import lax
from jax.experimental import pallas as pl
from jax.experimental.pallas import tpu as pltpu

CHUNK = 64
CHUNK_SHIFT = 6
ROPE_THETA = 10000.0
MLA_HEADS = 8
MLA_Q_LORA = 512
MLA_KV_LORA = 512
MLA_NOPE = 128
MLA_ROPE = 64
MLA_V = 128
DSA_HEADS = 8
DSA_KV_HEADS = 2
DSA_GROUP = DSA_HEADS // DSA_KV_HEADS
DSA_HEAD_DIM = 128
IDX_HEADS = 16
IDX_DIM = 64
IDX_TOPK = 256
LN_EPS = 1e-5
RMS_EPS = 1e-6

LANES = 128
MASKED = -1e30
INT_MIN = -2 ** 31
VMEM_LIMIT = 56 * 1024 * 1024

_C_CQ = 0
_C_CKV = 512
_C_QD = 1024
_C_KD = 2048
_C_VD = 2304
_C_QI = 2560
_C_KR = 3584
_C_KILO = 3712
_C_KIHI = 3840
_C_WI = 3968
_IN_COLS_P = 4096

F32 = jnp.float32
BF16 = jnp.bfloat16


def _dot(a, b):
    return jnp.dot(a, b, preferred_element_type=F32)


def _dot_nt(a, b):
    return lax.dot_general(a, b, (((1,), (1,)), ((), ())), preferred_element_type=F32)


def _params(n_axes, vmem=VMEM_LIMIT):
    return pltpu.CompilerParams(dimension_semantics=("arbitrary",) * n_axes, vmem_limit_bytes=vmem)


def _round_up(n, m):
    return -(-n // m) * m


def _resident(shape, index_map):
    return pl.BlockSpec(shape, index_map, pipeline_mode=pl.Buffered(1))


def _rope128(x, cos, sin_signed):
    return x * cos + pltpu.roll(x, 64, 1) * sin_signed


def _rope64(x, cos, sin_signed, first_half):
    rot = jnp.where(first_half, pltpu.roll(x, 96, 1), pltpu.roll(x, 32, 1))
    return x * cos + rot * sin_signed


def _rope_tables(pos):
    pos = pos.astype(F32)[:, None]
    lane = np.arange(LANES)

    def tables(dim):
        half = dim // 2
        inv = 1.0 / (ROPE_THETA ** (jnp.arange(half, dtype=F32) / half))
        ang = pos * inv[None, :]
        cos, sin = jnp.cos(ang), jnp.sin(ang)
        idx = lane % half
        sign = np.where((lane % dim) < half, -1.0, 1.0).astype(np.float32)
        return cos[:, idx], sin[:, idx] * sign[None, :]

    c128, s128 = tables(128)
    c64, s64 = tables(64)
    return c128, s128, c64, s64


def _proj_kernel(x_ref, w_ref, wuq_ref, gq_ref, gkv_ref, c128_ref, s128_ref, c64_ref, s64_ref,
                 ckv_ref, kpe_ref, kd_ref, vd_ref, ki_ref,
                 q_ref, qd_ref, qi_ref, wi_ref, kpeb_ref, kdb_ref, vdb_ref, kilo_ref, kihi_ref):
    xb = x_ref[...].astype(BF16)
    c128, s128 = c128_ref[...], s128_ref[...]
    c64, s64 = c64_ref[...], s64_ref[...]
    first_half = (lax.broadcasted_iota(jnp.int32, c64.shape, 1) % 64) < 32

    def seg(a, b):
        return _dot(xb, w_ref[:, a:b])

    def rms(v, g):
        return v * lax.rsqrt(jnp.mean(v * v, axis=-1, keepdims=True) + RMS_EPS) * g

    qn = rms(seg(_C_CQ, _C_CQ + MLA_Q_LORA), gq_ref[...]).astype(BF16)
    for h in range(MLA_HEADS):
        qh = _dot(qn, wuq_ref[:, h * 256:(h + 1) * 256])
        q_ref[h, :, 0:128] = qh[:, 0:128].astype(BF16)
        q_ref[h, :, 128:256] = _rope64(qh[:, 128:256], c64, s64, first_half).astype(BF16)

    ckv_ref[...] = rms(seg(_C_CKV, _C_CKV + MLA_KV_LORA), gkv_ref[...])

    kr = _rope64(seg(_C_KR, _C_KR + 128), c64, s64, first_half)
    kpe_ref[...] = kr[:, 0:MLA_ROPE]
    kpeb_ref[...] = kr.astype(BF16)

    for h in range(DSA_HEADS):
        a = _C_QD + h * 128
        qd_ref[h] = _rope128(seg(a, a + 128), c128, s128).astype(BF16)
    for c in range(DSA_KV_HEADS):
        a = _C_KD + c * 128
        kdc = _rope128(seg(a, a + 128), c128, s128)
        kd_ref[:, c * 128:(c + 1) * 128] = kdc
        kdb_ref[:, c * 128:(c + 1) * 128] = kdc.astype(BF16)
    vd = seg(_C_VD, _C_VD + 256)
    vd_ref[...] = vd
    vdb_ref[...] = vd.astype(BF16)

    for hp in range(IDX_HEADS // 2):
        a = _C_QI + hp * 128
        qi_ref[hp] = _rope64(seg(a, a + 128), c64, s64, first_half).astype(BF16)
    kilo = _rope64(seg(_C_KILO, _C_KILO + 128), c64, s64, first_half)
    ki_ref[...] = kilo[:, 0:IDX_DIM]
    kilo_ref[...] = kilo.astype(BF16)
    kihi_ref[...] = _rope64(seg(_C_KIHI, _C_KIHI + 128), c64, s64, first_half).astype(BF16)
    wi_ref[...] = seg(_C_WI, _C_WI + 128) * (IDX_DIM ** -0.5 * IDX_HEADS ** -0.5)


def _proj(x2d, pos, wp, tm):
    n, d = x2d.shape
    c128, s128, c64, s64 = _rope_tables(pos)
    row = lambda w: pl.BlockSpec((tm, w), lambda i: (i, 0))
    heads = lambda nh, w: pl.BlockSpec((nh, tm, w), lambda i: (0, i, 0))
    out_shapes = (
        jax.ShapeDtypeStruct((n, MLA_KV_LORA), F32),
        jax.ShapeDtypeStruct((n, MLA_ROPE), F32),
        jax.ShapeDtypeStruct((n, 256), F32),
        jax.ShapeDtypeStruct((n, 256), F32),
        jax.ShapeDtypeStruct((n, IDX_DIM), F32),
        jax.ShapeDtypeStruct((MLA_HEADS, n, 256), BF16),
        jax.ShapeDtypeStruct((DSA_HEADS, n, 128), BF16),
        jax.ShapeDtypeStruct((IDX_HEADS // 2, n, 128), BF16),
        jax.ShapeDtypeStruct((n, 128), F32),
        jax.ShapeDtypeStruct((n, 128), BF16),
        jax.ShapeDtypeStruct((n, 256), BF16),
        jax.ShapeDtypeStruct((n, 256), BF16),
        jax.ShapeDtypeStruct((n, 128), BF16),
        jax.ShapeDtypeStruct((n, 128), BF16),
    )
    out_specs = (row(MLA_KV_LORA), row(MLA_ROPE), row(256), row(256), row(IDX_DIM),
                 heads(MLA_HEADS, 256), heads(DSA_HEADS, 128), heads(IDX_HEADS // 2, 128),
                 row(128), row(128), row(256), row(256), row(128), row(128))
    in_specs = [row(d),
                _resident((d, _IN_COLS_P), lambda i: (0, 0)),
                _resident((MLA_Q_LORA, MLA_HEADS * 256), lambda i: (0, 0)),
                _resident((1, MLA_Q_LORA), lambda i: (0, 0)),
                _resident((1, MLA_KV_LORA), lambda i: (0, 0)),
                row(128), row(128), row(128), row(128)]
    return pl.pallas_call(
        _proj_kernel, grid=(n // tm,), in_specs=in_specs, out_specs=out_specs, out_shape=out_shapes,
        compiler_params=_params(1), name="proj",
    )(x2d, wp["w_in"], wp["w_uq"], wp["g_q"], wp["g_kv"], c128, s128, c64, s64)


def _kvup_kernel(ckv_ref, kpeb_ref, wk_ref, wv_ref, k_ref, v_ref):
    cb = ckv_ref[...].astype(BF16)
    kpe = kpeb_ref[...]
    for h in range(MLA_HEADS):
        k_ref[h, :, 0:128] = _dot(cb, wk_ref[:, h * 128:(h + 1) * 128]).astype(BF16)
        k_ref[h, :, 128:256] = kpe
        v_ref[h] = _dot(cb, wv_ref[:, h * 128:(h + 1) * 128]).astype(BF16)


def _kv_up(ckv2d, kpeb2d, wp, tm):
    n = ckv2d.shape[0]
    return pl.pallas_call(
        _kvup_kernel, grid=(n // tm,),
        in_specs=[pl.BlockSpec((tm, MLA_KV_LORA), lambda i: (i, 0)),
                  pl.BlockSpec((tm, 128), lambda i: (i, 0)),
                  _resident((MLA_KV_LORA, MLA_HEADS * MLA_NOPE), lambda i: (0, 0)),
                  _resident((MLA_KV_LORA, MLA_HEADS * MLA_V), lambda i: (0, 0))],
        out_specs=(pl.BlockSpec((MLA_HEADS, tm, 256), lambda i: (0, i, 0)),
                   pl.BlockSpec((MLA_HEADS, tm, 128), lambda i: (0, i, 0))),
        out_shape=(jax.ShapeDtypeStruct((MLA_HEADS, n, 256), BF16),
                   jax.ShapeDtypeStruct((MLA_HEADS, n, 128), BF16)),
        compiler_params=_params(1), name="kv_up",
    )(ckv2d, kpeb2d, wp["w_uk"], wp["w_uv"])


LOG2E = 1.4426950408889634


def _row_max(s):
    return jnp.broadcast_to(jnp.max(s, axis=1, keepdims=True), (s.shape[0], LANES))


KCH = 256


def _softmax_probs(s, smax, m_scr, l_scr, scale):
    coef = scale * LOG2E
    m_prev = m_scr[...]
    m_next = m_prev
    for sm in smax:
        m_next = jnp.maximum(m_next, sm)
    m_wide = jnp.concatenate([m_next] * (KCH // LANES), axis=1)
    alpha = jnp.exp2((m_prev - m_next) * coef)
    l_sum, probs = None, []
    for s_ch in s:
        p = jnp.exp2((s_ch - m_wide) * coef)
        p_sum = jnp.sum(p, axis=1, keepdims=True)
        l_sum = p_sum if l_sum is None else l_sum + p_sum
        probs.append(p.astype(BF16))
    l_scr[...] = alpha * l_scr[...] + l_sum
    m_scr[...] = m_next
    return alpha, probs


def _accumulate_pv(alpha, probs, v, acc_scr):
    pv = None
    for p_ch, v_ch in zip(probs, v):
        pv_ch = _dot(p_ch, v_ch)
        pv = pv_ch if pv is None else pv + pv_ch
    acc_scr[...] = acc_scr[...] * alpha + pv


def _chunk_mask(q0, kb0, tq, tk):
    qch = (q0 + lax.broadcasted_iota(jnp.int32, (tq, 1), 0)) >> CHUNK_SHIFT
    kch = (kb0 + lax.broadcasted_iota(jnp.int32, (1, tk), 1)) >> CHUNK_SHIFT
    return kch <= qch


def _three_stage_key_steps(q0, w, qk, sm, pv):
    n = q0 // w
    at = lambda i: pl.multiple_of(i * w, w)

    @pl.when(n == 0)
    def _():
        qk(0, 0)
        sm(0, 0, True)
        pv(0, 0)

    @pl.when(n >= 1)
    def _():
        qk(0, 0)
        qk(w, 1)
        sm(0, 0, False)
        pairs = (n - 1) // 2

        def body(t, carry):
            i = 2 * t
            qk(at(i + 2), 0)
            sm(1, at(i + 1), False)
            pv(0, at(i))
            qk(at(i + 3), 1)
            sm(0, at(i + 2), False)
            pv(1, at(i + 1))
            return carry

        lax.fori_loop(0, pairs, body, 0)
        i = 2 * pairs

        @pl.when(n - i == 1)
        def _():
            sm(1, at(i + 1), True)
            pv(0, at(i))
            pv(1, at(i + 1))

        @pl.when(n - i == 2)
        def _():
            qk(at(i + 2), 0)
            sm(1, at(i + 1), False)
            pv(0, at(i))
            sm(0, at(i + 2), True)
            pv(1, at(i + 1))
            pv(0, at(i + 2))

    return (n + 1) * w


def _direct_key_steps(q0, w, fn, wide=2):
    n = q0 // w
    ww = wide * w
    lax.fori_loop(0, n // wide, lambda i, c: (fn(pl.multiple_of(i * ww, ww), ww, False), c)[1], 0)
    lax.fori_loop((n // wide) * wide, n, lambda i, c: (fn(pl.multiple_of(i * w, w), w, False), c)[1], 0)
    fn(pl.multiple_of(n * w, w), w, True)
    return (n + 1) * w


def _mla_kernel(q_ref, k_ref, v_ref, o_ref, s_scr, smax_scr, p_scr, alpha_scr, m_scr, l_scr, acc_scr,
                *, tq, w, q_pos0, scale):
    q0 = q_pos0 + pl.program_id(2) * tq
    m_scr[...] = jnp.full(m_scr.shape, MASKED, F32)
    l_scr[...] = jnp.zeros(l_scr.shape, F32)
    acc_scr[...] = jnp.zeros(acc_scr.shape, F32)
    q = q_ref[0, 0]

    nch = w // KCH

    def qk(k0, buf):
        for ch in range(nch):
            s = _dot_nt(q, k_ref[0, 0, pl.ds(k0 + ch * KCH, KCH), :])
            s_scr[buf, ch] = s
            smax_scr[buf, ch] = _row_max(s)

    def sm(buf, k0, masked):
        s = [s_scr[buf, ch] for ch in range(nch)]
        if masked:
            s = [jnp.where(_chunk_mask(q0, k0 + ch * KCH, tq, KCH), s[ch], MASKED) for ch in range(nch)]
            smax = [_row_max(s_ch) for s_ch in s]
        else:
            smax = [smax_scr[buf, ch] for ch in range(nch)]
        alpha, probs = _softmax_probs(s, smax, m_scr, l_scr, scale)
        alpha_scr[buf] = alpha
        for ch in range(nch):
            p_scr[buf, ch] = probs[ch]

    def pv(buf, k0):
        _accumulate_pv(alpha_scr[buf], [p_scr[buf, ch] for ch in range(nch)],
                       [v_ref[0, 0, pl.ds(k0 + ch * KCH, KCH), :] for ch in range(nch)], acc_scr)

    _three_stage_key_steps(q0, w, qk, sm, pv)
    o_ref[0] = (acc_scr[...] / l_scr[...]).astype(o_ref.dtype)


def _mla_attention(q, k, v, q_pos0, tq, w):
    nh, b, t, _ = q.shape
    s = k.shape[2]
    assert w >= tq and s % w == 0 and q_pos0 % CHUNK == 0 and tq % CHUNK == 0
    kern = functools.partial(_mla_kernel, tq=tq, w=w, q_pos0=q_pos0,
                             scale=(MLA_NOPE + MLA_ROPE) ** -0.5)
    return pl.pallas_call(
        kern, grid=(b, nh, t // tq),
        in_specs=[pl.BlockSpec((1, 1, tq, 256), lambda bi, h, i: (h, bi, i, 0)),
                  pl.BlockSpec((1, 1, s, 256), lambda bi, h, i: (h, bi, 0, 0)),
                  pl.BlockSpec((1, 1, s, 128), lambda bi, h, i: (h, bi, 0, 0))],
        out_specs=pl.BlockSpec((1, tq, 128), lambda bi, h, i: (bi, i, h)),
        out_shape=jax.ShapeDtypeStruct((b, t, nh * MLA_V), BF16),
        scratch_shapes=[pltpu.VMEM((2, w // KCH, tq, KCH), F32), pltpu.VMEM((2, w // KCH, tq, LANES), F32),
                        pltpu.VMEM((2, w // KCH, tq, KCH), BF16), pltpu.VMEM((2, tq, LANES), F32),
                        pltpu.VMEM((tq, LANES), F32), pltpu.VMEM((tq, LANES), F32),
                        pltpu.VMEM((tq, MLA_V), F32)],
        compiler_params=_params(3), name="mla_attn",
    )(q, k, v)


def _dsa_kernel(qd_ref, qi_ref, wi_ref, kd_ref, vd_ref, kilo_ref, kihi_ref, o_ref,
                key_scr, top_scr, c_scr, hi_scr, cnt_scr, cand_scr, m_scr, l_scr, acc_scr,
                *, tq, w, wide_c, q_pos0, s_real, topk, idx_bits, scale):
    q0 = q_pos0 + pl.program_id(1) * tq
    sub = w // LANES

    qi_all = qi_ref[:, 0].reshape(IDX_HEADS // 2 * tq, LANES)
    wi = wi_ref[0]

    def to_key(f):
        bits = pltpu.bitcast(f, jnp.int32)
        return bits ^ ((bits >> 31) & 0x7FFFFFFF)

    def score_step(k0, width, masked):
        lo = _dot_nt(qi_all, kilo_ref[0, pl.ds(k0, width), :])
        hi = _dot_nt(qi_all, kihi_ref[0, pl.ds(k0, width), :])
        score = jnp.zeros((tq, width), F32)
        for hp in range(IDX_HEADS // 2):
            rows = slice(hp * tq, (hp + 1) * tq)
            score = score + wi[:, 2 * hp:2 * hp + 1] * jnp.maximum(lo[rows], 0.0)
            score = score + wi[:, 2 * hp + 1:2 * hp + 2] * jnp.maximum(hi[rows], 0.0)
        key = to_key(score)
        if masked:
            visible_here = _chunk_mask(q0, k0, tq, width)
            key = jnp.where(visible_here, key, INT_MIN)
            score = jnp.where(visible_here, score, -jnp.inf)
        key_scr[:, pl.ds(k0, width)] = key
        top1, top2 = top_scr[0], top_scr[1]
        for u in range(width // LANES):
            x = score[:, u * LANES:(u + 1) * LANES]
            top2 = jnp.maximum(top2, jnp.minimum(top1, x))
            top1 = jnp.maximum(top1, x)
        top_scr[0] = top1
        top_scr[1] = top2

    top_scr[...] = jnp.full(top_scr.shape, -jnp.inf, F32)
    k_end = _direct_key_steps(q0, w, score_step)
    n_sb = k_end // w

    qpos = q0 + lax.broadcasted_iota(jnp.int32, (tq, LANES), 0)
    visible = jnp.minimum(((qpos >> CHUNK_SHIFT) + 1) * CHUNK, s_real)
    k_target = jnp.minimum(visible, topk).astype(F32)

    def count(mode):
        cand = cand_scr[...]
        cval = c_scr[...]

        def span(k0, nblk, acc):
            for u in range(nblk):
                blk = key_scr[:, pl.ds(k0 + u * LANES, LANES)]
                if mode == "ge":
                    hit = blk >= cand
                else:
                    idx = k0 + u * LANES + lax.broadcasted_iota(jnp.int32, (tq, LANES), 1)
                    hit = jnp.where(blk == cval, idx, jnp.int32(2 ** 30)) < cand
                acc = acc + jnp.where(hit, 1.0, 0.0)
            return acc

        acc = lax.fori_loop(0, n_sb // 4,
                            lambda i, a: span(pl.multiple_of(i * 4 * w, 4 * w), 4 * sub, a),
                            jnp.zeros((tq, LANES), F32))
        acc = lax.fori_loop(4 * (n_sb // 4), n_sb,
                            lambda i, a: span(pl.multiple_of(i * w, w), sub, a), acc)
        cnt_scr[...] = jnp.broadcast_to(jnp.sum(acc, axis=1, keepdims=True), (tq, LANES))

    lane_min = lambda a: jnp.broadcast_to(jnp.min(a, axis=1, keepdims=True), (tq, LANES))
    lane_max = lambda a: jnp.broadcast_to(jnp.max(a, axis=1, keepdims=True), (tq, LANES))
    lo_f = lane_min(top_scr[1])
    hi_f = jnp.where(k_target > float(LANES), lane_max(top_scr[1]), lane_max(top_scr[0]))
    c_lo = jnp.where(lo_f == -jnp.inf, INT_MIN, jnp.where(lo_f == 0.0, -1, to_key(lo_f)))
    c_hi = jnp.where(hi_f == 0.0, 0, to_key(hi_f))

    c_scr[...] = c_lo
    hi_scr[...] = c_hi + 1

    def midpoint(lo, hi):
        return (lo >> 1) + (hi >> 1) + (lo & hi & 1)

    def bisect(carry):
        it, _, cnt_lo = carry
        lo, hi = c_scr[...], hi_scr[...]
        mid = midpoint(lo, hi)
        cand = jnp.where(cnt_lo == k_target, lo, mid)
        cand_scr[...] = cand
        count("ge")
        cnt = cnt_scr[...]
        take = cnt >= k_target
        lo, hi = jnp.where(take, cand, lo), jnp.where(take, hi, cand)
        cnt_lo = jnp.where(take, cnt, cnt_lo)
        c_scr[...] = lo
        hi_scr[...] = hi
        open_rows = jnp.logical_and(cnt_lo != k_target, midpoint(lo, hi) != lo)
        return it + 1, jnp.max(jnp.where(open_rows, 1.0, 0.0)) > 0.0, cnt_lo

    _, _, cnt_lo = lax.while_loop(lambda carry: jnp.logical_and(carry[1], carry[0] < 40), bisect,
                                  (jnp.int32(0), jnp.bool_(True), jnp.full((tq, LANES), -1.0, F32)))
    ties = jnp.max(jnp.abs(cnt_lo - k_target)) > 0.0

    @pl.when(ties)
    def _():
        cand_scr[...] = c_scr[...] + 1
        count("ge")
        need = k_target - cnt_scr[...]
        x = jnp.zeros((tq, LANES), jnp.int32)
        for bit in range(idx_bits - 1, -1, -1):
            cand_scr[...] = x + (1 << bit)
            count("eq_lt")
            x = jnp.where(cnt_scr[...] < need, x + (1 << bit), x)
        cand_scr[...] = x

        def demote(kb, carry):
            kb0 = pl.multiple_of(kb * 256, 256)
            cval = jnp.concatenate([c_scr[...]] * 2, axis=1)
            last = jnp.concatenate([cand_scr[...]] * 2, axis=1)
            idx = kb0 + lax.broadcasted_iota(jnp.int32, (tq, 256), 1)
            blk = key_scr[:, pl.ds(kb0, 256)]
            drop = jnp.where(blk == cval, idx, jnp.int32(-1)) > last
            key_scr[:, pl.ds(kb0, 256)] = jnp.where(drop, cval - 1, blk)
            return carry

        lax.fori_loop(0, k_end // 256, demote, 0)

    m_scr[...] = jnp.full(m_scr.shape, MASKED, F32)
    l_scr[...] = jnp.zeros(l_scr.shape, F32)
    acc_scr[...] = jnp.zeros(acc_scr.shape, F32)

    rows_c = DSA_GROUP * tq

    def attend(k0, width, masked):
        del masked
        chunks = range(width // KCH)
        cval = jnp.concatenate([c_scr[...]] * (KCH // LANES), axis=1)
        bias = [jnp.where(key_scr[:, pl.ds(k0 + ch * KCH, KCH)] >= cval, 0.0, MASKED) for ch in chunks]
        bias = [jnp.concatenate([b] * DSA_GROUP, axis=0) for b in bias]
        for c in range(DSA_KV_HEADS):
            qg = qd_ref[c * DSA_GROUP:(c + 1) * DSA_GROUP, 0].reshape(rows_c, DSA_HEAD_DIM)
            cols = slice(c * DSA_HEAD_DIM, (c + 1) * DSA_HEAD_DIM)
            s = [_dot_nt(qg, kd_ref[0, pl.ds(k0 + ch * KCH, KCH), cols]) + bias[ch] for ch in chunks]
            alpha, probs = _softmax_probs(s, [_row_max(s_ch) for s_ch in s],
                                          m_scr.at[c], l_scr.at[c], scale)
            _accumulate_pv(alpha, probs, [vd_ref[0, pl.ds(k0 + ch * KCH, KCH), cols] for ch in chunks],
                           acc_scr.at[c])

    _direct_key_steps(q0, w, attend, wide=wide_c)
    for c in range(DSA_KV_HEADS):
        o = acc_scr[c] / l_scr[c]
        for g in range(DSA_GROUP):
            h = c * DSA_GROUP + g
            o_ref[0, :, h * DSA_HEAD_DIM:(h + 1) * DSA_HEAD_DIM] = o[g * tq:(g + 1) * tq].astype(o_ref.dtype)


def _dsa_attention(qd, qi, wi, kd, vd, kilo, kihi, q_pos0, s_real, tq, w, wide_c):
    _, b, t, _ = qd.shape
    s = kd.shape[1]
    assert w >= tq and s % w == 0 and q_pos0 % CHUNK == 0 and tq % CHUNK == 0
    kern = functools.partial(
        _dsa_kernel, tq=tq, w=w, wide_c=wide_c, q_pos0=q_pos0, s_real=s_real,
        topk=min(IDX_TOPK, s_real // 4), idx_bits=int(s).bit_length(), scale=DSA_HEAD_DIM ** -0.5)
    heads = lambda: pl.BlockSpec((8, 1, tq, 128), lambda bi, i: (0, bi, i, 0))
    keys = lambda width: _resident((1, s, width), lambda bi, i: (bi, 0, 0))
    return pl.pallas_call(
        kern, grid=(b, t // tq),
        in_specs=[heads(), heads(), pl.BlockSpec((1, tq, 128), lambda bi, i: (bi, i, 0)),
                  keys(256), keys(256), keys(128), keys(128)],
        out_specs=pl.BlockSpec((1, tq, DSA_HEADS * DSA_HEAD_DIM), lambda bi, i: (bi, i, 0)),
        out_shape=jax.ShapeDtypeStruct((b, t, DSA_HEADS * DSA_HEAD_DIM), BF16),
        scratch_shapes=[pltpu.VMEM((tq, s), jnp.int32), pltpu.VMEM((2, tq, LANES), F32),
                        pltpu.VMEM((tq, LANES), jnp.int32), pltpu.VMEM((tq, LANES), jnp.int32),
                        pltpu.VMEM((tq, LANES), F32),
                        pltpu.VMEM((tq, LANES), jnp.int32),
                        pltpu.VMEM((DSA_KV_HEADS, DSA_GROUP * tq, LANES), F32),
                        pltpu.VMEM((DSA_KV_HEADS, DSA_GROUP * tq, LANES), F32),
                        pltpu.VMEM((DSA_KV_HEADS, DSA_GROUP * tq, DSA_HEAD_DIM), F32)],
        compiler_params=_params(2), name="dsa_attn",
    )(qd, qi, wi, kd, vd, kilo, kihi)


def _layer_norm(v, g, b):
    mu = jnp.mean(v, axis=-1, keepdims=True)
    d = v - mu
    var = jnp.mean(d * d, axis=-1, keepdims=True)
    return d * lax.rsqrt(var + LN_EPS) * g + b


def _outln_kernel(mla_ref, dsa_ref, x_ref, wo_ref, g_ref, b_ref, o_ref, *, alpha, half):
    a = _dot(mla_ref[...], wo_ref[0:half, :]) + _dot(dsa_ref[...], wo_ref[half:, :])
    o_ref[...] = _layer_norm(alpha * x_ref[...] + a, g_ref[...], b_ref[...])


def _out_ln(mla_o, dsa_o, x2d, wp, alpha, tm):
    n, d = x2d.shape
    half = mla_o.shape[1]
    row = lambda w: pl.BlockSpec((tm, w), lambda i: (i, 0))
    return pl.pallas_call(
        functools.partial(_outln_kernel, alpha=alpha, half=half), grid=(n // tm,),
        in_specs=[row(half), row(dsa_o.shape[1]), row(d),
                  _resident(wp["w_o"].shape, lambda i: (0, 0)),
                  _resident((1, d), lambda i: (0, 0)), _resident((1, d), lambda i: (0, 0))],
        out_specs=row(d), out_shape=jax.ShapeDtypeStruct((n, d), F32),
        compiler_params=_params(1), name="out_ln",
    )(mla_o, dsa_o, x2d, wp["w_o"], wp["ln1_g"], wp["ln1_b"])


def _ffn_kernel(x_ref, wg_ref, wu_ref, wd_ref, g_ref, b_ref, o_ref, xb_scr, acc_scr, *, alpha):
    j = pl.program_id(1)

    @pl.when(j == 0)
    def _():
        xb_scr[...] = x_ref[...].astype(BF16)
        acc_scr[...] = jnp.zeros(acc_scr.shape, F32)

    xb = xb_scr[...]
    gate = _dot(xb, wg_ref[...])
    up = _dot(xb, wu_ref[...])
    hidden = gate * (1.0 / (1.0 + jnp.exp(-gate))) * up
    acc_scr[...] += _dot(hidden.astype(BF16), wd_ref[...])

    @pl.when(j == pl.num_programs(1) - 1)
    def _():
        o_ref[...] = _layer_norm(alpha * x_ref[...] + acc_scr[...], g_ref[...], b_ref[...])


def _ffn_ln(x2d, wp, alpha, tm, tf):
    n, d = x2d.shape
    dff = wp["w_gate"].shape[1]
    return pl.pallas_call(
        functools.partial(_ffn_kernel, alpha=alpha), grid=(n // tm, dff // tf),
        in_specs=[pl.BlockSpec((tm, d), lambda i, j: (i, 0)),
                  pl.BlockSpec((d, tf), lambda i, j: (0, j)),
                  pl.BlockSpec((d, tf), lambda i, j: (0, j)),
                  pl.BlockSpec((tf, d), lambda i, j: (j, 0)),
                  _resident((1, d), lambda i, j: (0, 0)), _resident((1, d), lambda i, j: (0, 0))],
        out_specs=pl.BlockSpec((tm, d), lambda i, j: (i, 0)),
        out_shape=jax.ShapeDtypeStruct((n, d), F32),
        scratch_shapes=[pltpu.VMEM((tm, d), BF16), pltpu.VMEM((tm, d), F32)],
        compiler_params=_params(2), name="ffn_ln",
    )(x2d, wp["w_gate"], wp["w_up"], wp["w_down"], wp["ln2_g"], wp["ln2_b"])


def _pack_weights(w_in, w_uq, g_q, w_ukv, g_kv, w_o, ln1_g, ln1_b, w_gate, w_up, w_down, ln2_g, ln2_b):
    d = w_in.shape[0]
    splits = (MLA_Q_LORA, MLA_KV_LORA, MLA_ROPE, DSA_HEADS * DSA_HEAD_DIM, DSA_KV_HEADS * DSA_HEAD_DIM,
              DSA_KV_HEADS * DSA_HEAD_DIM, IDX_HEADS * IDX_DIM, IDX_DIM, IDX_HEADS)
    offs = np.cumsum(splits)[:-1].tolist()
    c_q, c_kv, k_r, q_d, k_d, v_d, q_i, k_i, w_i = jnp.split(w_in, offs, axis=1)
    z = lambda n: jnp.zeros((d, n), w_in.dtype)
    w_in_p = jnp.concatenate(
        [c_q, c_kv, q_d, k_d, v_d, q_i, k_r, z(64), k_i, z(64), z(64), k_i, w_i, z(128 - IDX_HEADS)], axis=1)
    assert w_in_p.shape[1] == _IN_COLS_P
    w_uq_p = jnp.pad(w_uq, ((0, 0), (0, 0), (0, 256 - MLA_NOPE - MLA_ROPE)))
    return {
        "w_in": w_in_p.astype(BF16),
        "w_uq": w_uq_p.reshape(MLA_Q_LORA, MLA_HEADS * 256).astype(BF16),
        "g_q": g_q.reshape(1, -1), "g_kv": g_kv.reshape(1, -1),
        "w_uk": w_ukv[:, :, :MLA_NOPE].reshape(MLA_KV_LORA, MLA_HEADS * MLA_NOPE).astype(BF16),
        "w_uv": w_ukv[:, :, MLA_NOPE:].reshape(MLA_KV_LORA, MLA_HEADS * MLA_V).astype(BF16),
        "w_o": w_o.astype(BF16),
        "ln1_g": ln1_g.reshape(1, -1), "ln1_b": ln1_b.reshape(1, -1),
        "w_gate": w_gate.astype(BF16), "w_up": w_up.astype(BF16), "w_down": w_down.astype(BF16),
        "ln2_g": ln2_g.reshape(1, -1), "ln2_b": ln2_b.reshape(1, -1),
    }


def _trunk_layer(x, q_pos0, past, wp, alpha, cfg):
    b, t, d = x.shape
    n = b * t
    x2d = x.reshape(n, d)
    pos = jnp.tile(q_pos0 + jnp.arange(t, dtype=jnp.int32), b)
    (ckv, kpe, kd, vd, ki, q_mla, qd, qi, wi, kpeb, kdb, vdb, kilo, kihi) = _proj(x2d, pos, wp, cfg["tm_proj"])
    new_rows = (ckv.reshape(b, t, -1), kpe.reshape(b, t, -1),
                kd.reshape(b, t, DSA_KV_HEADS, DSA_HEAD_DIM), vd.reshape(b, t, DSA_KV_HEADS, DSA_HEAD_DIM),
                ki.reshape(b, t, -1))

    per_b = lambda a: a.reshape(b, t, a.shape[-1])
    ckv_all, kpeb_all, kdb_all, vdb_all, kilo_all, kihi_all = map(per_b, (ckv, kpeb, kdb, vdb, kilo, kihi))
    s_real = t
    if past is not None:
        p_ckv, p_kpe, p_kd, p_vd, p_ki = past
        s_real = p_ckv.shape[1] + t
        z64 = jnp.zeros(p_kpe.shape, BF16)
        cat = lambda c, nw: jnp.concatenate([c, nw], axis=1)
        ckv_all = cat(p_ckv, ckv_all)
        kpeb_all = cat(jnp.concatenate([p_kpe.astype(BF16), z64], axis=-1), kpeb_all)
        kdb_all = cat(p_kd.reshape(b, -1, 256).astype(BF16), kdb_all)
        vdb_all = cat(p_vd.reshape(b, -1, 256).astype(BF16), vdb_all)
        kilo_all = cat(jnp.concatenate([p_ki.astype(BF16), z64], axis=-1), kilo_all)
        kihi_all = cat(jnp.concatenate([z64, p_ki.astype(BF16)], axis=-1), kihi_all)
    s_pad = _round_up(s_real, max(cfg["w_mla"], cfg["w_dsa"]))
    if s_pad != s_real:
        padk = lambda a: jnp.pad(a, ((0, 0), (0, s_pad - s_real), (0, 0)))
        ckv_all, kpeb_all, kdb_all, vdb_all, kilo_all, kihi_all = map(
            padk, (ckv_all, kpeb_all, kdb_all, vdb_all, kilo_all, kihi_all))

    k_mla, v_mla = _kv_up(ckv_all.reshape(b * s_pad, -1), kpeb_all.reshape(b * s_pad, -1), wp, cfg["tm_kv"])
    k_mla = k_mla.reshape(MLA_HEADS, b, s_pad, 256)
    v_mla = v_mla.reshape(MLA_HEADS, b, s_pad, 128)
    mla_o = _mla_attention(q_mla.reshape(MLA_HEADS, b, t, 256), k_mla, v_mla, q_pos0,
                           cfg["tq_mla"], cfg["w_mla"])
    dsa_o = _dsa_attention(qd.reshape(DSA_HEADS, b, t, 128), qi.reshape(IDX_HEADS // 2, b, t, 128),
                           wi.reshape(b, t, 128), kdb_all, vdb_all, kilo_all, kihi_all,
                           q_pos0, s_real, cfg["tq_dsa"], cfg["w_dsa"], cfg["wide_dsa"])
    x1 = _out_ln(mla_o.reshape(n, -1), dsa_o.reshape(n, -1), x2d, wp, alpha, cfg["tm_out"])
    y = _ffn_ln(x1, wp, alpha, cfg["tm_ffn"], cfg["tf_ffn"])
    return y.reshape(b, t, d), new_rows


_PROMPT_CFG = dict(tm_proj=256, tm_kv=256, tq_mla=512, tq_dsa=128, w_mla=512, w_dsa=512, wide_dsa=4,
                   tm_out=256, tm_ffn=512, tf_ffn=512)
_SAMPLE_CFG = dict(tm_proj=256, tm_kv=256, tq_mla=64, tq_dsa=64, w_mla=256, w_dsa=256, wide_dsa=2,
                   tm_out=256, tm_ffn=512, tf_ffn=512)


def kernel(x_prompt, x_sample, cache_mla_ckv, cache_mla_kpe, cache_dsa_k, cache_dsa_v, cache_idx_k, w_in, w_uq, mla_q_norm_g, w_ukv, mla_kv_norm_g, w_o, ln1_g, ln1_b, w_gate, w_up, w_down, ln2_g, ln2_b):
    depth = w_in.shape[0]
    alpha = (2 * depth) ** 0.25
    past_len = cache_mla_ckv.shape[2]
    y_p, y_s = x_prompt, x_sample
    rows_p, rows_s = [], []
    for l in range(depth):
        wp = _pack_weights(w_in[l], w_uq[l], mla_q_norm_g[l], w_ukv[l], mla_kv_norm_g[l], w_o[l],
                           ln1_g[l], ln1_b[l], w_gate[l], w_up[l], w_down[l], ln2_g[l], ln2_b[l])
        y_p, r_p = _trunk_layer(y_p, 0, None, wp, alpha, _PROMPT_CFG)
        past = (cache_mla_ckv[l], cache_mla_kpe[l], cache_dsa_k[l], cache_dsa_v[l], cache_idx_k[l])
        y_s, r_s = _trunk_layer(y_s, past_len, past, wp, alpha, _SAMPLE_CFG)
        rows_p.append(r_p)
        rows_s.append(r_s)
    stack = lambda rows, i: jnp.stack([r[i] for r in rows], axis=0)
    return (y_p, y_s,
            stack(rows_p, 0), stack(rows_p, 1), stack(rows_p, 2), stack(rows_p, 3), stack(rows_p, 4),
            stack(rows_s, 0), stack(rows_s, 1), stack(rows_s, 2), stack(rows_s, 3), stack(rows_s, 4))
```

```python
import functools

import numpy as np
import jax
import jax.numpy as jnp
from jax import lax
from jax.experimental import pallas as pl
from jax.experimental.pallas import tpu as pltpu

CHUNK = 64
CHUNK_SHIFT = 6
ROPE_THETA = 10000.0
MLA_HEADS = 8
MLA_Q_LORA = 512
MLA_KV_LORA = 512
MLA_NOPE = 128
MLA_ROPE = 64
MLA_V = 128
DSA_HEADS = 8
DSA_KV_HEADS = 2
DSA_GROUP = DSA_HEADS // DSA_KV_HEADS
DSA_HEAD_DIM = 128
IDX_HEADS = 16
IDX_DIM = 64
IDX_TOPK = 256
LN_EPS = 1e-5
RMS_EPS = 1e-6

LANES = 128
MASKED = -1e30
INT_MIN = -2 ** 31
VMEM_LIMIT = 56 * 1024 * 1024

_C_CQ = 0
_C_CKV = 512
_C_QD = 1024
_C_KD = 2048
_C_VD = 2304
_C_QI = 2560
_C_KR = 3584
_C_KILO = 3712
_C_KIHI = 3840
_C_WI = 3968
_IN_COLS_P = 4096

F32 = jnp.float32
BF16 = jnp.bfloat16


def _dot(a, b):
    return jnp.dot(a, b, preferred_element_type=F32)


def _dot_nt(a, b):
    return lax.dot_general(a, b, (((1,), (1,)), ((), ())), preferred_element_type=F32)


def _params(n_axes, vmem=VMEM_LIMIT):
    return pltpu.CompilerParams(dimension_semantics=("arbitrary",) * n_axes, vmem_limit_bytes=vmem)


def _round_up(n, m):
    return -(-n // m) * m


def _resident(shape, index_map):
    return pl.BlockSpec(shape, index_map, pipeline_mode=pl.Buffered(1))


def _rope128(x, cos, sin_signed):
    return x * cos + pltpu.roll(x, 64, 1) * sin_signed


def _rope64(x, cos, sin_signed, first_half):
    rot = jnp.where(first_half, pltpu.roll(x, 96, 1), pltpu.roll(x, 32, 1))
    return x * cos + rot * sin_signed


def _rope_tables(pos):
    pos = pos.astype(F32)[:, None]
    lane = np.arange(LANES)

    def tables(dim):
        half = dim // 2
        inv = 1.0 / (ROPE_THETA ** (jnp.arange(half, dtype=F32) / half))
        ang = pos * inv[None, :]
        cos, sin = jnp.cos(ang), jnp.sin(ang)
        idx = lane % half
        sign = np.where((lane % dim) < half, -1.0, 1.0).astype(np.float32)
        return cos[:, idx], sin[:, idx] * sign[None, :]

    c128, s128 = tables(128)
    c64, s64 = tables(64)
    return c128, s128, c64, s64


def _proj_kernel(x_ref, w_ref, wuq_ref, gq_ref, gkv_ref, c128_ref, s128_ref, c64_ref, s64_ref,
                 ckv_ref, kpe_ref, kd_ref, vd_ref, ki_ref,
                 q_ref, qd_ref, qi_ref, wi_ref, kpeb_ref, kdb_ref, vdb_ref, kilo_ref, kihi_ref):
    xb = x_ref[...].astype(BF16)
    c128, s128 = c128_ref[...], s128_ref[...]
    c64, s64 = c64_ref[...], s64_ref[...]
    first_half = (lax.broadcasted_iota(jnp.int32, c64.shape, 1) % 64) < 32

    def seg(a, b):
        return _dot(xb, w_ref[:, a:b])

    def rms(v, g):
        return v * lax.rsqrt(jnp.mean(v * v, axis=-1, keepdims=True) + RMS_EPS) * g

    qn = rms(seg(_C_CQ, _C_CQ + MLA_Q_LORA), gq_ref[...]).astype(BF16)
    for h in range(MLA_HEADS):
        qh = _dot(qn, wuq_ref[:, h * 256:(h + 1) * 256])
        q_ref[h, :, 0:128] = qh[:, 0:128].astype(BF16)
        q_ref[h, :, 128:256] = _rope64(qh[:, 128:256], c64, s64, first_half).astype(BF16)

    ckv_ref[...] = rms(seg(_C_CKV, _C_CKV + MLA_KV_LORA), gkv_ref[...])

    def slab_pair(a):
        y = seg(a, a + 256)
        return y[:, 0:128], y[:, 128:256]

    for hp in range(DSA_HEADS // 2):
        for j, slab in enumerate(slab_pair(_C_QD + hp * 256)):
            qd_ref[2 * hp + j] = _rope128(slab, c128, s128).astype(BF16)
    for c, slab in enumerate(slab_pair(_C_KD)):
        kdc = _rope128(slab, c128, s128)
        kd_ref[:, c * 128:(c + 1) * 128] = kdc
        kdb_ref[:, c * 128:(c + 1) * 128] = kdc.astype(BF16)
    vd = seg(_C_VD, _C_VD + 256)
    vd_ref[...] = vd
    vdb_ref[...] = vd.astype(BF16)

    for hq in range(IDX_HEADS // 4):
        for j, slab in enumerate(slab_pair(_C_QI + hq * 256)):
            qi_ref[2 * hq + j] = _rope64(slab, c64, s64, first_half).astype(BF16)

    kr, kilo = slab_pair(_C_KR)
    kr = _rope64(kr, c64, s64, first_half)
    kpe_ref[...] = kr[:, 0:MLA_ROPE]
    kpeb_ref[...] = kr.astype(BF16)
    kilo = _rope64(kilo, c64, s64, first_half)
    ki_ref[...] = kilo[:, 0:IDX_DIM]
    kilo_ref[...] = kilo.astype(BF16)
    kihi, wi = slab_pair(_C_KIHI)
    kihi_ref[...] = _rope64(kihi, c64, s64, first_half).astype(BF16)
    wi_ref[...] = wi * (IDX_DIM ** -0.5 * IDX_HEADS ** -0.5)


def _proj(x2d, pos, wp, tm):
    n, d = x2d.shape
    c128, s128, c64, s64 = _rope_tables(pos)
    row = lambda w: pl.BlockSpec((tm, w), lambda i: (i, 0))
    heads = lambda nh, w: pl.BlockSpec((nh, tm, w), lambda i: (0, i, 0))
    out_shapes = (
        jax.ShapeDtypeStruct((n, MLA_KV_LORA), F32),
        jax.ShapeDtypeStruct((n, MLA_ROPE), F32),
        jax.ShapeDtypeStruct((n, 256), F32),
        jax.ShapeDtypeStruct((n, 256), F32),
        jax.ShapeDtypeStruct((n, IDX_DIM), F32),
        jax.ShapeDtypeStruct((MLA_HEADS, n, 256), BF16),
        jax.ShapeDtypeStruct((DSA_HEADS, n, 128), BF16),
        jax.ShapeDtypeStruct((IDX_HEADS // 2, n, 128), BF16),
        jax.ShapeDtypeStruct((n, 128), F32),
        jax.ShapeDtypeStruct((n, 128), BF16),
        jax.ShapeDtypeStruct((n, 256), BF16),
        jax.ShapeDtypeStruct((n, 256), BF16),
        jax.ShapeDtypeStruct((n, 128), BF16),
        jax.ShapeDtypeStruct((n, 128), BF16),
    )
    out_specs = (row(MLA_KV_LORA), row(MLA_ROPE), row(256), row(256), row(IDX_DIM),
                 heads(MLA_HEADS, 256), heads(DSA_HEADS, 128), heads(IDX_HEADS // 2, 128),
                 row(128), row(128), row(256), row(256), row(128), row(128))
    in_specs = [row(d),
                _resident((d, _IN_COLS_P), lambda i: (0, 0)),
                _resident((MLA_Q_LORA, MLA_HEADS * 256), lambda i: (0, 0)),
                _resident((1, MLA_Q_LORA), lambda i: (0, 0)),
                _resident((1, MLA_KV_LORA), lambda i: (0, 0)),
                row(128), row(128), row(128), row(128)]
    return pl.pallas_call(
        _proj_kernel, grid=(n // tm,), in_specs=in_specs, out_specs=out_specs, out_shape=out_shapes,
        compiler_params=_params(1), name="proj",
    )(x2d, wp["w_in"], wp["w_uq"], wp["g_q"], wp["g_kv"], c128, s128, c64, s64)


def _kvup_kernel(ckv_ref, kpeb_ref, wk_ref, wv_ref, k_ref, v_ref):
    cb = ckv_ref[...].astype(BF16)
    kpe = kpeb_ref[...]
    for hp in range(MLA_HEADS // 2):
        k2 = _dot(cb, wk_ref[:, hp * 256:(hp + 1) * 256]).astype(BF16)
        v2 = _dot(cb, wv_ref[:, hp * 256:(hp + 1) * 256]).astype(BF16)
        for j in range(2):
            h = 2 * hp + j
            k_ref[h, :, 0:128] = k2[:, j * 128:(j + 1) * 128]
            k_ref[h, :, 128:256] = kpe
            v_ref[h] = v2[:, j * 128:(j + 1) * 128]


def _kv_up(ckv2d, kpeb2d, wp, tm):
    n = ckv2d.shape[0]
    return pl.pallas_call(
        _kvup_kernel, grid=(n // tm,),
        in_specs=[pl.BlockSpec((tm, MLA_KV_LORA), lambda i: (i, 0)),
                  pl.BlockSpec((tm, 128), lambda i: (i, 0)),
                  _resident((MLA_KV_LORA, MLA_HEADS * MLA_NOPE), lambda i: (0, 0)),
                  _resident((MLA_KV_LORA, MLA_HEADS * MLA_V), lambda i: (0, 0))],
        out_specs=(pl.BlockSpec((MLA_HEADS, tm, 256), lambda i: (0, i, 0)),
                   pl.BlockSpec((MLA_HEADS, tm, 128), lambda i: (0, i, 0))),
        out_shape=(jax.ShapeDtypeStruct((MLA_HEADS, n, 256), BF16),
                   jax.ShapeDtypeStruct((MLA_HEADS, n, 128), BF16)),
        compiler_params=_params(1), name="kv_up",
    )(ckv2d, kpeb2d, wp["w_uk"], wp["w_uv"])


LOG2E = 1.4426950408889634


def _row_max(s):
    return jnp.broadcast_to(jnp.max(s, axis=1, keepdims=True), (s.shape[0], LANES))


KCH = 256


def _softmax_probs(s, smax, m_scr, l_scr, scale):
    coef = scale * LOG2E
    m_prev = m_scr[...]
    m_next = m_prev
    for sm in smax:
        m_next = jnp.maximum(m_next, sm)
    m_wide = jnp.concatenate([m_next] * (KCH // LANES), axis=1)
    alpha = jnp.exp2((m_prev - m_next) * coef)
    l_sum, probs = None, []
    for s_ch in s:
        p = jnp.exp2((s_ch - m_wide) * coef)
        p_sum = jnp.sum(p, axis=1, keepdims=True)
        l_sum = p_sum if l_sum is None else l_sum + p_sum
        probs.append(p.astype(BF16))
    l_scr[...] = alpha * l_scr[...] + l_sum
    m_scr[...] = m_next
    return alpha, probs


def _accumulate_pv(alpha, probs, v, acc_scr):
    pv = None
    for p_ch, v_ch in zip(probs, v):
        pv_ch = _dot(p_ch, v_ch)
        pv = pv_ch if pv is None else pv + pv_ch
    acc_scr[...] = acc_scr[...] * alpha + pv


def _chunk_mask(q0, kb0, tq, tk):
    qch = (q0 + lax.broadcasted_iota(jnp.int32, (tq, 1), 0)) >> CHUNK_SHIFT
    kch = (kb0 + lax.broadcasted_iota(jnp.int32, (1, tk), 1)) >> CHUNK_SHIFT
    return kch <= qch


def _three_stage_key_steps(q0, w, qk, sm, pv):
    n = q0 // w
    at = lambda i: pl.multiple_of(i * w, w)

    @pl.when(n == 0)
    def _():
        qk(0, 0)
        sm(0, 0, True)
        pv(0, 0)

    @pl.when(n >= 1)
    def _():
        qk(0, 0)
        qk(w, 1)
        sm(0, 0, False)
        pairs = (n - 1) // 2

        def body(t, carry):
            i = 2 * t
            qk(at(i + 2), 0)
            sm(1, at(i + 1), False)
            pv(0, at(i))
            qk(at(i + 3), 1)
            sm(0, at(i + 2), False)
            pv(1, at(i + 1))
            return carry

        lax.fori_loop(0, pairs, body, 0)
        i = 2 * pairs

        @pl.when(n - i == 1)
        def _():
            sm(1, at(i + 1), True)
            pv(0, at(i))
            pv(1, at(i + 1))

        @pl.when(n - i == 2)
        def _():
            qk(at(i + 2), 0)
            sm(1, at(i + 1), False)
            pv(0, at(i))
            sm(0, at(i + 2), True)
            pv(1, at(i + 1))
            pv(0, at(i + 2))

    return (n + 1) * w


def _direct_key_steps(q0, w, fn, wide=2):
    n = q0 // w
    ww = wide * w
    lax.fori_loop(0, n // wide, lambda i, c: (fn(pl.multiple_of(i * ww, ww), ww, False), c)[1], 0)
    lax.fori_loop((n // wide) * wide, n, lambda i, c: (fn(pl.multiple_of(i * w, w), w, False), c)[1], 0)
    fn(pl.multiple_of(n * w, w), w, True)
    return (n + 1) * w


def _mla_kernel(q_ref, k_ref, v_ref, o_ref, s_scr, smax_scr, p_scr, alpha_scr, m_scr, l_scr, acc_scr,
                *, tq, w, q_pos0, scale):
    q0 = q_pos0 + pl.program_id(2) * tq
    m_scr[...] = jnp.full(m_scr.shape, MASKED, F32)
    l_scr[...] = jnp.zeros(l_scr.shape, F32)
    acc_scr[...] = jnp.zeros(acc_scr.shape, F32)
    q = q_ref[0, 0]

    nch = w // KCH

    def qk(k0, buf):
        for ch in range(nch):
            s = _dot_nt(q, k_ref[0, 0, pl.ds(k0 + ch * KCH, KCH), :])
            s_scr[buf, ch] = s
            smax_scr[buf, ch] = _row_max(s)

    def sm(buf, k0, masked):
        s = [s_scr[buf, ch] for ch in range(nch)]
        if masked:
            s = [jnp.where(_chunk_mask(q0, k0 + ch * KCH, tq, KCH), s[ch], MASKED) for ch in range(nch)]
            smax = [_row_max(s_ch) for s_ch in s]
        else:
            smax = [smax_scr[buf, ch] for ch in range(nch)]
        alpha, probs = _softmax_probs(s, smax, m_scr, l_scr, scale)
        alpha_scr[buf] = alpha
        for ch in range(nch):
            p_scr[buf, ch] = probs[ch]

    def pv(buf, k0):
        _accumulate_pv(alpha_scr[buf], [p_scr[buf, ch] for ch in range(nch)],
                       [v_ref[0, 0, pl.ds(k0 + ch * KCH, KCH), :] for ch in range(nch)], acc_scr)

    _three_stage_key_steps(q0, w, qk, sm, pv)
    o_ref[0] = (acc_scr[...] / l_scr[...]).astype(o_ref.dtype)


def _mla_attention(q, k, v, q_pos0, tq, w):
    nh, b, t, _ = q.shape
    s = k.shape[2]
    assert w >= tq and s % w == 0 and q_pos0 % CHUNK == 0 and tq % CHUNK == 0
    kern = functools.partial(_mla_kernel, tq=tq, w=w, q_pos0=q_pos0,
                             scale=(MLA_NOPE + MLA_ROPE) ** -0.5)
    return pl.pallas_call(
        kern, grid=(b, nh, t // tq),
        in_specs=[pl.BlockSpec((1, 1, tq, 256), lambda bi, h, i: (h, bi, i, 0)),
                  pl.BlockSpec((1, 1, s, 256), lambda bi, h, i: (h, bi, 0, 0)),
                  pl.BlockSpec((1, 1, s, 128), lambda bi, h, i: (h, bi, 0, 0))],
        out_specs=pl.BlockSpec((1, tq, 128), lambda bi, h, i: (bi, i, h)),
        out_shape=jax.ShapeDtypeStruct((b, t, nh * MLA_V), BF16),
        scratch_shapes=[pltpu.VMEM((2, w // KCH, tq, KCH), F32), pltpu.VMEM((2, w // KCH, tq, LANES), F32),
                        pltpu.VMEM((2, w // KCH, tq, KCH), BF16), pltpu.VMEM((2, tq, LANES), F32),
                        pltpu.VMEM((tq, LANES), F32), pltpu.VMEM((tq, LANES), F32),
                        pltpu.VMEM((tq, MLA_V), F32)],
        compiler_params=_params(3), name="mla_attn",
    )(q, k, v)


def _dsa_kernel(qd_ref, qi_ref, wi_ref, kd_ref, vd_ref, kilo_ref, kihi_ref, o_ref,
                key_scr, top_scr, c_scr, hi_scr, cnt_scr, cand_scr, m_scr, l_scr, acc_scr,
                *, tq, w, wide_c, q_pos0, s_real, topk, idx_bits, scale):
    q0 = q_pos0 + pl.program_id(1) * tq
    sub = w // LANES

    qi_all = qi_ref[:, 0].reshape(IDX_HEADS // 2 * tq, LANES)
    wi = wi_ref[0]

    def to_key(f):
        bits = pltpu.bitcast(f, jnp.int32)
        return bits ^ ((bits >> 31) & 0x7FFFFFFF)

    def score_step(k0, width, masked):
        lo = _dot_nt(qi_all, kilo_ref[0, pl.ds(k0, width), :])
        hi = _dot_nt(qi_all, kihi_ref[0, pl.ds(k0, width), :])
        score = jnp.zeros((tq, width), F32)
        for hp in range(IDX_HEADS // 2):
            rows = slice(hp * tq, (hp + 1) * tq)
            score = score + wi[:, 2 * hp:2 * hp + 1] * jnp.maximum(lo[rows], 0.0)
            score = score + wi[:, 2 * hp + 1:2 * hp + 2] * jnp.maximum(hi[rows], 0.0)
        key = to_key(score)
        if masked:
            visible_here = _chunk_mask(q0, k0, tq, width)
            key = jnp.where(visible_here, key, INT_MIN)
            score = jnp.where(visible_here, score, -jnp.inf)
        key_scr[:, pl.ds(k0, width)] = key
        top1, top2 = top_scr[0], top_scr[1]
        for u in range(width // LANES):
            x = score[:, u * LANES:(u + 1) * LANES]
            top2 = jnp.maximum(top2, jnp.minimum(top1, x))
            top1 = jnp.maximum(top1, x)
        top_scr[0] = top1
        top_scr[1] = top2

    top_scr[...] = jnp.full(top_scr.shape, -jnp.inf, F32)
    k_end = _direct_key_steps(q0, w, score_step)
    n_sb = k_end // w

    qpos = q0 + lax.broadcasted_iota(jnp.int32, (tq, LANES), 0)
    visible = jnp.minimum(((qpos >> CHUNK_SHIFT) + 1) * CHUNK, s_real)
    k_target = jnp.minimum(visible, topk).astype(F32)

    def count(mode):
        cand = cand_scr[...]
        cval = c_scr[...]

        def span(k0, nblk, acc):
            for u in range(nblk):
                blk = key_scr[:, pl.ds(k0 + u * LANES, LANES)]
                if mode == "ge":
                    hit = blk >= cand
                else:
                    idx = k0 + u * LANES + lax.broadcasted_iota(jnp.int32, (tq, LANES), 1)
                    hit = jnp.where(blk == cval, idx, jnp.int32(2 ** 30)) < cand
                acc = acc + jnp.where(hit, 1.0, 0.0)
            return acc

        acc = lax.fori_loop(0, n_sb // 4,
                            lambda i, a: span(pl.multiple_of(i * 4 * w, 4 * w), 4 * sub, a),
                            jnp.zeros((tq, LANES), F32))
        acc = lax.fori_loop(4 * (n_sb // 4), n_sb,
                            lambda i, a: span(pl.multiple_of(i * w, w), sub, a), acc)
        cnt_scr[...] = jnp.broadcast_to(jnp.sum(acc, axis=1, keepdims=True), (tq, LANES))

    lane_min = lambda a: jnp.broadcast_to(jnp.min(a, axis=1, keepdims=True), (tq, LANES))
    lane_max = lambda a: jnp.broadcast_to(jnp.max(a, axis=1, keepdims=True), (tq, LANES))
    lo_f = lane_min(top_scr[1])
    hi_f = jnp.where(k_target > float(LANES), lane_max(top_scr[1]), lane_max(top_scr[0]))
    c_lo = jnp.where(lo_f == -jnp.inf, INT_MIN, jnp.where(lo_f == 0.0, -1, to_key(lo_f)))
    c_hi = jnp.where(hi_f == 0.0, 0, to_key(hi_f))

    c_scr[...] = c_lo
    hi_scr[...] = c_hi + 1

    def midpoint(lo, hi):
        return (lo >> 1) + (hi >> 1) + (lo & hi & 1)

    def bisect(carry):
        it, _, cnt_lo = carry
        lo, hi = c_scr[...], hi_scr[...]
        mid = midpoint(lo, hi)
        cand = jnp.where(cnt_lo == k_target, lo, mid)
        cand_scr[...] = cand
        count("ge")
        cnt = cnt_scr[...]
        take = cnt >= k_target
        lo, hi = jnp.where(take, cand, lo), jnp.where(take, hi, cand)
        cnt_lo = jnp.where(take, cnt, cnt_lo)
        c_scr[...] = lo
        hi_scr[...] = hi
        open_rows = jnp.logical_and(cnt_lo != k_target, midpoint(lo, hi) != lo)
        return it + 1, jnp.max(jnp.where(open_rows, 1.0, 0.0)) > 0.0, cnt_lo

    _, _, cnt_lo = lax.while_loop(lambda carry: jnp.logical_and(carry[1], carry[0] < 40), bisect,
                                  (jnp.int32(0), jnp.bool_(True), jnp.full((tq, LANES), -1.0, F32)))
    ties = jnp.max(jnp.abs(cnt_lo - k_target)) > 0.0

    @pl.when(ties)
    def _():
        cand_scr[...] = c_scr[...] + 1
        count("ge")
        need = k_target - cnt_scr[...]
        x = jnp.zeros((tq, LANES), jnp.int32)
        for bit in range(idx_bits - 1, -1, -1):
            cand_scr[...] = x + (1 << bit)
            count("eq_lt")
            x = jnp.where(cnt_scr[...] < need, x + (1 << bit), x)
        cand_scr[...] = x

        def demote(kb, carry):
            kb0 = pl.multiple_of(kb * 256, 256)
            cval = jnp.concatenate([c_scr[...]] * 2, axis=1)
            last = jnp.concatenate([cand_scr[...]] * 2, axis=1)
            idx = kb0 + lax.broadcasted_iota(jnp.int32, (tq, 256), 1)
            blk = key_scr[:, pl.ds(kb0, 256)]
            drop = jnp.where(blk == cval, idx, jnp.int32(-1)) > last
            key_scr[:, pl.ds(kb0, 256)] = jnp.where(drop, cval - 1, blk)
            return carry

        lax.fori_loop(0, k_end // 256, demote, 0)

    m_scr[...] = jnp.full(m_scr.shape, MASKED, F32)
    l_scr[...] = jnp.zeros(l_scr.shape, F32)
    acc_scr[...] = jnp.zeros(acc_scr.shape, F32)

    rows_c = DSA_GROUP * tq

    def attend(k0, width, masked):
        del masked
        chunks = range(width // KCH)
        cval = jnp.concatenate([c_scr[...]] * (KCH // LANES), axis=1)
        bias = [jnp.where(key_scr[:, pl.ds(k0 + ch * KCH, KCH)] >= cval, 0.0, MASKED) for ch in chunks]
        bias = [jnp.concatenate([b] * DSA_GROUP, axis=0) for b in bias]
        for c in range(DSA_KV_HEADS):
            qg = qd_ref[c * DSA_GROUP:(c + 1) * DSA_GROUP, 0].reshape(rows_c, DSA_HEAD_DIM)
            cols = slice(c * DSA_HEAD_DIM, (c + 1) * DSA_HEAD_DIM)
            s = [_dot_nt(qg, kd_ref[0, pl.ds(k0 + ch * KCH, KCH), cols]) + bias[ch] for ch in chunks]
            alpha, probs = _softmax_probs(s, [_row_max(s_ch) for s_ch in s],
                                          m_scr.at[c], l_scr.at[c], scale)
            _accumulate_pv(alpha, probs, [vd_ref[0, pl.ds(k0 + ch * KCH, KCH), cols] for ch in chunks],
                           acc_scr.at[c])

    _direct_key_steps(q0, w, attend, wide=wide_c)
    for c in range(DSA_KV_HEADS):
        o = acc_scr[c] / l_scr[c]
        for g in range(DSA_GROUP):
            h = c * DSA_GROUP + g
            o_ref[0, :, h * DSA_HEAD_DIM:(h + 1) * DSA_HEAD_DIM] = o[g * tq:(g + 1) * tq].astype(o_ref.dtype)


def _dsa_attention(qd, qi, wi, kd, vd, kilo, kihi, q_pos0, s_real, tq, w, wide_c):
    _, b, t, _ = qd.shape
    s = kd.shape[1]
    assert w >= tq and s % w == 0 and q_pos0 % CHUNK == 0 and tq % CHUNK == 0
    kern = functools.partial(
        _dsa_kernel, tq=tq, w=w, wide_c=wide_c, q_pos0=q_pos0, s_real=s_real,
        topk=min(IDX_TOPK, s_real // 4), idx_bits=int(s).bit_length(), scale=DSA_HEAD_DIM ** -0.5)
    heads = lambda: pl.BlockSpec((8, 1, tq, 128), lambda bi, i: (0, bi, i, 0))
    keys = lambda width: _resident((1, s, width), lambda bi, i: (bi, 0, 0))
    return pl.pallas_call(
        kern, grid=(b, t // tq),
        in_specs=[heads(), heads(), pl.BlockSpec((1, tq, 128), lambda bi, i: (bi, i, 0)),
                  keys(256), keys(256), keys(128), keys(128)],
        out_specs=pl.BlockSpec((1, tq, DSA_HEADS * DSA_HEAD_DIM), lambda bi, i: (bi, i, 0)),
        out_shape=jax.ShapeDtypeStruct((b, t, DSA_HEADS * DSA_HEAD_DIM), BF16),
        scratch_shapes=[pltpu.VMEM((tq, s), jnp.int32), pltpu.VMEM((2, tq, LANES), F32),
                        pltpu.VMEM((tq, LANES), jnp.int32), pltpu.VMEM((tq, LANES), jnp.int32),
                        pltpu.VMEM((tq, LANES), F32),
                        pltpu.VMEM((tq, LANES), jnp.int32),
                        pltpu.VMEM((DSA_KV_HEADS, DSA_GROUP * tq, LANES), F32),
                        pltpu.VMEM((DSA_KV_HEADS, DSA_GROUP * tq, LANES), F32),
                        pltpu.VMEM((DSA_KV_HEADS, DSA_GROUP * tq, DSA_HEAD_DIM), F32)],
        compiler_params=_params(2), name="dsa_attn",
    )(qd, qi, wi, kd, vd, kilo, kihi)


def _layer_norm(v, g, b):
    mu = jnp.mean(v, axis=-1, keepdims=True)
    d = v - mu
    var = jnp.mean(d * d, axis=-1, keepdims=True)
    return d * lax.rsqrt(var + LN_EPS) * g + b


def _outln_kernel(mla_ref, dsa_ref, x_ref, wo_ref, g_ref, b_ref, o_ref, *, alpha, half):
    a = _dot(mla_ref[...], wo_ref[0:half, :]) + _dot(dsa_ref[...], wo_ref[half:, :])
    o_ref[...] = _layer_norm(alpha * x_ref[...] + a, g_ref[...], b_ref[...])


def _out_ln(mla_o, dsa_o, x2d, wp, alpha, tm):
    n, d = x2d.shape
    half = mla_o.shape[1]
    row = lambda w: pl.BlockSpec((tm, w), lambda i: (i, 0))
    return pl.pallas_call(
        functools.partial(_outln_kernel, alpha=alpha, half=half), grid=(n // tm,),
        in_specs=[row(half), row(dsa_o.shape[1]), row(d),
                  _resident(wp["w_o"].shape, lambda i: (0, 0)),
                  _resident((1, d), lambda i: (0, 0)), _resident((1, d), lambda i: (0, 0))],
        out_specs=row(d), out_shape=jax.ShapeDtypeStruct((n, d), F32),
        compiler_params=_params(1), name="out_ln",
    )(mla_o, dsa_o, x2d, wp["w_o"], wp["ln1_g"], wp["ln1_b"])


def _ffn_kernel(x_ref, wg_ref, wu_ref, wd_ref, g_ref, b_ref, o_ref, xb_scr, acc_scr, *, alpha):
    j = pl.program_id(1)

    @pl.when(j == 0)
    def _():
        xb_scr[...] = x_ref[...].astype(BF16)
        acc_scr[...] = jnp.zeros(acc_scr.shape, F32)

    xb = xb_scr[...]
    gate = _dot(xb, wg_ref[...])
    up = _dot(xb, wu_ref[...])
    hidden = gate * (1.0 / (1.0 + jnp.exp(-gate))) * up
    acc_scr[...] += _dot(hidden.astype(BF16), wd_ref[...])

    @pl.when(j == pl.num_programs(1) - 1)
    def _():
        o_ref[...] = _layer_norm(alpha * x_ref[...] + acc_scr[...], g_ref[...], b_ref[...])


def _ffn_ln(x2d, wp, alpha, tm, tf):
    n, d = x2d.shape
    dff = wp["w_gate"].shape[1]
    return pl.pallas_call(
        functools.partial(_ffn_kernel, alpha=alpha), grid=(n // tm, dff // tf),
        in_specs=[pl.BlockSpec((tm, d), lambda i, j: (i, 0)),
                  pl.BlockSpec((d, tf), lambda i, j: (0, j)),
                  pl.BlockSpec((d, tf), lambda i, j: (0, j)),
                  pl.BlockSpec((tf, d), lambda i, j: (j, 0)),
                  _resident((1, d), lambda i, j: (0, 0)), _resident((1, d), lambda i, j: (0, 0))],
        out_specs=pl.BlockSpec((tm, d), lambda i, j: (i, 0)),
        out_shape=jax.ShapeDtypeStruct((n, d), F32),
        scratch_shapes=[pltpu.VMEM((tm, d), BF16), pltpu.VMEM((tm, d), F32)],
        compiler_params=_params(2), name="ffn_ln",
    )(x2d, wp["w_gate"], wp["w_up"], wp["w_down"], wp["ln2_g"], wp["ln2_b"])


def _pack_weights(w_in, w_uq, g_q, w_ukv, g_kv, w_o, ln1_g, ln1_b, w_gate, w_up, w_down, ln2_g, ln2_b):
    d = w_in.shape[0]
    splits = (MLA_Q_LORA, MLA_KV_LORA, MLA_ROPE, DSA_HEADS * DSA_HEAD_DIM, DSA_KV_HEADS * DSA_HEAD_DIM,
              DSA_KV_HEADS * DSA_HEAD_DIM, IDX_HEADS * IDX_DIM, IDX_DIM, IDX_HEADS)
    offs = np.cumsum(splits)[:-1].tolist()
    c_q, c_kv, k_r, q_d, k_d, v_d, q_i, k_i, w_i = jnp.split(w_in, offs, axis=1)
    z = lambda n: jnp.zeros((d, n), w_in.dtype)
    w_in_p = jnp.concatenate(
        [c_q, c_kv, q_d, k_d, v_d, q_i, k_r, z(64), k_i, z(64), z(64), k_i, w_i, z(128 - IDX_HEADS)], axis=1)
    assert w_in_p.shape[1] == _IN_COLS_P
    w_uq_p = jnp.pad(w_uq, ((0, 0), (0, 0), (0, 256 - MLA_NOPE - MLA_ROPE)))
    return {
        "w_in": w_in_p.astype(BF16),
        "w_uq": w_uq_p.reshape(MLA_Q_LORA, MLA_HEADS * 256).astype(BF16),
        "g_q": g_q.reshape(1, -1), "g_kv": g_kv.reshape(1, -1),
        "w_uk": w_ukv[:, :, :MLA_NOPE].reshape(MLA_KV_LORA, MLA_HEADS * MLA_NOPE).astype(BF16),
        "w_uv": w_ukv[:, :, MLA_NOPE:].reshape(MLA_KV_LORA, MLA_HEADS * MLA_V).astype(BF16),
        "w_o": w_o.astype(BF16),
        "ln1_g": ln1_g.reshape(1, -1), "ln1_b": ln1_b.reshape(1, -1),
        "w_gate": w_gate.astype(BF16), "w_up": w_up.astype(BF16), "w_down": w_down.astype(BF16),
        "ln2_g": ln2_g.reshape(1, -1), "ln2_b": ln2_b.reshape(1, -1),
    }


def _trunk_layer(x, q_pos0, past, wp, alpha, cfg):
    b, t, d = x.shape
    n = b * t
    x2d = x.reshape(n, d)
    pos = jnp.tile(q_pos0 + jnp.arange(t, dtype=jnp.int32), b)
    (ckv, kpe, kd, vd, ki, q_mla, qd, qi, wi, kpeb, kdb, vdb, kilo, kihi) = _proj(x2d, pos, wp, cfg["tm_proj"])
    new_rows = (ckv.reshape(b, t, -1), kpe.reshape(b, t, -1),
                kd.reshape(b, t, DSA_KV_HEADS, DSA_HEAD_DIM), vd.reshape(b, t, DSA_KV_HEADS, DSA_HEAD_DIM),
                ki.reshape(b, t, -1))

    per_b = lambda a: a.reshape(b, t, a.shape[-1])
    ckv_all, kpeb_all, kdb_all, vdb_all, kilo_all, kihi_all = map(per_b, (ckv, kpeb, kdb, vdb, kilo, kihi))
    s_real = t
    if past is not None:
        p_ckv, p_kpe, p_kd, p_vd, p_ki = past
        s_real = p_ckv.shape[1] + t
        z64 = jnp.zeros(p_kpe.shape, BF16)
        cat = lambda c, nw: jnp.concatenate([c, nw], axis=1)
        ckv_all = cat(p_ckv, ckv_all)
        kpeb_all = cat(jnp.concatenate([p_kpe.astype(BF16), z64], axis=-1), kpeb_all)
        kdb_all = cat(p_kd.reshape(b, -1, 256).astype(BF16), kdb_all)
        vdb_all = cat(p_vd.reshape(b, -1, 256).astype(BF16), vdb_all)
        kilo_all = cat(jnp.concatenate([p_ki.astype(BF16), z64], axis=-1), kilo_all)
        kihi_all = cat(jnp.concatenate([z64, p_ki.astype(BF16)], axis=-1), kihi_all)
    s_pad = _round_up(s_real, max(cfg["w_mla"], cfg["w_dsa"]))
    if s_pad != s_real:
        padk = lambda a: jnp.pad(a, ((0, 0), (0, s_pad - s_real), (0, 0)))
        ckv_all, kpeb_all, kdb_all, vdb_all, kilo_all, kihi_all = map(
            padk, (ckv_all, kpeb_all, kdb_all, vdb_all, kilo_all, kihi_all))

    k_mla, v_mla = _kv_up(ckv_all.reshape(b * s_pad, -1), kpeb_all.reshape(b * s_pad, -1), wp, cfg["tm_kv"])
    k_mla = k_mla.reshape(MLA_HEADS, b, s_pad, 256)
    v_mla = v_mla.reshape(MLA_HEADS, b, s_pad, 128)
    mla_o = _mla_attention(q_mla.reshape(MLA_HEADS, b, t, 256), k_mla, v_mla, q_pos0,
                           cfg["tq_mla"], cfg["w_mla"])
    dsa_o = _dsa_attention(qd.reshape(DSA_HEADS, b, t, 128), qi.reshape(IDX_HEADS // 2, b, t, 128),
                           wi.reshape(b, t, 128), kdb_all, vdb_all, kilo_all, kihi_all,
                           q_pos0, s_real, cfg["tq_dsa"], cfg["w_dsa"], cfg["wide_dsa"])
    x1 = _out_ln(mla_o.reshape(n, -1), dsa_o.reshape(n, -1), x2d, wp, alpha, cfg["tm_out"])
    y = _ffn_ln(x1, wp, alpha, cfg["tm_ffn"], cfg["tf_ffn"])
    return y.reshape(b, t, d), new_rows


_PROMPT_CFG = dict(tm_proj=512, tm_kv=512, tq_mla=512, tq_dsa=128, w_mla=512, w_dsa=512, wide_dsa=4,
                   tm_out=512, tm_ffn=512, tf_ffn=512)
_SAMPLE_CFG = dict(tm_proj=256, tm_kv=256, tq_mla=64, tq_dsa=64, w_mla=256, w_dsa=256, wide_dsa=2,
                   tm_out=256, tm_ffn=512, tf_ffn=512)


def kernel(x_prompt, x_sample, cache_mla_ckv, cache_mla_kpe, cache_dsa_k, cache_dsa_v, cache_idx_k, w_in, w_uq, mla_q_norm_g, w_ukv, mla_kv_norm_g, w_o, ln1_g, ln1_b, w_gate, w_up, w_down, ln2_g, ln2_b):
    depth = w_in.shape[0]
    alpha = (2 * depth) ** 0.25
    past_len = cache_mla_ckv.shape[2]
    y_p, y_s = x_prompt, x_sample
    rows_p, rows_s = [], []
    for l in range(depth):
        wp = _pack_weights(w_in[l], w_uq[l], mla_q_norm_g[l], w_ukv[l], mla_kv_norm_g[l], w_o[l],
                           ln1_g[l], ln1_b[l], w_gate[l], w_up[l], w_down[l], ln2_g[l], ln2_b[l])
        y_p, r_p = _trunk_layer(y_p, 0, None, wp, alpha, _PROMPT_CFG)
        past = (cache_mla_ckv[l], cache_mla_kpe[l], cache_dsa_k[l], cache_dsa_v[l], cache_idx_k[l])
        y_s, r_s = _trunk_layer(y_s, past_len, past, wp, alpha, _SAMPLE_CFG)
        rows_p.append(r_p)
        rows_s.append(r_s)
    stack = lambda rows, i: jnp.stack([r[i] for r in rows], axis=0)
    return (y_p, y_s,
            stack(rows_p, 0), stack(rows_p, 1), stack(rows_p, 2), stack(rows_p, 3), stack(rows_p, 4),
            stack(rows_s, 0), stack(rows_s, 1), stack(rows_s, 2), stack(rows_s, 3), stack(rows_s, 4))
```

```python
import functools

import numpy as np
import jax
import jax.numpy as jnp
from jax import lax
from jax.experimental import pallas as pl
from jax.experimental.pallas import tpu as pltpu

CHUNK = 64
CHUNK_SHIFT = 6
ROPE_THETA = 10000.0
MLA_HEADS = 8
MLA_Q_LORA = 512
MLA_KV_LORA = 512
MLA_NOPE = 128
MLA_ROPE = 64
MLA_V = 128
DSA_HEADS = 8
DSA_KV_HEADS = 2
DSA_GROUP = DSA_HEADS // DSA_KV_HEADS
DSA_HEAD_DIM = 128
IDX_HEADS = 16
IDX_DIM = 64
IDX_TOPK = 256
LN_EPS = 1e-5
RMS_EPS = 1e-6

LANES = 128
KCH = 256
MASKED = -1e30
INT_MIN = -2 ** 31
LOG2E = 1.4426950408889634
VMEM_LIMIT = 56 * 1024 * 1024

_C_CQ = 0
_C_CKV = 512
_C_QD = 1024
_C_KD = 2048
_C_VD = 2304
_C_QI = 2560
_C_KR = 3584
_C_KIHI = 3840
_IN_COLS_P = 4096

F32 = jnp.float32
BF16 = jnp.bfloat16


def _dot(a, b):
    return jnp.dot(a, b, preferred_element_type=F32)


def _dot_nt(a, b):
    return lax.dot_general(a, b, (((1,), (1,)), ((), ())), preferred_element_type=F32)


def _params(n_axes, vmem=VMEM_LIMIT):
    return pltpu.CompilerParams(dimension_semantics=("arbitrary",) * n_axes, vmem_limit_bytes=vmem)


def _round_up(n, m):
    return -(-n // m) * m


def _resident(shape, index_map):
    return pl.BlockSpec(shape, index_map, pipeline_mode=pl.Buffered(1))


def _rope128(x, cos, sin_signed):
    return x * cos + pltpu.roll(x, 64, 1) * sin_signed


def _rope64(x, cos, sin_signed, first_half):
    rot = jnp.where(first_half, pltpu.roll(x, 96, 1), pltpu.roll(x, 32, 1))
    return x * cos + rot * sin_signed


def _rope_tables(pos):
    pos = pos.astype(F32)[:, None]
    lane = np.arange(LANES)

    def tables(dim):
        half = dim // 2
        inv = 1.0 / (ROPE_THETA ** (jnp.arange(half, dtype=F32) / half))
        ang = pos * inv[None, :]
        cos, sin = jnp.cos(ang), jnp.sin(ang)
        idx = lane % half
        sign = np.where((lane % dim) < half, -1.0, 1.0).astype(np.float32)
        return cos[:, idx], sin[:, idx] * sign[None, :]

    c128, s128 = tables(128)
    c64, s64 = tables(64)
    return c128, s128, c64, s64


def _proj_kernel(x_ref, w_ref, wuq_ref, gq_ref, gkv_ref, c128_ref, s128_ref, c64_ref, s64_ref,
                 ckv_ref, kpe_ref, kd_ref, vd_ref, ki_ref,
                 q_ref, qd_ref, qi_ref, wi_ref, kpeb_ref, kdb_ref, vdb_ref, kilo_ref, kihi_ref):
    xb = x_ref[...].astype(BF16)
    c128, s128 = c128_ref[...], s128_ref[...]
    c64, s64 = c64_ref[...], s64_ref[...]
    first_half = (lax.broadcasted_iota(jnp.int32, c64.shape, 1) % 64) < 32

    def seg(a, b):
        return _dot(xb, w_ref[:, a:b])

    def rms(v, g):
        return v * lax.rsqrt(jnp.mean(v * v, axis=-1, keepdims=True) + RMS_EPS) * g

    qn = rms(seg(_C_CQ, _C_CQ + MLA_Q_LORA), gq_ref[...]).astype(BF16)
    for h in range(MLA_HEADS):
        qh = _dot(qn, wuq_ref[:, h * 256:(h + 1) * 256])
        q_ref[h, :, 0:128] = qh[:, 0:128].astype(BF16)
        q_ref[h, :, 128:256] = _rope64(qh[:, 128:256], c64, s64, first_half).astype(BF16)

    ckv_ref[...] = rms(seg(_C_CKV, _C_CKV + MLA_KV_LORA), gkv_ref[...])

    def slab_pair(a):
        y = seg(a, a + 256)
        return y[:, 0:128], y[:, 128:256]

    for hp in range(DSA_HEADS // 2):
        for j, slab in enumerate(slab_pair(_C_QD + hp * 256)):
            qd_ref[2 * hp + j] = _rope128(slab, c128, s128).astype(BF16)
    for c, slab in enumerate(slab_pair(_C_KD)):
        kdc = _rope128(slab, c128, s128)
        kd_ref[:, c * 128:(c + 1) * 128] = kdc
        kdb_ref[:, c * 128:(c + 1) * 128] = kdc.astype(BF16)
    vd = seg(_C_VD, _C_VD + 256)
    vd_ref[...] = vd
    vdb_ref[...] = vd.astype(BF16)

    for hq in range(IDX_HEADS // 4):
        for j, slab in enumerate(slab_pair(_C_QI + hq * 256)):
            qi_ref[2 * hq + j] = _rope64(slab, c64, s64, first_half).astype(BF16)

    kr, kilo = slab_pair(_C_KR)
    kr = _rope64(kr, c64, s64, first_half)
    kpe_ref[...] = kr[:, 0:MLA_ROPE]
    kpeb_ref[...] = kr.astype(BF16)
    kilo = _rope64(kilo, c64, s64, first_half)
    ki_ref[...] = kilo[:, 0:IDX_DIM]
    kilo_ref[...] = kilo.astype(BF16)
    kihi, wi = slab_pair(_C_KIHI)
    kihi_ref[...] = _rope64(kihi, c64, s64, first_half).astype(BF16)
    wi_ref[...] = wi * (IDX_DIM ** -0.5 * IDX_HEADS ** -0.5)


def _proj(x2d, pos, wp, tm):
    n, d = x2d.shape
    c128, s128, c64, s64 = _rope_tables(pos)
    row = lambda w: pl.BlockSpec((tm, w), lambda i: (i, 0))
    heads = lambda nh, w: pl.BlockSpec((nh, tm, w), lambda i: (0, i, 0))
    out_shapes = (
        jax.ShapeDtypeStruct((n, MLA_KV_LORA), F32),
        jax.ShapeDtypeStruct((n, MLA_ROPE), F32),
        jax.ShapeDtypeStruct((n, 256), F32),
        jax.ShapeDtypeStruct((n, 256), F32),
        jax.ShapeDtypeStruct((n, IDX_DIM), F32),
        jax.ShapeDtypeStruct((MLA_HEADS, n, 256), BF16),
        jax.ShapeDtypeStruct((DSA_HEADS, n, 128), BF16),
        jax.ShapeDtypeStruct((IDX_HEADS // 2, n, 128), BF16),
        jax.ShapeDtypeStruct((n, 128), F32),
        jax.ShapeDtypeStruct((n, 128), BF16),
        jax.ShapeDtypeStruct((n, 256), BF16),
        jax.ShapeDtypeStruct((n, 256), BF16),
        jax.ShapeDtypeStruct((n, 128), BF16),
        jax.ShapeDtypeStruct((n, 128), BF16),
    )
    out_specs = (row(MLA_KV_LORA), row(MLA_ROPE), row(256), row(256), row(IDX_DIM),
                 heads(MLA_HEADS, 256), heads(DSA_HEADS, 128), heads(IDX_HEADS // 2, 128),
                 row(128), row(128), row(256), row(256), row(128), row(128))
    in_specs = [row(d),
                _resident((d, _IN_COLS_P), lambda i: (0, 0)),
                _resident((MLA_Q_LORA, MLA_HEADS * 256), lambda i: (0, 0)),
                _resident((1, MLA_Q_LORA), lambda i: (0, 0)),
                _resident((1, MLA_KV_LORA), lambda i: (0, 0)),
                row(128), row(128), row(128), row(128)]
    return pl.pallas_call(
        _proj_kernel, grid=(n // tm,), in_specs=in_specs, out_specs=out_specs, out_shape=out_shapes,
        compiler_params=_params(1), name="proj",
    )(x2d, wp["w_in"], wp["w_uq"], wp["g_q"], wp["g_kv"], c128, s128, c64, s64)


def _kvup_kernel(ckv_ref, kpeb_ref, wk_ref, wv_ref, k_ref, v_ref):
    cb = ckv_ref[...].astype(BF16)
    kpe = kpeb_ref[...]
    for hp in range(MLA_HEADS // 2):
        k2 = _dot(cb, wk_ref[:, hp * 256:(hp + 1) * 256]).astype(BF16)
        v2 = _dot(cb, wv_ref[:, hp * 256:(hp + 1) * 256]).astype(BF16)
        for j in range(2):
            h = 2 * hp + j
            k_ref[h, :, 0:128] = k2[:, j * 128:(j + 1) * 128]
            k_ref[h, :, 128:256] = kpe
            v_ref[h] = v2[:, j * 128:(j + 1) * 128]


def _kv_up(ckv2d, kpeb2d, wp, tm):
    n = ckv2d.shape[0]
    return pl.pallas_call(
        _kvup_kernel, grid=(n // tm,),
        in_specs=[pl.BlockSpec((tm, MLA_KV_LORA), lambda i: (i, 0)),
                  pl.BlockSpec((tm, 128), lambda i: (i, 0)),
                  _resident((MLA_KV_LORA, MLA_HEADS * MLA_NOPE), lambda i: (0, 0)),
                  _resident((MLA_KV_LORA, MLA_HEADS * MLA_V), lambda i: (0, 0))],
        out_specs=(pl.BlockSpec((MLA_HEADS, tm, 256), lambda i: (0, i, 0)),
                   pl.BlockSpec((MLA_HEADS, tm, 128), lambda i: (0, i, 0))),
        out_shape=(jax.ShapeDtypeStruct((MLA_HEADS, n, 256), BF16),
                   jax.ShapeDtypeStruct((MLA_HEADS, n, 128), BF16)),
        compiler_params=_params(1), name="kv_up",
    )(ckv2d, kpeb2d, wp["w_uk"], wp["w_uv"])


def _row_max(s):
    return jnp.broadcast_to(jnp.max(s, axis=1, keepdims=True), (s.shape[0], LANES))


def _softmax_probs(s, smax, m_scr, l_scr, scale):
    coef = scale * LOG2E
    m_prev = m_scr[...]
    m_next = m_prev
    for sm in smax:
        m_next = jnp.maximum(m_next, sm)
    m_wide = jnp.concatenate([m_next] * (KCH // LANES), axis=1)
    alpha = jnp.exp2((m_prev - m_next) * coef)
    l_sum, probs = None, []
    for s_ch in s:
        p = jnp.exp2((s_ch - m_wide) * coef)
        p_sum = jnp.sum(p, axis=1, keepdims=True)
        l_sum = p_sum if l_sum is None else l_sum + p_sum
        probs.append(p.astype(BF16))
    l_scr[...] = alpha * l_scr[...] + l_sum
    m_scr[...] = m_next
    return alpha, probs


def _accumulate_pv(alpha, probs, v, acc_scr):
    pv = None
    for p_ch, v_ch in zip(probs, v):
        pv_ch = _dot(p_ch, v_ch)
        pv = pv_ch if pv is None else pv + pv_ch
    acc_scr[...] = acc_scr[...] * alpha + pv


def _chunk_mask(q0, kb0, tq, tk):
    qch = (q0 + lax.broadcasted_iota(jnp.int32, (tq, 1), 0)) >> CHUNK_SHIFT
    kch = (kb0 + lax.broadcasted_iota(jnp.int32, (1, tk), 1)) >> CHUNK_SHIFT
    return kch <= qch


def _three_stage_key_steps(q0, w, qk, sm, pv):
    n = q0 // w
    at = lambda i: pl.multiple_of(i * w, w)

    @pl.when(n == 0)
    def _():
        qk(0, 0)
        sm(0, 0, True)
        pv(0, 0)

    @pl.when(n >= 1)
    def _():
        qk(0, 0)
        qk(w, 1)
        sm(0, 0, False)
        pairs = (n - 1) // 2

        def body(t, carry):
            i = 2 * t
            qk(at(i + 2), 0)
            sm(1, at(i + 1), False)
            pv(0, at(i))
            qk(at(i + 3), 1)
            sm(0, at(i + 2), False)
            pv(1, at(i + 1))
            return carry

        lax.fori_loop(0, pairs, body, 0)
        i = 2 * pairs

        @pl.when(n - i == 1)
        def _():
            sm(1, at(i + 1), True)
            pv(0, at(i))
            pv(1, at(i + 1))

        @pl.when(n - i == 2)
        def _():
            qk(at(i + 2), 0)
            sm(1, at(i + 1), False)
            pv(0, at(i))
            sm(0, at(i + 2), True)
            pv(1, at(i + 1))
            pv(0, at(i + 2))

    return (n + 1) * w


def _direct_key_steps(q0, w, fn, wide=2):
    n = q0 // w
    ww = wide * w
    lax.fori_loop(0, n // wide, lambda i, c: (fn(pl.multiple_of(i * ww, ww), ww, False), c)[1], 0)
    lax.fori_loop((n // wide) * wide, n, lambda i, c: (fn(pl.multiple_of(i * w, w), w, False), c)[1], 0)
    fn(pl.multiple_of(n * w, w), w, True)
    return (n + 1) * w


def _mla_kernel(q_ref, k_ref, v_ref, o_ref, s_scr, smax_scr, p_scr, alpha_scr, m_scr, l_scr, acc_scr,
                *, tq, w, q_pos0, scale):
    q0 = q_pos0 + pl.program_id(2) * tq
    m_scr[...] = jnp.full(m_scr.shape, MASKED, F32)
    l_scr[...] = jnp.zeros(l_scr.shape, F32)
    acc_scr[...] = jnp.zeros(acc_scr.shape, F32)
    q = q_ref[0, 0]

    nch = w // KCH

    def qk(k0, buf):
        for ch in range(nch):
            s = _dot_nt(q, k_ref[0, 0, pl.ds(k0 + ch * KCH, KCH), :])
            s_scr[buf, ch] = s
            smax_scr[buf, ch] = _row_max(s)

    def sm(buf, k0, masked):
        s = [s_scr[buf, ch] for ch in range(nch)]
        if masked:
            s = [jnp.where(_chunk_mask(q0, k0 + ch * KCH, tq, KCH), s[ch], MASKED) for ch in range(nch)]
            smax = [_row_max(s_ch) for s_ch in s]
        else:
            smax = [smax_scr[buf, ch] for ch in range(nch)]
        alpha, probs = _softmax_probs(s, smax, m_scr, l_scr, scale)
        alpha_scr[buf] = alpha
        for ch in range(nch):
            p_scr[buf, ch] = probs[ch]

    def pv(buf, k0):
        _accumulate_pv(alpha_scr[buf], [p_scr[buf, ch] for ch in range(nch)],
                       [v_ref[0, 0, pl.ds(k0 + ch * KCH, KCH), :] for ch in range(nch)], acc_scr)

    _three_stage_key_steps(q0, w, qk, sm, pv)
    o_ref[0] = (acc_scr[...] / l_scr[...]).astype(o_ref.dtype)


def _mla_attention(q, k, v, q_pos0, tq, w):
    nh, b, t, _ = q.shape
    s = k.shape[2]
    assert w % tq == 0 and w % KCH == 0 and s % w == 0 and q_pos0 % tq == 0 and tq % CHUNK == 0
    kern = functools.partial(_mla_kernel, tq=tq, w=w, q_pos0=q_pos0,
                             scale=(MLA_NOPE + MLA_ROPE) ** -0.5)
    return pl.pallas_call(
        kern, grid=(b, nh, t // tq),
        in_specs=[pl.BlockSpec((1, 1, tq, 256), lambda bi, h, i: (h, bi, i, 0)),
                  pl.BlockSpec((1, 1, s, 256), lambda bi, h, i: (h, bi, 0, 0)),
                  pl.BlockSpec((1, 1, s, 128), lambda bi, h, i: (h, bi, 0, 0))],
        out_specs=pl.BlockSpec((1, tq, 128), lambda bi, h, i: (bi, i, h)),
        out_shape=jax.ShapeDtypeStruct((b, t, nh * MLA_V), BF16),
        scratch_shapes=[pltpu.VMEM((2, w // KCH, tq, KCH), F32), pltpu.VMEM((2, w // KCH, tq, LANES), F32),
                        pltpu.VMEM((2, w // KCH, tq, KCH), BF16), pltpu.VMEM((2, tq, LANES), F32),
                        pltpu.VMEM((tq, LANES), F32), pltpu.VMEM((tq, LANES), F32),
                        pltpu.VMEM((tq, MLA_V), F32)],
        compiler_params=_params(3), name="mla_attn",
    )(q, k, v)


def _dsa_kernel(qd_ref, qi_ref, wi_ref, kd_ref, vd_ref, kilo_ref, kihi_ref, o_ref,
                key_scr, top_scr, c_scr, hi_scr, cnt_scr, cand_scr, m_scr, l_scr, acc_scr,
                *, tq, w, wide_c, q_pos0, s_real, topk, idx_bits, scale):
    q0 = q_pos0 + pl.program_id(1) * tq
    sub = w // LANES

    qi_all = qi_ref[:, 0].reshape(IDX_HEADS // 2 * tq, LANES)
    wi = wi_ref[0]

    def to_key(f):
        bits = pltpu.bitcast(f, jnp.int32)
        return bits ^ ((bits >> 31) & 0x7FFFFFFF)

    def score_step(k0, width, masked):
        lo = _dot_nt(qi_all, kilo_ref[0, pl.ds(k0, width), :])
        hi = _dot_nt(qi_all, kihi_ref[0, pl.ds(k0, width), :])
        score = jnp.zeros((tq, width), F32)
        for hp in range(IDX_HEADS // 2):
            rows = slice(hp * tq, (hp + 1) * tq)
            score = score + wi[:, 2 * hp:2 * hp + 1] * jnp.maximum(lo[rows], 0.0)
            score = score + wi[:, 2 * hp + 1:2 * hp + 2] * jnp.maximum(hi[rows], 0.0)
        key = to_key(score)
        if masked:
            visible_here = _chunk_mask(q0, k0, tq, width)
            key = jnp.where(visible_here, key, INT_MIN)
            score = jnp.where(visible_here, score, -jnp.inf)
        for u in range(width // LANES):
            key_scr[k0 // LANES + u] = key[:, u * LANES:(u + 1) * LANES]
        top1, top2 = top_scr[0], top_scr[1]
        for u in range(width // LANES):
            x = score[:, u * LANES:(u + 1) * LANES]
            top2 = jnp.maximum(top2, jnp.minimum(top1, x))
            top1 = jnp.maximum(top1, x)
        top_scr[0] = top1
        top_scr[1] = top2

    top_scr[...] = jnp.full(top_scr.shape, -jnp.inf, F32)
    k_end = _direct_key_steps(q0, w, score_step)
    n_sb = k_end // w

    qpos = q0 + lax.broadcasted_iota(jnp.int32, (tq, LANES), 0)
    visible = jnp.minimum(((qpos >> CHUNK_SHIFT) + 1) * CHUNK, s_real)
    k_target = jnp.minimum(visible, topk).astype(F32)

    def count(mode):
        cand = cand_scr[...]
        cval = c_scr[...]

        def span(k0, nblk, acc):
            for u in range(nblk):
                blk = key_scr[k0 // LANES + u]
                if mode == "ge":
                    hit = blk >= cand
                else:
                    idx = k0 + u * LANES + lax.broadcasted_iota(jnp.int32, (tq, LANES), 1)
                    hit = jnp.where(blk == cval, idx, jnp.int32(2 ** 30)) < cand
                acc = acc + jnp.where(hit, 1.0, 0.0)
            return acc

        acc = lax.fori_loop(0, n_sb // 4,
                            lambda i, a: span(pl.multiple_of(i * 4 * w, 4 * w), 4 * sub, a),
                            jnp.zeros((tq, LANES), F32))
        acc = lax.fori_loop(4 * (n_sb // 4), n_sb,
                            lambda i, a: span(pl.multiple_of(i * w, w), sub, a), acc)
        cnt_scr[...] = jnp.broadcast_to(jnp.sum(acc, axis=1, keepdims=True), (tq, LANES))

    lane_min = lambda a: jnp.broadcast_to(jnp.min(a, axis=1, keepdims=True), (tq, LANES))
    lane_max = lambda a: jnp.broadcast_to(jnp.max(a, axis=1, keepdims=True), (tq, LANES))
    lo_f = lane_min(top_scr[1])
    hi_f = jnp.where(k_target > float(LANES), lane_max(top_scr[1]), lane_max(top_scr[0]))
    c_lo = jnp.where(lo_f == -jnp.inf, INT_MIN, jnp.where(lo_f == 0.0, -1, to_key(lo_f)))
    c_hi = jnp.where(hi_f == 0.0, 0, to_key(hi_f))

    c_scr[...] = c_lo
    hi_scr[...] = c_hi + 1

    def midpoint(lo, hi):
        return (lo >> 1) + (hi >> 1) + (lo & hi & 1)

    def bisect(carry):
        it, _, cnt_lo = carry
        lo, hi = c_scr[...], hi_scr[...]
        mid = midpoint(lo, hi)
        cand = jnp.where(cnt_lo == k_target, lo, mid)
        cand_scr[...] = cand
        count("ge")
        cnt = cnt_scr[...]
        take = cnt >= k_target
        lo, hi = jnp.where(take, cand, lo), jnp.where(take, hi, cand)
        cnt_lo = jnp.where(take, cnt, cnt_lo)
        c_scr[...] = lo
        hi_scr[...] = hi
        open_rows = jnp.logical_and(cnt_lo != k_target, midpoint(lo, hi) != lo)
        return it + 1, jnp.max(jnp.where(open_rows, 1.0, 0.0)) > 0.0, cnt_lo

    _, _, cnt_lo = lax.while_loop(lambda carry: jnp.logical_and(carry[1], carry[0] < 40), bisect,
                                  (jnp.int32(0), jnp.bool_(True), jnp.full((tq, LANES), -1.0, F32)))
    ties = jnp.max(jnp.abs(cnt_lo - k_target)) > 0.0

    @pl.when(ties)
    def _():
        cand_scr[...] = c_scr[...] + 1
        count("ge")
        need = k_target - cnt_scr[...]
        x = jnp.zeros((tq, LANES), jnp.int32)
        for bit in range(idx_bits - 1, -1, -1):
            cand_scr[...] = x + (1 << bit)
            count("eq_lt")
            x = jnp.where(cnt_scr[...] < need, x + (1 << bit), x)
        cand_scr[...] = x

        def demote(kb, carry):
            cval, last = c_scr[...], cand_scr[...]
            idx = kb * LANES + lax.broadcasted_iota(jnp.int32, (tq, LANES), 1)
            blk = key_scr[kb]
            drop = jnp.where(blk == cval, idx, jnp.int32(-1)) > last
            key_scr[kb] = jnp.where(drop, cval - 1, blk)
            return carry

        lax.fori_loop(0, k_end // LANES, demote, 0)

    m_scr[...] = jnp.full(m_scr.shape, MASKED, F32)
    l_scr[...] = jnp.zeros(l_scr.shape, F32)
    acc_scr[...] = jnp.zeros(acc_scr.shape, F32)

    rows_c = DSA_GROUP * tq

    def attend(k0, width, masked):
        del masked
        chunks = range(width // KCH)
        cval = c_scr[...]
        per_chunk = KCH // LANES
        bias = [jnp.concatenate([jnp.where(key_scr[k0 // LANES + ch * per_chunk + u] >= cval, 0.0, MASKED)
                                 for u in range(per_chunk)], axis=1) for ch in chunks]
        bias = [jnp.concatenate([b] * DSA_GROUP, axis=0) for b in bias]
        for c in range(DSA_KV_HEADS):
            qg = qd_ref[c * DSA_GROUP:(c + 1) * DSA_GROUP, 0].reshape(rows_c, DSA_HEAD_DIM)
            cols = slice(c * DSA_HEAD_DIM, (c + 1) * DSA_HEAD_DIM)
            s = [_dot_nt(qg, kd_ref[0, pl.ds(k0 + ch * KCH, KCH), cols]) + bias[ch] for ch in chunks]
            alpha, probs = _softmax_probs(s, [_row_max(s_ch) for s_ch in s],
                                          m_scr.at[c], l_scr.at[c], scale)
            _accumulate_pv(alpha, probs, [vd_ref[0, pl.ds(k0 + ch * KCH, KCH), cols] for ch in chunks],
                           acc_scr.at[c])

    _direct_key_steps(q0, w, attend, wide=wide_c)
    for c in range(DSA_KV_HEADS):
        o = acc_scr[c] / l_scr[c]
        for g in range(DSA_GROUP):
            h = c * DSA_GROUP + g
            o_ref[0, :, h * DSA_HEAD_DIM:(h + 1) * DSA_HEAD_DIM] = o[g * tq:(g + 1) * tq].astype(o_ref.dtype)


def _dsa_attention(qd, qi, wi, kd, vd, kilo, kihi, q_pos0, s_real, tq, w, wide_c):
    _, b, t, _ = qd.shape
    s = kd.shape[1]
    assert w % tq == 0 and w % KCH == 0 and s % w == 0 and q_pos0 % tq == 0 and tq % CHUNK == 0
    kern = functools.partial(
        _dsa_kernel, tq=tq, w=w, wide_c=wide_c, q_pos0=q_pos0, s_real=s_real,
        topk=min(IDX_TOPK, s_real // 4), idx_bits=int(s).bit_length(), scale=DSA_HEAD_DIM ** -0.5)
    heads = lambda: pl.BlockSpec((8, 1, tq, 128), lambda bi, i: (0, bi, i, 0))
    keys = lambda width: _resident((1, s, width), lambda bi, i: (bi, 0, 0))
    return pl.pallas_call(
        kern, grid=(b, t // tq),
        in_specs=[heads(), heads(), pl.BlockSpec((1, tq, 128), lambda bi, i: (bi, i, 0)),
                  keys(256), keys(256), keys(128), keys(128)],
        out_specs=pl.BlockSpec((1, tq, DSA_HEADS * DSA_HEAD_DIM), lambda bi, i: (bi, i, 0)),
        out_shape=jax.ShapeDtypeStruct((b, t, DSA_HEADS * DSA_HEAD_DIM), BF16),
        scratch_shapes=[pltpu.VMEM((s // LANES, tq, LANES), jnp.int32), pltpu.VMEM((2, tq, LANES), F32),
                        pltpu.VMEM((tq, LANES), jnp.int32), pltpu.VMEM((tq, LANES), jnp.int32),
                        pltpu.VMEM((tq, LANES), F32),
                        pltpu.VMEM((tq, LANES), jnp.int32),
                        pltpu.VMEM((DSA_KV_HEADS, DSA_GROUP * tq, LANES), F32),
                        pltpu.VMEM((DSA_KV_HEADS, DSA_GROUP * tq, LANES), F32),
                        pltpu.VMEM((DSA_KV_HEADS, DSA_GROUP * tq, DSA_HEAD_DIM), F32)],
        compiler_params=_params(2), name="dsa_attn",
    )(qd, qi, wi, kd, vd, kilo, kihi)


def _layer_norm(v, g, b):
    mu = jnp.mean(v, axis=-1, keepdims=True)
    d = v - mu
    var = jnp.mean(d * d, axis=-1, keepdims=True)
    return d * lax.rsqrt(var + LN_EPS) * g + b


def _outln_kernel(mla_ref, dsa_ref, x_ref, wo_ref, g_ref, b_ref, o_ref, *, alpha, half):
    a = _dot(mla_ref[...], wo_ref[0:half, :]) + _dot(dsa_ref[...], wo_ref[half:, :])
    o_ref[...] = _layer_norm(alpha * x_ref[...] + a, g_ref[...], b_ref[...])


def _out_ln(mla_o, dsa_o, x2d, wp, alpha, tm):
    n, d = x2d.shape
    half = mla_o.shape[1]
    row = lambda w: pl.BlockSpec((tm, w), lambda i: (i, 0))
    return pl.pallas_call(
        functools.partial(_outln_kernel, alpha=alpha, half=half), grid=(n // tm,),
        in_specs=[row(half), row(dsa_o.shape[1]), row(d),
                  _resident(wp["w_o"].shape, lambda i: (0, 0)),
                  _resident((1, d), lambda i: (0, 0)), _resident((1, d), lambda i: (0, 0))],
        out_specs=row(d), out_shape=jax.ShapeDtypeStruct((n, d), F32),
        compiler_params=_params(1), name="out_ln",
    )(mla_o, dsa_o, x2d, wp["w_o"], wp["ln1_g"], wp["ln1_b"])


def _ffn_kernel(x_ref, wg_ref, wu_ref, wd_ref, g_ref, b_ref, o_ref, xb_scr, acc_scr, *, alpha):
    j = pl.program_id(1)

    @pl.when(j == 0)
    def _():
        xb_scr[...] = x_ref[...].astype(BF16)
        acc_scr[...] = jnp.zeros(acc_scr.shape, F32)

    xb = xb_scr[...]
    gate = _dot(xb, wg_ref[...])
    up = _dot(xb, wu_ref[...])
    hidden = gate * (1.0 / (1.0 + jnp.exp(-gate))) * up
    acc_scr[...] += _dot(hidden.astype(BF16), wd_ref[...])

    @pl.when(j == pl.num_programs(1) - 1)
    def _():
        o_ref[...] = _layer_norm(alpha * x_ref[...] + acc_scr[...], g_ref[...], b_ref[...])


def _ffn_ln(x2d, wp, alpha, tm, tf):
    n, d = x2d.shape
    dff = wp["w_gate"].shape[1]
    return pl.pallas_call(
        functools.partial(_ffn_kernel, alpha=alpha), grid=(n // tm, dff // tf),
        in_specs=[pl.BlockSpec((tm, d), lambda i, j: (i, 0)),
                  pl.BlockSpec((d, tf), lambda i, j: (0, j)),
                  pl.BlockSpec((d, tf), lambda i, j: (0, j)),
                  pl.BlockSpec((tf, d), lambda i, j: (j, 0)),
                  _resident((1, d), lambda i, j: (0, 0)), _resident((1, d), lambda i, j: (0, 0))],
        out_specs=pl.BlockSpec((tm, d), lambda i, j: (i, 0)),
        out_shape=jax.ShapeDtypeStruct((n, d), F32),
        scratch_shapes=[pltpu.VMEM((tm, d), BF16), pltpu.VMEM((tm, d), F32)],
        compiler_params=_params(2), name="ffn_ln",
    )(x2d, wp["w_gate"], wp["w_up"], wp["w_down"], wp["ln2_g"], wp["ln2_b"])


def _pack_weights(w_in, w_uq, g_q, w_ukv, g_kv, w_o, ln1_g, ln1_b, w_gate, w_up, w_down, ln2_g, ln2_b):
    d = w_in.shape[0]
    splits = (MLA_Q_LORA, MLA_KV_LORA, MLA_ROPE, DSA_HEADS * DSA_HEAD_DIM, DSA_KV_HEADS * DSA_HEAD_DIM,
              DSA_KV_HEADS * DSA_HEAD_DIM, IDX_HEADS * IDX_DIM, IDX_DIM, IDX_HEADS)
    offs = np.cumsum(splits)[:-1].tolist()
    c_q, c_kv, k_r, q_d, k_d, v_d, q_i, k_i, w_i = jnp.split(w_in, offs, axis=1)
    z = lambda n: jnp.zeros((d, n), w_in.dtype)
    w_in_p = jnp.concatenate(
        [c_q, c_kv, q_d, k_d, v_d, q_i, k_r, z(64), k_i, z(64), z(64), k_i, w_i, z(128 - IDX_HEADS)], axis=1)
    assert w_in_p.shape[1] == _IN_COLS_P
    w_uq_p = jnp.pad(w_uq, ((0, 0), (0, 0), (0, 256 - MLA_NOPE - MLA_ROPE)))
    return {
        "w_in": w_in_p.astype(BF16),
        "w_uq": w_uq_p.reshape(MLA_Q_LORA, MLA_HEADS * 256).astype(BF16),
        "g_q": g_q.reshape(1, -1), "g_kv": g_kv.reshape(1, -1),
        "w_uk": w_ukv[:, :, :MLA_NOPE].reshape(MLA_KV_LORA, MLA_HEADS * MLA_NOPE).astype(BF16),
        "w_uv": w_ukv[:, :, MLA_NOPE:].reshape(MLA_KV_LORA, MLA_HEADS * MLA_V).astype(BF16),
        "w_o": w_o.astype(BF16),
        "ln1_g": ln1_g.reshape(1, -1), "ln1_b": ln1_b.reshape(1, -1),
        "w_gate": w_gate.astype(BF16), "w_up": w_up.astype(BF16), "w_down": w_down.astype(BF16),
        "ln2_g": ln2_g.reshape(1, -1), "ln2_b": ln2_b.reshape(1, -1),
    }


def _trunk_layer(x, q_pos0, past, wp, alpha, cfg):
    b, t, d = x.shape
    n = b * t
    x2d = x.reshape(n, d)
    pos = jnp.tile(q_pos0 + jnp.arange(t, dtype=jnp.int32), b)
    (ckv, kpe, kd, vd, ki, q_mla, qd, qi, wi, kpeb, kdb, vdb, kilo, kihi) = _proj(x2d, pos, wp, cfg["tm_proj"])
    new_rows = (ckv.reshape(b, t, -1), kpe.reshape(b, t, -1),
                kd.reshape(b, t, DSA_KV_HEADS, DSA_HEAD_DIM), vd.reshape(b, t, DSA_KV_HEADS, DSA_HEAD_DIM),
                ki.reshape(b, t, -1))

    per_b = lambda a: a.reshape(b, t, a.shape[-1])
    ckv_all, kpeb_all, kdb_all, vdb_all, kilo_all, kihi_all = map(per_b, (ckv, kpeb, kdb, vdb, kilo, kihi))
    s_real = t
    if past is not None:
        p_ckv, p_kpe, p_kd, p_vd, p_ki = past
        s_real = p_ckv.shape[1] + t
        z64 = jnp.zeros(p_kpe.shape, BF16)
        cat = lambda c, nw: jnp.concatenate([c, nw], axis=1)
        ckv_all = cat(p_ckv, ckv_all)
        kpeb_all = cat(jnp.concatenate([p_kpe.astype(BF16), z64], axis=-1), kpeb_all)
        kdb_all = cat(p_kd.reshape(b, -1, 256).astype(BF16), kdb_all)
        vdb_all = cat(p_vd.reshape(b, -1, 256).astype(BF16), vdb_all)
        kilo_all = cat(jnp.concatenate([p_ki.astype(BF16), z64], axis=-1), kilo_all)
        kihi_all = cat(jnp.concatenate([z64, p_ki.astype(BF16)], axis=-1), kihi_all)
    s_pad = _round_up(s_real, max(cfg["w_mla"], cfg["w_dsa"]))
    if s_pad != s_real:
        padk = lambda a: jnp.pad(a, ((0, 0), (0, s_pad - s_real), (0, 0)))
        ckv_all, kpeb_all, kdb_all, vdb_all, kilo_all, kihi_all = map(
            padk, (ckv_all, kpeb_all, kdb_all, vdb_all, kilo_all, kihi_all))

    k_mla, v_mla = _kv_up(ckv_all.reshape(b * s_pad, -1), kpeb_all.reshape(b * s_pad, -1), wp, cfg["tm_kv"])
    k_mla = k_mla.reshape(MLA_HEADS, b, s_pad, 256)
    v_mla = v_mla.reshape(MLA_HEADS, b, s_pad, 128)
    mla_o = _mla_attention(q_mla.reshape(MLA_HEADS, b, t, 256), k_mla, v_mla, q_pos0,
                           cfg["tq_mla"], cfg["w_mla"])
    dsa_o = _dsa_attention(qd.reshape(DSA_HEADS, b, t, 128), qi.reshape(IDX_HEADS // 2, b, t, 128),
                           wi.reshape(b, t, 128), kdb_all, vdb_all, kilo_all, kihi_all,
                           q_pos0, s_real, cfg["tq_dsa"], cfg["w_dsa"], cfg["wide_dsa"])
    x1 = _out_ln(mla_o.reshape(n, -1), dsa_o.reshape(n, -1), x2d, wp, alpha, cfg["tm_out"])
    y = _ffn_ln(x1, wp, alpha, cfg["tm_ffn"], cfg["tf_ffn"])
    return y.reshape(b, t, d), new_rows


_PROMPT_CFG = dict(tm_proj=512, tm_kv=512, tq_mla=512, tq_dsa=128, w_mla=512, w_dsa=512, wide_dsa=4,
                   tm_out=512, tm_ffn=512, tf_ffn=512)
_SAMPLE_CFG = dict(tm_proj=256, tm_kv=256, tq_mla=64, tq_dsa=64, w_mla=256, w_dsa=256, wide_dsa=2,
                   tm_out=256, tm_ffn=512, tf_ffn=512)


def kernel(x_prompt, x_sample, cache_mla_ckv, cache_mla_kpe, cache_dsa_k, cache_dsa_v, cache_idx_k, w_in, w_uq, mla_q_norm_g, w_ukv, mla_kv_norm_g, w_o, ln1_g, ln1_b, w_gate, w_up, w_down, ln2_g, ln2_b):
    depth = w_in.shape[0]
    alpha = (2 * depth) ** 0.25
    past_len = cache_mla_ckv.shape[2]
    y_p, y_s = x_prompt, x_sample
    rows_p, rows_s = [], []
    for l in range(depth):
        wp = _pack_weights(w_in[l], w_uq[l], mla_q_norm_g[l], w_ukv[l], mla_kv_norm_g[l], w_o[l],
                           ln1_g[l], ln1_b[l], w_gate[l], w_up[l], w_down[l], ln2_g[l], ln2_b[l])
        y_p, r_p = _trunk_layer(y_p, 0, None, wp, alpha, _PROMPT_CFG)
        past = (cache_mla_ckv[l], cache_mla_kpe[l], cache_dsa_k[l], cache_dsa_v[l], cache_idx_k[l])
        y_s, r_s = _trunk_layer(y_s, past_len, past, wp, alpha, _SAMPLE_CFG)
        rows_p.append(r_p)
        rows_s.append(r_s)
    stack = lambda rows, i: jnp.stack([r[i] for r in rows], axis=0)
    return (y_p, y_s,
            stack(rows_p, 0), stack(rows_p, 1), stack(rows_p, 2), stack(rows_p, 3), stack(rows_p, 4),
            stack(rows_s, 0), stack(rows_s, 1), stack(rows_s, 2), stack(rows_s, 3), stack(rows_s, 4))
```

```python
import functools

import numpy as np
import jax
import jax.numpy as jnp
from jax import lax
from jax.experimental import pallas as pl
from jax.experimental.pallas import tpu as pltpu

CHUNK = 64
CHUNK_SHIFT = 6
ROPE_THETA = 10000.0
MLA_HEADS = 8
MLA_Q_LORA = 512
MLA_KV_LORA = 512
MLA_NOPE = 128
MLA_ROPE = 64
MLA_V = 128
DSA_HEADS = 8
DSA_KV_HEADS = 2
DSA_GROUP = DSA_HEADS // DSA_KV_HEADS
DSA_HEAD_DIM = 128
IDX_HEADS = 16
IDX_DIM = 64
IDX_TOPK = 256
LN_EPS = 1e-5
RMS_EPS = 1e-6

LANES = 128
KCH = 256
MASKED = -1e30
INT_MIN = -2 ** 31
LOG2E = 1.4426950408889634
VMEM_LIMIT = 56 * 1024 * 1024

_C_CQ = 0
_C_CKV = 512
_C_QD = 1024
_C_KD = 2048
_C_VD = 2304
_C_QI = 2560
_C_KR = 3584
_C_KIHI = 3840
_IN_COLS_P = 4096

F32 = jnp.float32
BF16 = jnp.bfloat16


def _dot(a, b):
    return jnp.dot(a, b, preferred_element_type=F32)


def _dot_nt(a, b):
    return lax.dot_general(a, b, (((1,), (1,)), ((), ())), preferred_element_type=F32)


def _params(n_axes, vmem=VMEM_LIMIT):
    return pltpu.CompilerParams(dimension_semantics=("arbitrary",) * n_axes, vmem_limit_bytes=vmem)


def _round_up(n, m):
    return -(-n // m) * m


def _resident(shape, index_map):
    return pl.BlockSpec(shape, index_map, pipeline_mode=pl.Buffered(1))


def _rope128(x, cos, sin_signed):
    return x * cos + pltpu.roll(x, 64, 1) * sin_signed


def _rope64(x, cos, sin_signed, first_half):
    rot = jnp.where(first_half, pltpu.roll(x, 96, 1), pltpu.roll(x, 32, 1))
    return x * cos + rot * sin_signed


def _rope_tables(pos):
    pos = pos.astype(F32)[:, None]
    lane = np.arange(LANES)

    def tables(dim):
        half = dim // 2
        inv = 1.0 / (ROPE_THETA ** (jnp.arange(half, dtype=F32) / half))
        ang = pos * inv[None, :]
        cos, sin = jnp.cos(ang), jnp.sin(ang)
        idx = lane % half
        sign = np.where((lane % dim) < half, -1.0, 1.0).astype(np.float32)
        return cos[:, idx], sin[:, idx] * sign[None, :]

    c128, s128 = tables(128)
    c64, s64 = tables(64)
    return c128, s128, c64, s64


def _proj_kernel(x_ref, w_ref, wuq_ref, gq_ref, gkv_ref, c128_ref, s128_ref, c64_ref, s64_ref,
                 ckv_ref, kpe_ref, kd_ref, vd_ref, ki_ref,
                 q_ref, qd_ref, qi_ref, wi_ref, kpeb_ref, kdb_ref, vdb_ref, kilo_ref, kihi_ref):
    xb = x_ref[...].astype(BF16)
    c128, s128 = c128_ref[...], s128_ref[...]
    c64, s64 = c64_ref[...], s64_ref[...]
    first_half = (lax.broadcasted_iota(jnp.int32, c64.shape, 1) % 64) < 32

    def seg(a, b):
        return _dot(xb, w_ref[:, a:b])

    def rms(v, g):
        return v * lax.rsqrt(jnp.mean(v * v, axis=-1, keepdims=True) + RMS_EPS) * g

    qn = rms(seg(_C_CQ, _C_CQ + MLA_Q_LORA), gq_ref[...]).astype(BF16)
    for h in range(MLA_HEADS):
        qh = _dot(qn, wuq_ref[:, h * 256:(h + 1) * 256])
        q_ref[h, :, 0:128] = qh[:, 0:128].astype(BF16)
        q_ref[h, :, 128:256] = _rope64(qh[:, 128:256], c64, s64, first_half).astype(BF16)

    ckv_ref[...] = rms(seg(_C_CKV, _C_CKV + MLA_KV_LORA), gkv_ref[...])

    def slab_pair(a):
        y = seg(a, a + 256)
        return y[:, 0:128], y[:, 128:256]

    for hp in range(DSA_HEADS // 2):
        for j, slab in enumerate(slab_pair(_C_QD + hp * 256)):
            qd_ref[2 * hp + j] = _rope128(slab, c128, s128).astype(BF16)
    for c, slab in enumerate(slab_pair(_C_KD)):
        kdc = _rope128(slab, c128, s128)
        kd_ref[:, c * 128:(c + 1) * 128] = kdc
        kdb_ref[:, c * 128:(c + 1) * 128] = kdc.astype(BF16)
    vd = seg(_C_VD, _C_VD + 256)
    vd_ref[...] = vd
    vdb_ref[...] = vd.astype(BF16)

    for hq in range(IDX_HEADS // 4):
        for j, slab in enumerate(slab_pair(_C_QI + hq * 256)):
            qi_ref[2 * hq + j] = _rope64(slab, c64, s64, first_half).astype(BF16)

    kr, kilo = slab_pair(_C_KR)
    kr = _rope64(kr, c64, s64, first_half)
    kpe_ref[...] = kr[:, 0:MLA_ROPE]
    kpeb_ref[...] = kr.astype(BF16)
    kilo = _rope64(kilo, c64, s64, first_half)
    ki_ref[...] = kilo[:, 0:IDX_DIM]
    kilo_ref[...] = kilo.astype(BF16)
    kihi, wi = slab_pair(_C_KIHI)
    kihi_ref[...] = _rope64(kihi, c64, s64, first_half).astype(BF16)
    wi_ref[...] = wi * (IDX_DIM ** -0.5 * IDX_HEADS ** -0.5)


def _proj(x2d, pos, wp, tm):
    n, d = x2d.shape
    c128, s128, c64, s64 = _rope_tables(pos)
    row = lambda w: pl.BlockSpec((tm, w), lambda i: (i, 0))
    heads = lambda nh, w: pl.BlockSpec((nh, tm, w), lambda i: (0, i, 0))
    out_shapes = (
        jax.ShapeDtypeStruct((n, MLA_KV_LORA), F32),
        jax.ShapeDtypeStruct((n, MLA_ROPE), F32),
        jax.ShapeDtypeStruct((n, 256), F32),
        jax.ShapeDtypeStruct((n, 256), F32),
        jax.ShapeDtypeStruct((n, IDX_DIM), F32),
        jax.ShapeDtypeStruct((MLA_HEADS, n, 256), BF16),
        jax.ShapeDtypeStruct((DSA_HEADS, n, 128), BF16),
        jax.ShapeDtypeStruct((IDX_HEADS // 2, n, 128), BF16),
        jax.ShapeDtypeStruct((n, 128), F32),
        jax.ShapeDtypeStruct((n, 128), BF16),
        jax.ShapeDtypeStruct((n, 256), BF16),
        jax.ShapeDtypeStruct((n, 256), BF16),
        jax.ShapeDtypeStruct((n, 128), BF16),
        jax.ShapeDtypeStruct((n, 128), BF16),
    )
    out_specs = (row(MLA_KV_LORA), row(MLA_ROPE), row(256), row(256), row(IDX_DIM),
                 heads(MLA_HEADS, 256), heads(DSA_HEADS, 128), heads(IDX_HEADS // 2, 128),
                 row(128), row(128), row(256), row(256), row(128), row(128))
    in_specs = [row(d),
                _resident((d, _IN_COLS_P), lambda i: (0, 0)),
                _resident((MLA_Q_LORA, MLA_HEADS * 256), lambda i: (0, 0)),
                _resident((1, MLA_Q_LORA), lambda i: (0, 0)),
                _resident((1, MLA_KV_LORA), lambda i: (0, 0)),
                row(128), row(128), row(128), row(128)]
    return pl.pallas_call(
        _proj_kernel, grid=(n // tm,), in_specs=in_specs, out_specs=out_specs, out_shape=out_shapes,
        compiler_params=_params(1), name="proj",
    )(x2d, wp["w_in"], wp["w_uq"], wp["g_q"], wp["g_kv"], c128, s128, c64, s64)


def _kvup_kernel(ckv_ref, kpeb_ref, wk_ref, wv_ref, k_ref, v_ref):
    cb = ckv_ref[...].astype(BF16)
    kpe = kpeb_ref[...]
    for hp in range(MLA_HEADS // 2):
        k2 = _dot(cb, wk_ref[:, hp * 256:(hp + 1) * 256]).astype(BF16)
        v2 = _dot(cb, wv_ref[:, hp * 256:(hp + 1) * 256]).astype(BF16)
        for j in range(2):
            h = 2 * hp + j
            k_ref[h, :, 0:128] = k2[:, j * 128:(j + 1) * 128]
            k_ref[h, :, 128:256] = kpe
            v_ref[h] = v2[:, j * 128:(j + 1) * 128]


def _kv_up(ckv2d, kpeb2d, wp, tm):
    n = ckv2d.shape[0]
    return pl.pallas_call(
        _kvup_kernel, grid=(n // tm,),
        in_specs=[pl.BlockSpec((tm, MLA_KV_LORA), lambda i: (i, 0)),
                  pl.BlockSpec((tm, 128), lambda i: (i, 0)),
                  _resident((MLA_KV_LORA, MLA_HEADS * MLA_NOPE), lambda i: (0, 0)),
                  _resident((MLA_KV_LORA, MLA_HEADS * MLA_V), lambda i: (0, 0))],
        out_specs=(pl.BlockSpec((MLA_HEADS, tm, 256), lambda i: (0, i, 0)),
                   pl.BlockSpec((MLA_HEADS, tm, 128), lambda i: (0, i, 0))),
        out_shape=(jax.ShapeDtypeStruct((MLA_HEADS, n, 256), BF16),
                   jax.ShapeDtypeStruct((MLA_HEADS, n, 128), BF16)),
        compiler_params=_params(1), name="kv_up",
    )(ckv2d, kpeb2d, wp["w_uk"], wp["w_uv"])


def _row_max(s):
    return jnp.broadcast_to(jnp.max(s, axis=1, keepdims=True), (s.shape[0], LANES))


def _softmax_probs(s, smax, m_scr, l_scr, scale):
    coef = scale * LOG2E
    m_prev = m_scr[...]
    m_next = m_prev
    for sm in smax:
        m_next = jnp.maximum(m_next, sm)
    m_wide = jnp.concatenate([m_next] * (KCH // LANES), axis=1)
    alpha = jnp.exp2((m_prev - m_next) * coef)
    l_sum, probs = None, []
    for s_ch in s:
        p = jnp.exp2((s_ch - m_wide) * coef)
        p_sum = jnp.sum(p, axis=1, keepdims=True)
        l_sum = p_sum if l_sum is None else l_sum + p_sum
        probs.append(p.astype(BF16))
    l_scr[...] = alpha * l_scr[...] + l_sum
    m_scr[...] = m_next
    return alpha, probs


def _accumulate_pv(alpha, probs, v, acc_scr):
    pv = None
    for p_ch, v_ch in zip(probs, v):
        pv_ch = _dot(p_ch, v_ch)
        pv = pv_ch if pv is None else pv + pv_ch
    acc_scr[...] = acc_scr[...] * alpha + pv


def _chunk_mask(q0, kb0, tq, tk):
    qch = (q0 + lax.broadcasted_iota(jnp.int32, (tq, 1), 0)) >> CHUNK_SHIFT
    kch = (kb0 + lax.broadcasted_iota(jnp.int32, (1, tk), 1)) >> CHUNK_SHIFT
    return kch <= qch


def _three_stage_key_steps(q0, w, qk, sm, pv):
    n = q0 // w
    at = lambda i: pl.multiple_of(i * w, w)

    @pl.when(n == 0)
    def _():
        qk(0, 0)
        sm(0, 0, True)
        pv(0, 0)

    @pl.when(n >= 1)
    def _():
        qk(0, 0)
        qk(w, 1)
        sm(0, 0, False)
        pairs = (n - 1) // 2

        def body(t, carry):
            i = 2 * t
            qk(at(i + 2), 0)
            sm(1, at(i + 1), False)
            pv(0, at(i))
            qk(at(i + 3), 1)
            sm(0, at(i + 2), False)
            pv(1, at(i + 1))
            return carry

        lax.fori_loop(0, pairs, body, 0)
        i = 2 * pairs

        @pl.when(n - i == 1)
        def _():
            sm(1, at(i + 1), True)
            pv(0, at(i))
            pv(1, at(i + 1))

        @pl.when(n - i == 2)
        def _():
            qk(at(i + 2), 0)
            sm(1, at(i + 1), False)
            pv(0, at(i))
            sm(0, at(i + 2), True)
            pv(1, at(i + 1))
            pv(0, at(i + 2))

    return (n + 1) * w


def _direct_key_steps(q0, w, fn, wide=2):
    n = q0 // w
    ww = wide * w
    lax.fori_loop(0, n // wide, lambda i, c: (fn(pl.multiple_of(i * ww, ww), ww, False), c)[1], 0)
    lax.fori_loop((n // wide) * wide, n, lambda i, c: (fn(pl.multiple_of(i * w, w), w, False), c)[1], 0)
    fn(pl.multiple_of(n * w, w), w, True)
    return (n + 1) * w


def _mla_kernel(q_ref, k_ref, v_ref, o_ref, s_scr, smax_scr, p_scr, alpha_scr, m_scr, l_scr, acc_scr,
                *, tq, w, q_pos0, scale):
    q0 = q_pos0 + pl.program_id(2) * tq
    m_scr[...] = jnp.full(m_scr.shape, MASKED, F32)
    l_scr[...] = jnp.zeros(l_scr.shape, F32)
    acc_scr[...] = jnp.zeros(acc_scr.shape, F32)
    q = q_ref[0, 0]

    nch = w // KCH

    def qk(k0, buf):
        for ch in range(nch):
            s = _dot_nt(q, k_ref[0, 0, pl.ds(k0 + ch * KCH, KCH), :])
            s_scr[buf, ch] = s
            smax_scr[buf, ch] = _row_max(s)

    def sm(buf, k0, masked):
        s = [s_scr[buf, ch] for ch in range(nch)]
        if masked:
            s = [jnp.where(_chunk_mask(q0, k0 + ch * KCH, tq, KCH), s[ch], MASKED) for ch in range(nch)]
            smax = [_row_max(s_ch) for s_ch in s]
        else:
            smax = [smax_scr[buf, ch] for ch in range(nch)]
        alpha, probs = _softmax_probs(s, smax, m_scr, l_scr, scale)
        alpha_scr[buf] = alpha
        for ch in range(nch):
            p_scr[buf, ch] = probs[ch]

    def pv(buf, k0):
        _accumulate_pv(alpha_scr[buf], [p_scr[buf, ch] for ch in range(nch)],
                       [v_ref[0, 0, pl.ds(k0 + ch * KCH, KCH), :] for ch in range(nch)], acc_scr)

    _three_stage_key_steps(q0, w, qk, sm, pv)
    o_ref[0] = (acc_scr[...] / l_scr[...]).astype(o_ref.dtype)


def _mla_attention(q, k, v, q_pos0, tq, w):
    nh, b, t, _ = q.shape
    s = k.shape[2]
    assert w % tq == 0 and w % KCH == 0 and s % w == 0 and q_pos0 % tq == 0 and tq % CHUNK == 0
    kern = functools.partial(_mla_kernel, tq=tq, w=w, q_pos0=q_pos0,
                             scale=(MLA_NOPE + MLA_ROPE) ** -0.5)
    return pl.pallas_call(
        kern, grid=(b, nh, t // tq),
        in_specs=[pl.BlockSpec((1, 1, tq, 256), lambda bi, h, i: (h, bi, i, 0)),
                  pl.BlockSpec((1, 1, s, 256), lambda bi, h, i: (h, bi, 0, 0)),
                  pl.BlockSpec((1, 1, s, 128), lambda bi, h, i: (h, bi, 0, 0))],
        out_specs=pl.BlockSpec((1, tq, 128), lambda bi, h, i: (bi, i, h)),
        out_shape=jax.ShapeDtypeStruct((b, t, nh * MLA_V), BF16),
        scratch_shapes=[pltpu.VMEM((2, w // KCH, tq, KCH), F32), pltpu.VMEM((2, w // KCH, tq, LANES), F32),
                        pltpu.VMEM((2, w // KCH, tq, KCH), BF16), pltpu.VMEM((2, tq, LANES), F32),
                        pltpu.VMEM((tq, LANES), F32), pltpu.VMEM((tq, LANES), F32),
                        pltpu.VMEM((tq, MLA_V), F32)],
        compiler_params=_params(3), name="mla_attn",
    )(q, k, v)


def _dsa_kernel(qd_ref, qi_ref, wi_ref, kd_ref, vd_ref, kilo_ref, kihi_ref, o_ref,
                key_scr, top_scr, c_scr, hi_scr, cnt_scr, cand_scr, m_scr, l_scr, acc_scr,
                *, tq, w, wide_c, q_pos0, s_real, topk, idx_bits, scale):
    q0 = q_pos0 + pl.program_id(1) * tq
    sub = w // LANES

    qi_all = qi_ref[:, 0].reshape(IDX_HEADS // 2 * tq, LANES)
    wi = wi_ref[0]

    def to_key(f):
        bits = pltpu.bitcast(f, jnp.int32)
        return bits ^ ((bits >> 31) & 0x7FFFFFFF)

    def score_step(k0, width, masked):
        lo = _dot_nt(qi_all, kilo_ref[0, pl.ds(k0, width), :])
        hi = _dot_nt(qi_all, kihi_ref[0, pl.ds(k0, width), :])
        score = jnp.zeros((tq, width), F32)
        for hp in range(IDX_HEADS // 2):
            rows = slice(hp * tq, (hp + 1) * tq)
            score = score + wi[:, 2 * hp:2 * hp + 1] * jnp.maximum(lo[rows], 0.0)
            score = score + wi[:, 2 * hp + 1:2 * hp + 2] * jnp.maximum(hi[rows], 0.0)
        key = to_key(score)
        if masked:
            visible_here = _chunk_mask(q0, k0, tq, width)
            key = jnp.where(visible_here, key, INT_MIN)
            score = jnp.where(visible_here, score, -jnp.inf)
        for u in range(width // LANES):
            key_scr[k0 // LANES + u] = key[:, u * LANES:(u + 1) * LANES]
        top1, top2 = top_scr[0], top_scr[1]
        for u in range(width // LANES):
            x = score[:, u * LANES:(u + 1) * LANES]
            top2 = jnp.maximum(top2, jnp.minimum(top1, x))
            top1 = jnp.maximum(top1, x)
        top_scr[0] = top1
        top_scr[1] = top2

    top_scr[...] = jnp.full(top_scr.shape, -jnp.inf, F32)
    k_end = _direct_key_steps(q0, w, score_step)
    n_sb = k_end // w

    qpos = q0 + lax.broadcasted_iota(jnp.int32, (tq, LANES), 0)
    visible = jnp.minimum(((qpos >> CHUNK_SHIFT) + 1) * CHUNK, s_real)
    k_target = jnp.minimum(visible, topk).astype(F32)

    def count(mode):
        cand = cand_scr[...]
        cval = c_scr[...]

        def span(k0, nblk, acc):
            for u in range(nblk):
                blk = key_scr[k0 // LANES + u]
                if mode == "ge":
                    hit = blk >= cand
                else:
                    idx = k0 + u * LANES + lax.broadcasted_iota(jnp.int32, (tq, LANES), 1)
                    hit = jnp.where(blk == cval, idx, jnp.int32(2 ** 30)) < cand
                acc = acc + jnp.where(hit, 1.0, 0.0)
            return acc

        acc = lax.fori_loop(0, n_sb // 4,
                            lambda i, a: span(pl.multiple_of(i * 4 * w, 4 * w), 4 * sub, a),
                            jnp.zeros((tq, LANES), F32))
        acc = lax.fori_loop(4 * (n_sb // 4), n_sb,
                            lambda i, a: span(pl.multiple_of(i * w, w), sub, a), acc)
        cnt_scr[...] = jnp.broadcast_to(jnp.sum(acc, axis=1, keepdims=True), (tq, LANES))

    lane_min = lambda a: jnp.broadcast_to(jnp.min(a, axis=1, keepdims=True), (tq, LANES))
    lane_max = lambda a: jnp.broadcast_to(jnp.max(a, axis=1, keepdims=True), (tq, LANES))
    lo_f = lane_min(top_scr[1])
    hi_f = jnp.where(k_target > float(LANES), lane_max(top_scr[1]), lane_max(top_scr[0]))
    c_lo = jnp.where(lo_f == -jnp.inf, INT_MIN, jnp.where(lo_f == 0.0, -1, to_key(lo_f)))
    c_hi = jnp.where(hi_f == 0.0, 0, to_key(hi_f))

    c_scr[...] = c_lo
    hi_scr[...] = c_hi + 1

    def midpoint(lo, hi):
        return (lo >> 1) + (hi >> 1) + (lo & hi & 1)

    def bisect(carry):
        it, _, cnt_lo = carry
        lo, hi = c_scr[...], hi_scr[...]
        mid = midpoint(lo, hi)
        cand = jnp.where(cnt_lo == k_target, lo, mid)
        cand_scr[...] = cand
        count("ge")
        cnt = cnt_scr[...]
        take = cnt >= k_target
        lo, hi = jnp.where(take, cand, lo), jnp.where(take, hi, cand)
        cnt_lo = jnp.where(take, cnt, cnt_lo)
        c_scr[...] = lo
        hi_scr[...] = hi
        open_rows = jnp.logical_and(cnt_lo != k_target, midpoint(lo, hi) != lo)
        any_open = jnp.max(jnp.where(open_rows, 1.0, 0.0), axis=0, keepdims=True)[0, 0] > 0.0
        return it + 1, any_open, cnt_lo

    _, _, cnt_lo = lax.while_loop(lambda carry: jnp.logical_and(carry[1], carry[0] < 40), bisect,
                                  (jnp.int32(0), jnp.bool_(True), jnp.full((tq, LANES), -1.0, F32)))
    ties = jnp.max(jnp.abs(cnt_lo - k_target)) > 0.0

    @pl.when(ties)
    def _():
        cand_scr[...] = c_scr[...] + 1
        count("ge")
        need = k_target - cnt_scr[...]
        x = jnp.zeros((tq, LANES), jnp.int32)
        for bit in range(idx_bits - 1, -1, -1):
            cand_scr[...] = x + (1 << bit)
            count("eq_lt")
            x = jnp.where(cnt_scr[...] < need, x + (1 << bit), x)
        cand_scr[...] = x

        def demote(kb, carry):
            cval, last = c_scr[...], cand_scr[...]
            idx = kb * LANES + lax.broadcasted_iota(jnp.int32, (tq, LANES), 1)
            blk = key_scr[kb]
            drop = jnp.where(blk == cval, idx, jnp.int32(-1)) > last
            key_scr[kb] = jnp.where(drop, cval - 1, blk)
            return carry

        lax.fori_loop(0, k_end // LANES, demote, 0)

    m_scr[...] = jnp.full(m_scr.shape, MASKED, F32)
    l_scr[...] = jnp.zeros(l_scr.shape, F32)
    acc_scr[...] = jnp.zeros(acc_scr.shape, F32)

    rows_c = DSA_GROUP * tq

    def attend(k0, width, masked):
        del masked
        chunks = range(width // KCH)
        cval = c_scr[...]
        per_chunk = KCH // LANES
        bias = [jnp.concatenate([jnp.where(key_scr[k0 // LANES + ch * per_chunk + u] >= cval, 0.0, MASKED)
                                 for u in range(per_chunk)], axis=1) for ch in chunks]
        bias = [jnp.concatenate([b] * DSA_GROUP, axis=0) for b in bias]
        for c in range(DSA_KV_HEADS):
            qg = qd_ref[c * DSA_GROUP:(c + 1) * DSA_GROUP, 0].reshape(rows_c, DSA_HEAD_DIM)
            cols = slice(c * DSA_HEAD_DIM, (c + 1) * DSA_HEAD_DIM)
            s = [_dot_nt(qg, kd_ref[0, pl.ds(k0 + ch * KCH, KCH), cols]) + bias[ch] for ch in chunks]
            alpha, probs = _softmax_probs(s, [_row_max(s_ch) for s_ch in s],
                                          m_scr.at[c], l_scr.at[c], scale)
            _accumulate_pv(alpha, probs, [vd_ref[0, pl.ds(k0 + ch * KCH, KCH), cols] for ch in chunks],
                           acc_scr.at[c])

    _direct_key_steps(q0, w, attend, wide=wide_c)
    for c in range(DSA_KV_HEADS):
        o = acc_scr[c] / l_scr[c]
        for g in range(DSA_GROUP):
            h = c * DSA_GROUP + g
            o_ref[0, :, h * DSA_HEAD_DIM:(h + 1) * DSA_HEAD_DIM] = o[g * tq:(g + 1) * tq].astype(o_ref.dtype)


def _dsa_attention(qd, qi, wi, kd, vd, kilo, kihi, q_pos0, s_real, tq, w, wide_c):
    _, b, t, _ = qd.shape
    s = kd.shape[1]
    assert w % tq == 0 and w % KCH == 0 and s % w == 0 and q_pos0 % tq == 0 and tq % CHUNK == 0
    kern = functools.partial(
        _dsa_kernel, tq=tq, w=w, wide_c=wide_c, q_pos0=q_pos0, s_real=s_real,
        topk=min(IDX_TOPK, s_real // 4), idx_bits=int(s).bit_length(), scale=DSA_HEAD_DIM ** -0.5)
    heads = lambda: pl.BlockSpec((8, 1, tq, 128), lambda bi, i: (0, bi, i, 0))
    keys = lambda width: _resident((1, s, width), lambda bi, i: (bi, 0, 0))
    return pl.pallas_call(
        kern, grid=(b, t // tq),
        in_specs=[heads(), heads(), pl.BlockSpec((1, tq, 128), lambda bi, i: (bi, i, 0)),
                  keys(256), keys(256), keys(128), keys(128)],
        out_specs=pl.BlockSpec((1, tq, DSA_HEADS * DSA_HEAD_DIM), lambda bi, i: (bi, i, 0)),
        out_shape=jax.ShapeDtypeStruct((b, t, DSA_HEADS * DSA_HEAD_DIM), BF16),
        scratch_shapes=[pltpu.VMEM((s // LANES, tq, LANES), jnp.int32), pltpu.VMEM((2, tq, LANES), F32),
                        pltpu.VMEM((tq, LANES), jnp.int32), pltpu.VMEM((tq, LANES), jnp.int32),
                        pltpu.VMEM((tq, LANES), F32),
                        pltpu.VMEM((tq, LANES), jnp.int32),
                        pltpu.VMEM((DSA_KV_HEADS, DSA_GROUP * tq, LANES), F32),
                        pltpu.VMEM((DSA_KV_HEADS, DSA_GROUP * tq, LANES), F32),
                        pltpu.VMEM((DSA_KV_HEADS, DSA_GROUP * tq, DSA_HEAD_DIM), F32)],
        compiler_params=_params(2), name="dsa_attn",
    )(qd, qi, wi, kd, vd, kilo, kihi)


def _layer_norm(v, g, b):
    mu = jnp.mean(v, axis=-1, keepdims=True)
    d = v - mu
    var = jnp.mean(d * d, axis=-1, keepdims=True)
    return d * lax.rsqrt(var + LN_EPS) * g + b


def _outln_kernel(mla_ref, dsa_ref, x_ref, wo_ref, g_ref, b_ref, o_ref, *, alpha, half):
    a = _dot(mla_ref[...], wo_ref[0:half, :]) + _dot(dsa_ref[...], wo_ref[half:, :])
    o_ref[...] = _layer_norm(alpha * x_ref[...] + a, g_ref[...], b_ref[...])


def _out_ln(mla_o, dsa_o, x2d, wp, alpha, tm):
    n, d = x2d.shape
    half = mla_o.shape[1]
    row = lambda w: pl.BlockSpec((tm, w), lambda i: (i, 0))
    return pl.pallas_call(
        functools.partial(_outln_kernel, alpha=alpha, half=half), grid=(n // tm,),
        in_specs=[row(half), row(dsa_o.shape[1]), row(d),
                  _resident(wp["w_o"].shape, lambda i: (0, 0)),
                  _resident((1, d), lambda i: (0, 0)), _resident((1, d), lambda i: (0, 0))],
        out_specs=row(d), out_shape=jax.ShapeDtypeStruct((n, d), F32),
        compiler_params=_params(1), name="out_ln",
    )(mla_o, dsa_o, x2d, wp["w_o"], wp["ln1_g"], wp["ln1_b"])


def _ffn_kernel(x_ref, wg_ref, wu_ref, wd_ref, g_ref, b_ref, o_ref, xb_scr, acc_scr, *, alpha):
    j = pl.program_id(1)

    @pl.when(j == 0)
    def _():
        xb_scr[...] = x_ref[...].astype(BF16)
        acc_scr[...] = jnp.zeros(acc_scr.shape, F32)

    xb = xb_scr[...]
    gate = _dot(xb, wg_ref[...])
    up = _dot(xb, wu_ref[...])
    hidden = gate * (1.0 / (1.0 + jnp.exp(-gate))) * up
    acc_scr[...] += _dot(hidden.astype(BF16), wd_ref[...])

    @pl.when(j == pl.num_programs(1) - 1)
    def _():
        o_ref[...] = _layer_norm(alpha * x_ref[...] + acc_scr[...], g_ref[...], b_ref[...])


def _ffn_ln(x2d, wp, alpha, tm, tf):
    n, d = x2d.shape
    dff = wp["w_gate"].shape[1]
    return pl.pallas_call(
        functools.partial(_ffn_kernel, alpha=alpha), grid=(n // tm, dff // tf),
        in_specs=[pl.BlockSpec((tm, d), lambda i, j: (i, 0)),
                  pl.BlockSpec((d, tf), lambda i, j: (0, j)),
                  pl.BlockSpec((d, tf), lambda i, j: (0, j)),
                  pl.BlockSpec((tf, d), lambda i, j: (j, 0)),
                  _resident((1, d), lambda i, j: (0, 0)), _resident((1, d), lambda i, j: (0, 0))],
        out_specs=pl.BlockSpec((tm, d), lambda i, j: (i, 0)),
        out_shape=jax.ShapeDtypeStruct((n, d), F32),
        scratch_shapes=[pltpu.VMEM((tm, d), BF16), pltpu.VMEM((tm, d), F32)],
        compiler_params=_params(2), name="ffn_ln",
    )(x2d, wp["w_gate"], wp["w_up"], wp["w_down"], wp["ln2_g"], wp["ln2_b"])


def _pack_weights(w_in, w_uq, g_q, w_ukv, g_kv, w_o, ln1_g, ln1_b, w_gate, w_up, w_down, ln2_g, ln2_b):
    d = w_in.shape[0]
    splits = (MLA_Q_LORA, MLA_KV_LORA, MLA_ROPE, DSA_HEADS * DSA_HEAD_DIM, DSA_KV_HEADS * DSA_HEAD_DIM,
              DSA_KV_HEADS * DSA_HEAD_DIM, IDX_HEADS * IDX_DIM, IDX_DIM, IDX_HEADS)
    offs = np.cumsum(splits)[:-1].tolist()
    c_q, c_kv, k_r, q_d, k_d, v_d, q_i, k_i, w_i = jnp.split(w_in, offs, axis=1)
    z = lambda n: jnp.zeros((d, n), w_in.dtype)
    w_in_p = jnp.concatenate(
        [c_q, c_kv, q_d, k_d, v_d, q_i, k_r, z(64), k_i, z(64), z(64), k_i, w_i, z(128 - IDX_HEADS)], axis=1)
    assert w_in_p.shape[1] == _IN_COLS_P
    w_uq_p = jnp.pad(w_uq, ((0, 0), (0, 0), (0, 256 - MLA_NOPE - MLA_ROPE)))
    return {
        "w_in": w_in_p.astype(BF16),
        "w_uq": w_uq_p.reshape(MLA_Q_LORA, MLA_HEADS * 256).astype(BF16),
        "g_q": g_q.reshape(1, -1), "g_kv": g_kv.reshape(1, -1),
        "w_uk": w_ukv[:, :, :MLA_NOPE].reshape(MLA_KV_LORA, MLA_HEADS * MLA_NOPE).astype(BF16),
        "w_uv": w_ukv[:, :, MLA_NOPE:].reshape(MLA_KV_LORA, MLA_HEADS * MLA_V).astype(BF16),
        "w_o": w_o.astype(BF16),
        "ln1_g": ln1_g.reshape(1, -1), "ln1_b": ln1_b.reshape(1, -1),
        "w_gate": w_gate.astype(BF16), "w_up": w_up.astype(BF16), "w_down": w_down.astype(BF16),
        "ln2_g": ln2_g.reshape(1, -1), "ln2_b": ln2_b.reshape(1, -1),
    }


def _trunk_layer(x, q_pos0, past, wp, alpha, cfg):
    b, t, d = x.shape
    n = b * t
    x2d = x.reshape(n, d)
    pos = jnp.tile(q_pos0 + jnp.arange(t, dtype=jnp.int32), b)
    (ckv, kpe, kd, vd, ki, q_mla, qd, qi, wi, kpeb, kdb, vdb, kilo, kihi) = _proj(x2d, pos, wp, cfg["tm_proj"])
    new_rows = (ckv.reshape(b, t, -1), kpe.reshape(b, t, -1),
                kd.reshape(b, t, DSA_KV_HEADS, DSA_HEAD_DIM), vd.reshape(b, t, DSA_KV_HEADS, DSA_HEAD_DIM),
                ki.reshape(b, t, -1))

    per_b = lambda a: a.reshape(b, t, a.shape[-1])
    ckv_all, kpeb_all, kdb_all, vdb_all, kilo_all, kihi_all = map(per_b, (ckv, kpeb, kdb, vdb, kilo, kihi))
    s_real = t
    if past is not None:
        p_ckv, p_kpe, p_kd, p_vd, p_ki = past
        s_real = p_ckv.shape[1] + t
        z64 = jnp.zeros(p_kpe.shape, BF16)
        cat = lambda c, nw: jnp.concatenate([c, nw], axis=1)
        ckv_all = cat(p_ckv, ckv_all)
        kpeb_all = cat(jnp.concatenate([p_kpe.astype(BF16), z64], axis=-1), kpeb_all)
        kdb_all = cat(p_kd.reshape(b, -1, 256).astype(BF16), kdb_all)
        vdb_all = cat(p_vd.reshape(b, -1, 256).astype(BF16), vdb_all)
        kilo_all = cat(jnp.concatenate([p_ki.astype(BF16), z64], axis=-1), kilo_all)
        kihi_all = cat(jnp.concatenate([z64, p_ki.astype(BF16)], axis=-1), kihi_all)
    s_pad = _round_up(s_real, max(cfg["w_mla"], cfg["w_dsa"]))
    if s_pad != s_real:
        padk = lambda a: jnp.pad(a, ((0, 0), (0, s_pad - s_real), (0, 0)))
        ckv_all, kpeb_all, kdb_all, vdb_all, kilo_all, kihi_all = map(
            padk, (ckv_all, kpeb_all, kdb_all, vdb_all, kilo_all, kihi_all))

    k_mla, v_mla = _kv_up(ckv_all.reshape(b * s_pad, -1), kpeb_all.reshape(b * s_pad, -1), wp, cfg["tm_kv"])
    k_mla = k_mla.reshape(MLA_HEADS, b, s_pad, 256)
    v_mla = v_mla.reshape(MLA_HEADS, b, s_pad, 128)
    mla_o = _mla_attention(q_mla.reshape(MLA_HEADS, b, t, 256), k_mla, v_mla, q_pos0,
                           cfg["tq_mla"], cfg["w_mla"])
    dsa_o = _dsa_attention(qd.reshape(DSA_HEADS, b, t, 128), qi.reshape(IDX_HEADS // 2, b, t, 128),
                           wi.reshape(b, t, 128), kdb_all, vdb_all, kilo_all, kihi_all,
                           q_pos0, s_real, cfg["tq_dsa"], cfg["w_dsa"], cfg["wide_dsa"])
    x1 = _out_ln(mla_o.reshape(n, -1), dsa_o.reshape(n, -1), x2d, wp, alpha, cfg["tm_out"])
    y = _ffn_ln(x1, wp, alpha, cfg["tm_ffn"], cfg["tf_ffn"])
    return y.reshape(b, t, d), new_rows


_PROMPT_CFG = dict(tm_proj=512, tm_kv=512, tq_mla=512, tq_dsa=128, w_mla=512, w_dsa=512, wide_dsa=4,
                   tm_out=512, tm_ffn=512, tf_ffn=512)
_SAMPLE_CFG = dict(tm_proj=256, tm_kv=256, tq_mla=64, tq_dsa=64, w_mla=256, w_dsa=256, wide_dsa=2,
                   tm_out=256, tm_ffn=512, tf_ffn=512)


def kernel(x_prompt, x_sample, cache_mla_ckv, cache_mla_kpe, cache_dsa_k, cache_dsa_v, cache_idx_k, w_in, w_uq, mla_q_norm_g, w_ukv, mla_kv_norm_g, w_o, ln1_g, ln1_b, w_gate, w_up, w_down, ln2_g, ln2_b):
    depth = w_in.shape[0]
    alpha = (2 * depth) ** 0.25
    past_len = cache_mla_ckv.shape[2]
    y_p, y_s = x_prompt, x_sample
    rows_p, rows_s = [], []
    for l in range(depth):
        wp = _pack_weights(w_in[l], w_uq[l], mla_q_norm_g[l], w_ukv[l], mla_kv_norm_g[l], w_o[l],
                           ln1_g[l], ln1_b[l], w_gate[l], w_up[l], w_down[l], ln2_g[l], ln2_b[l])
        y_p, r_p = _trunk_layer(y_p, 0, None, wp, alpha, _PROMPT_CFG)
        past = (cache_mla_ckv[l], cache_mla_kpe[l], cache_dsa_k[l], cache_dsa_v[l], cache_idx_k[l])
        y_s, r_s = _trunk_layer(y_s, past_len, past, wp, alpha, _SAMPLE_CFG)
        rows_p.append(r_p)
        rows_s.append(r_s)
    stack = lambda rows, i: jnp.stack([r[i] for r in rows], axis=0)
    return (y_p, y_s,
            stack(rows_p, 0), stack(rows_p, 1), stack(rows_p, 2), stack(rows_p, 3), stack(rows_p, 4),
            stack(rows_s, 0), stack(rows_s, 1), stack(rows_s, 2), stack(rows_s, 3), stack(rows_s, 4))
```

```python
import functools

import numpy as np
import jax
import jax.numpy as jnp
from jax import lax
from jax.experimental import pallas as pl
from jax.experimental.pallas import tpu as pltpu

CHUNK = 64
CHUNK_SHIFT = 6
ROPE_THETA = 10000.0
MLA_HEADS = 8
MLA_Q_LORA = 512
MLA_KV_LORA = 512
MLA_NOPE = 128
MLA_ROPE = 64
MLA_V = 128
DSA_HEADS = 8
DSA_KV_HEADS = 2
DSA_GROUP = DSA_HEADS // DSA_KV_HEADS
DSA_HEAD_DIM = 128
IDX_HEADS = 16
IDX_DIM = 64
IDX_TOPK = 256
LN_EPS = 1e-5
RMS_EPS = 1e-6

LANES = 128
KCH = 256
MASKED = -1e30
INT_MIN = -2 ** 31
LOG2E = 1.4426950408889634
VMEM_LIMIT = 56 * 1024 * 1024

_C_CQ = 0
_C_CKV = 512
_C_QD = 1024
_C_KD = 2048
_C_VD = 2304
_C_QI = 2560
_C_KR = 3584
_C_KIHI = 3840
_IN_COLS_P = 4096

F32 = jnp.float32
BF16 = jnp.bfloat16


def _dot(a, b):
    return jnp.dot(a, b, preferred_element_type=F32)


def _dot_nt(a, b):
    return lax.dot_general(a, b, (((1,), (1,)), ((), ())), preferred_element_type=F32)


def _params(n_axes, vmem=VMEM_LIMIT):
    return pltpu.CompilerParams(dimension_semantics=("arbitrary",) * n_axes, vmem_limit_bytes=vmem)


def _round_up(n, m):
    return -(-n // m) * m


def _resident(shape, index_map):
    return pl.BlockSpec(shape, index_map, pipeline_mode=pl.Buffered(1))


def _rope128(x, cos, sin_signed):
    return x * cos + pltpu.roll(x, 64, 1) * sin_signed


def _rope64(x, cos, sin_signed, first_half):
    rot = jnp.where(first_half, pltpu.roll(x, 96, 1), pltpu.roll(x, 32, 1))
    return x * cos + rot * sin_signed


def _rope_tables(pos):
    pos = pos.astype(F32)[:, None]
    lane = np.arange(LANES)

    def tables(dim):
        half = dim // 2
        inv = 1.0 / (ROPE_THETA ** (jnp.arange(half, dtype=F32) / half))
        ang = pos * inv[None, :]
        cos, sin = jnp.cos(ang), jnp.sin(ang)
        idx = lane % half
        sign = np.where((lane % dim) < half, -1.0, 1.0).astype(np.float32)
        return cos[:, idx], sin[:, idx] * sign[None, :]

    c128, s128 = tables(128)
    c64, s64 = tables(64)
    return c128, s128, c64, s64


def _proj_kernel(x_ref, w_ref, wuq_ref, gq_ref, gkv_ref, c128_ref, s128_ref, c64_ref, s64_ref,
                 ckv_ref, kpe_ref, kd_ref, vd_ref, ki_ref,
                 q_ref, qd_ref, qi_ref, wi_ref, kpeb_ref, kdb_ref, vdb_ref, kilo_ref, kihi_ref):
    xb = x_ref[...].astype(BF16)
    c128, s128 = c128_ref[...], s128_ref[...]
    c64, s64 = c64_ref[...], s64_ref[...]
    first_half = (lax.broadcasted_iota(jnp.int32, c64.shape, 1) % 64) < 32

    def seg(a, b):
        return _dot(xb, w_ref[:, a:b])

    def rms(v, g):
        return v * lax.rsqrt(jnp.mean(v * v, axis=-1, keepdims=True) + RMS_EPS) * g

    qn = rms(seg(_C_CQ, _C_CQ + MLA_Q_LORA), gq_ref[...]).astype(BF16)
    for h in range(MLA_HEADS):
        qh = _dot(qn, wuq_ref[:, h * 256:(h + 1) * 256])
        q_ref[h, :, 0:128] = qh[:, 0:128].astype(BF16)
        q_ref[h, :, 128:256] = _rope64(qh[:, 128:256], c64, s64, first_half).astype(BF16)

    ckv_ref[...] = rms(seg(_C_CKV, _C_CKV + MLA_KV_LORA), gkv_ref[...])

    def slab_pair(a):
        y = seg(a, a + 256)
        return y[:, 0:128], y[:, 128:256]

    for hp in range(DSA_HEADS // 2):
        for j, slab in enumerate(slab_pair(_C_QD + hp * 256)):
            qd_ref[2 * hp + j] = _rope128(slab, c128, s128).astype(BF16)
    for c, slab in enumerate(slab_pair(_C_KD)):
        kdc = _rope128(slab, c128, s128)
        kd_ref[:, c * 128:(c + 1) * 128] = kdc
        kdb_ref[:, c * 128:(c + 1) * 128] = kdc.astype(BF16)
    vd = seg(_C_VD, _C_VD + 256)
    vd_ref[...] = vd
    vdb_ref[...] = vd.astype(BF16)

    for hq in range(IDX_HEADS // 4):
        for j, slab in enumerate(slab_pair(_C_QI + hq * 256)):
            qi_ref[2 * hq + j] = _rope64(slab, c64, s64, first_half).astype(BF16)

    kr, kilo = slab_pair(_C_KR)
    kr = _rope64(kr, c64, s64, first_half)
    kpe_ref[...] = kr[:, 0:MLA_ROPE]
    kpeb_ref[...] = kr.astype(BF16)
    kilo = _rope64(kilo, c64, s64, first_half)
    ki_ref[...] = kilo[:, 0:IDX_DIM]
    kilo_ref[...] = kilo.astype(BF16)
    kihi, wi = slab_pair(_C_KIHI)
    kihi_ref[...] = _rope64(kihi, c64, s64, first_half).astype(BF16)
    wi_ref[...] = wi * (IDX_DIM ** -0.5 * IDX_HEADS ** -0.5)


def _proj(x2d, pos, wp, tm):
    n, d = x2d.shape
    c128, s128, c64, s64 = _rope_tables(pos)
    row = lambda w: pl.BlockSpec((tm, w), lambda i: (i, 0))
    heads = lambda nh, w: pl.BlockSpec((nh, tm, w), lambda i: (0, i, 0))
    out_shapes = (
        jax.ShapeDtypeStruct((n, MLA_KV_LORA), F32),
        jax.ShapeDtypeStruct((n, MLA_ROPE), F32),
        jax.ShapeDtypeStruct((n, 256), F32),
        jax.ShapeDtypeStruct((n, 256), F32),
        jax.ShapeDtypeStruct((n, IDX_DIM), F32),
        jax.ShapeDtypeStruct((MLA_HEADS, n, 256), BF16),
        jax.ShapeDtypeStruct((DSA_HEADS, n, 128), BF16),
        jax.ShapeDtypeStruct((IDX_HEADS // 2, n, 128), BF16),
        jax.ShapeDtypeStruct((n, 128), F32),
        jax.ShapeDtypeStruct((n, 128), BF16),
        jax.ShapeDtypeStruct((n, 256), BF16),
        jax.ShapeDtypeStruct((n, 256), BF16),
        jax.ShapeDtypeStruct((n, 128), BF16),
        jax.ShapeDtypeStruct((n, 128), BF16),
    )
    out_specs = (row(MLA_KV_LORA), row(MLA_ROPE), row(256), row(256), row(IDX_DIM),
                 heads(MLA_HEADS, 256), heads(DSA_HEADS, 128), heads(IDX_HEADS // 2, 128),
                 row(128), row(128), row(256), row(256), row(128), row(128))
    in_specs = [row(d),
                _resident((d, _IN_COLS_P), lambda i: (0, 0)),
                _resident((MLA_Q_LORA, MLA_HEADS * 256), lambda i: (0, 0)),
                _resident((1, MLA_Q_LORA), lambda i: (0, 0)),
                _resident((1, MLA_KV_LORA), lambda i: (0, 0)),
                row(128), row(128), row(128), row(128)]
    return pl.pallas_call(
        _proj_kernel, grid=(n // tm,), in_specs=in_specs, out_specs=out_specs, out_shape=out_shapes,
        compiler_params=_params(1), name="proj",
    )(x2d, wp["w_in"], wp["w_uq"], wp["g_q"], wp["g_kv"], c128, s128, c64, s64)


def _kvup_kernel(ckv_ref, kpeb_ref, wk_ref, wv_ref, k_ref, v_ref):
    cb = ckv_ref[...].astype(BF16)
    kpe = kpeb_ref[...]
    for hp in range(MLA_HEADS // 2):
        k2 = _dot(cb, wk_ref[:, hp * 256:(hp + 1) * 256]).astype(BF16)
        v2 = _dot(cb, wv_ref[:, hp * 256:(hp + 1) * 256]).astype(BF16)
        for j in range(2):
            h = 2 * hp + j
            k_ref[h, :, 0:128] = k2[:, j * 128:(j + 1) * 128]
            k_ref[h, :, 128:256] = kpe
            v_ref[h] = v2[:, j * 128:(j + 1) * 128]


def _kv_up(ckv2d, kpeb2d, wp, tm):
    n = ckv2d.shape[0]
    return pl.pallas_call(
        _kvup_kernel, grid=(n // tm,),
        in_specs=[pl.BlockSpec((tm, MLA_KV_LORA), lambda i: (i, 0)),
                  pl.BlockSpec((tm, 128), lambda i: (i, 0)),
                  _resident((MLA_KV_LORA, MLA_HEADS * MLA_NOPE), lambda i: (0, 0)),
                  _resident((MLA_KV_LORA, MLA_HEADS * MLA_V), lambda i: (0, 0))],
        out_specs=(pl.BlockSpec((MLA_HEADS, tm, 256), lambda i: (0, i, 0)),
                   pl.BlockSpec((MLA_HEADS, tm, 128), lambda i: (0, i, 0))),
        out_shape=(jax.ShapeDtypeStruct((MLA_HEADS, n, 256), BF16),
                   jax.ShapeDtypeStruct((MLA_HEADS, n, 128), BF16)),
        compiler_params=_params(1), name="kv_up",
    )(ckv2d, kpeb2d, wp["w_uk"], wp["w_uv"])


def _row_max(s):
    return jnp.broadcast_to(jnp.max(s, axis=1, keepdims=True), (s.shape[0], LANES))


def _softmax_probs(s, smax, m_scr, l_scr, scale):
    coef = scale * LOG2E
    m_prev = m_scr[...]
    m_next = m_prev
    for sm in smax:
        m_next = jnp.maximum(m_next, sm)
    m_wide = jnp.concatenate([m_next] * (KCH // LANES), axis=1)
    alpha = jnp.exp2((m_prev - m_next) * coef)
    l_sum, probs = None, []
    for s_ch in s:
        p = jnp.exp2((s_ch - m_wide) * coef)
        p_sum = jnp.sum(p, axis=1, keepdims=True)
        l_sum = p_sum if l_sum is None else l_sum + p_sum
        probs.append(p.astype(BF16))
    l_scr[...] = alpha * l_scr[...] + l_sum
    m_scr[...] = m_next
    return alpha, probs


def _accumulate_pv(alpha, probs, v, acc_scr):
    pv = None
    for p_ch, v_ch in zip(probs, v):
        pv_ch = _dot(p_ch, v_ch)
        pv = pv_ch if pv is None else pv + pv_ch
    acc_scr[...] = acc_scr[...] * alpha + pv


def _chunk_mask(q0, kb0, tq, tk):
    qch = (q0 + lax.broadcasted_iota(jnp.int32, (tq, 1), 0)) >> CHUNK_SHIFT
    kch = (kb0 + lax.broadcasted_iota(jnp.int32, (1, tk), 1)) >> CHUNK_SHIFT
    return kch <= qch


def _three_stage_key_steps(q0, w, qk, sm, pv):
    n = q0 // w
    at = lambda i: pl.multiple_of(i * w, w)

    @pl.when(n == 0)
    def _():
        qk(0, 0)
        sm(0, 0, True)
        pv(0, 0)

    @pl.when(n >= 1)
    def _():
        qk(0, 0)
        qk(w, 1)
        sm(0, 0, False)
        pairs = (n - 1) // 2

        def body(t, carry):
            i = 2 * t
            qk(at(i + 2), 0)
            sm(1, at(i + 1), False)
            pv(0, at(i))
            qk(at(i + 3), 1)
            sm(0, at(i + 2), False)
            pv(1, at(i + 1))
            return carry

        lax.fori_loop(0, pairs, body, 0)
        i = 2 * pairs

        @pl.when(n - i == 1)
        def _():
            sm(1, at(i + 1), True)
            pv(0, at(i))
            pv(1, at(i + 1))

        @pl.when(n - i == 2)
        def _():
            qk(at(i + 2), 0)
            sm(1, at(i + 1), False)
            pv(0, at(i))
            sm(0, at(i + 2), True)
            pv(1, at(i + 1))
            pv(0, at(i + 2))

    return (n + 1) * w


def _direct_key_steps(q0, w, fn, wide=2):
    n = q0 // w
    ww = wide * w
    lax.fori_loop(0, n // wide, lambda i, c: (fn(pl.multiple_of(i * ww, ww), ww, False), c)[1], 0)
    lax.fori_loop((n // wide) * wide, n, lambda i, c: (fn(pl.multiple_of(i * w, w), w, False), c)[1], 0)
    fn(pl.multiple_of(n * w, w), w, True)
    return (n + 1) * w


def _mla_kernel(q_ref, k_ref, v_ref, o_ref, s_scr, smax_scr, p_scr, alpha_scr, m_scr, l_scr, acc_scr,
                *, tq, w, q_pos0, scale):
    q0 = q_pos0 + pl.program_id(2) * tq
    m_scr[...] = jnp.full(m_scr.shape, MASKED, F32)
    l_scr[...] = jnp.zeros(l_scr.shape, F32)
    acc_scr[...] = jnp.zeros(acc_scr.shape, F32)
    q = q_ref[0, 0]

    nch = w // KCH

    def qk(k0, buf):
        for ch in range(nch):
            s = _dot_nt(q, k_ref[0, 0, pl.ds(k0 + ch * KCH, KCH), :])
            s_scr[buf, ch] = s
            smax_scr[buf, ch] = _row_max(s)

    def sm(buf, k0, masked):
        s = [s_scr[buf, ch] for ch in range(nch)]
        if masked:
            s = [jnp.where(_chunk_mask(q0, k0 + ch * KCH, tq, KCH), s[ch], MASKED) for ch in range(nch)]
            smax = [_row_max(s_ch) for s_ch in s]
        else:
            smax = [smax_scr[buf, ch] for ch in range(nch)]
        alpha, probs = _softmax_probs(s, smax, m_scr, l_scr, scale)
        alpha_scr[buf] = alpha
        for ch in range(nch):
            p_scr[buf, ch] = probs[ch]

    def pv(buf, k0):
        _accumulate_pv(alpha_scr[buf], [p_scr[buf, ch] for ch in range(nch)],
                       [v_ref[0, 0, pl.ds(k0 + ch * KCH, KCH), :] for ch in range(nch)], acc_scr)

    _three_stage_key_steps(q0, w, qk, sm, pv)
    o_ref[0] = (acc_scr[...] / l_scr[...]).astype(o_ref.dtype)


def _mla_attention(q, k, v, q_pos0, tq, w):
    nh, b, t, _ = q.shape
    s = k.shape[2]
    assert w % tq == 0 and w % KCH == 0 and s % w == 0 and q_pos0 % tq == 0 and tq % CHUNK == 0
    kern = functools.partial(_mla_kernel, tq=tq, w=w, q_pos0=q_pos0,
                             scale=(MLA_NOPE + MLA_ROPE) ** -0.5)
    return pl.pallas_call(
        kern, grid=(b, nh, t // tq),
        in_specs=[pl.BlockSpec((1, 1, tq, 256), lambda bi, h, i: (h, bi, i, 0)),
                  pl.BlockSpec((1, 1, s, 256), lambda bi, h, i: (h, bi, 0, 0)),
                  pl.BlockSpec((1, 1, s, 128), lambda bi, h, i: (h, bi, 0, 0))],
        out_specs=pl.BlockSpec((1, tq, 128), lambda bi, h, i: (bi, i, h)),
        out_shape=jax.ShapeDtypeStruct((b, t, nh * MLA_V), BF16),
        scratch_shapes=[pltpu.VMEM((2, w // KCH, tq, KCH), F32), pltpu.VMEM((2, w // KCH, tq, LANES), F32),
                        pltpu.VMEM((2, w // KCH, tq, KCH), BF16), pltpu.VMEM((2, tq, LANES), F32),
                        pltpu.VMEM((tq, LANES), F32), pltpu.VMEM((tq, LANES), F32),
                        pltpu.VMEM((tq, MLA_V), F32)],
        compiler_params=_params(3), name="mla_attn",
    )(q, k, v)


def _dsa_kernel(qd_ref, qi_ref, wi_ref, kd_ref, vd_ref, kilo_ref, kihi_ref, o_ref,
                key_scr, top_scr, c_scr, hi_scr, cnt_scr, cand_scr, m_scr, l_scr, acc_scr,
                *, tq, w, wide, q_pos0, s_real, topk, idx_bits, scale):
    q0 = q_pos0 + pl.program_id(1) * tq
    sub = w // LANES

    qi_all = qi_ref[:, 0].reshape(IDX_HEADS // 2 * tq, LANES)
    wi = wi_ref[0]

    def to_key(f):
        bits = pltpu.bitcast(f, jnp.int32)
        return bits ^ ((bits >> 31) & 0x7FFFFFFF)

    def score_step(k0, width, masked):
        top1, top2 = top_scr[0], top_scr[1]
        for ch in range(width // KCH):
            kc = k0 + ch * KCH
            lo = _dot_nt(qi_all, kilo_ref[0, pl.ds(kc, KCH), :])
            hi = _dot_nt(qi_all, kihi_ref[0, pl.ds(kc, KCH), :])
            score = jnp.zeros((tq, KCH), F32)
            for hp in range(IDX_HEADS // 2):
                rows = slice(hp * tq, (hp + 1) * tq)
                score = score + wi[:, 2 * hp:2 * hp + 1] * jnp.maximum(lo[rows], 0.0)
                score = score + wi[:, 2 * hp + 1:2 * hp + 2] * jnp.maximum(hi[rows], 0.0)
            key = to_key(score)
            if masked:
                visible_here = _chunk_mask(q0, kc, tq, KCH)
                key = jnp.where(visible_here, key, INT_MIN)
                score = jnp.where(visible_here, score, -jnp.inf)
            for u in range(KCH // LANES):
                key_scr[kc // LANES + u] = key[:, u * LANES:(u + 1) * LANES]
                x = score[:, u * LANES:(u + 1) * LANES]
                top2 = jnp.maximum(top2, jnp.minimum(top1, x))
                top1 = jnp.maximum(top1, x)
        top_scr[0] = top1
        top_scr[1] = top2

    top_scr[...] = jnp.full(top_scr.shape, -jnp.inf, F32)
    k_end = _direct_key_steps(q0, w, score_step, wide=wide)
    n_sb = k_end // w

    qpos = q0 + lax.broadcasted_iota(jnp.int32, (tq, LANES), 0)
    visible = jnp.minimum(((qpos >> CHUNK_SHIFT) + 1) * CHUNK, s_real)
    k_target = jnp.minimum(visible, topk).astype(F32)

    def count(mode):
        cand = cand_scr[...]
        cval = c_scr[...]

        def span(k0, nblk, acc):
            for u in range(nblk):
                blk = key_scr[k0 // LANES + u]
                if mode == "ge":
                    hit = blk >= cand
                else:
                    idx = k0 + u * LANES + lax.broadcasted_iota(jnp.int32, (tq, LANES), 1)
                    hit = jnp.where(blk == cval, idx, jnp.int32(2 ** 30)) < cand
                acc = acc + jnp.where(hit, 1.0, 0.0)
            return acc

        acc = lax.fori_loop(0, n_sb // 4,
                            lambda i, a: span(pl.multiple_of(i * 4 * w, 4 * w), 4 * sub, a),
                            jnp.zeros((tq, LANES), F32))
        acc = lax.fori_loop(4 * (n_sb // 4), n_sb,
                            lambda i, a: span(pl.multiple_of(i * w, w), sub, a), acc)
        cnt_scr[...] = jnp.broadcast_to(jnp.sum(acc, axis=1, keepdims=True), (tq, LANES))

    lane_min = lambda a: jnp.broadcast_to(jnp.min(a, axis=1, keepdims=True), (tq, LANES))
    lane_max = lambda a: jnp.broadcast_to(jnp.max(a, axis=1, keepdims=True), (tq, LANES))
    lo_f = lane_min(top_scr[1])
    hi_f = jnp.where(k_target > float(LANES), lane_max(top_scr[1]), lane_max(top_scr[0]))
    c_lo = jnp.where(lo_f == -jnp.inf, INT_MIN, jnp.where(lo_f == 0.0, -1, to_key(lo_f)))
    c_hi = jnp.where(hi_f == 0.0, 0, to_key(hi_f))

    c_scr[...] = c_lo
    hi_scr[...] = c_hi + 1

    def midpoint(lo, hi):
        return (lo >> 1) + (hi >> 1) + (lo & hi & 1)

    def bisect(carry):
        it, _, cnt_lo = carry
        lo, hi = c_scr[...], hi_scr[...]
        mid = midpoint(lo, hi)
        cand = jnp.where(cnt_lo == k_target, lo, mid)
        cand_scr[...] = cand
        count("ge")
        cnt = cnt_scr[...]
        take = cnt >= k_target
        lo, hi = jnp.where(take, cand, lo), jnp.where(take, hi, cand)
        cnt_lo = jnp.where(take, cnt, cnt_lo)
        c_scr[...] = lo
        hi_scr[...] = hi
        open_rows = jnp.logical_and(cnt_lo != k_target, midpoint(lo, hi) != lo)
        any_open = jnp.max(jnp.where(open_rows, 1.0, 0.0), axis=0, keepdims=True)[0, 0] > 0.0
        return it + 1, any_open, cnt_lo

    _, _, cnt_lo = lax.while_loop(lambda carry: jnp.logical_and(carry[1], carry[0] < 40), bisect,
                                  (jnp.int32(0), jnp.bool_(True), jnp.full((tq, LANES), -1.0, F32)))
    ties = jnp.max(jnp.abs(cnt_lo - k_target)) > 0.0

    @pl.when(ties)
    def _():
        cand_scr[...] = c_scr[...] + 1
        count("ge")
        need = k_target - cnt_scr[...]
        x = jnp.zeros((tq, LANES), jnp.int32)
        for bit in range(idx_bits - 1, -1, -1):
            cand_scr[...] = x + (1 << bit)
            count("eq_lt")
            x = jnp.where(cnt_scr[...] < need, x + (1 << bit), x)
        cand_scr[...] = x

        def demote(kb, carry):
            cval, last = c_scr[...], cand_scr[...]
            idx = kb * LANES + lax.broadcasted_iota(jnp.int32, (tq, LANES), 1)
            blk = key_scr[kb]
            drop = jnp.where(blk == cval, idx, jnp.int32(-1)) > last
            key_scr[kb] = jnp.where(drop, cval - 1, blk)
            return carry

        lax.fori_loop(0, k_end // LANES, demote, 0)

    m_scr[...] = jnp.full(m_scr.shape, MASKED, F32)
    l_scr[...] = jnp.zeros(l_scr.shape, F32)
    acc_scr[...] = jnp.zeros(acc_scr.shape, F32)

    rows_c = DSA_GROUP * tq

    def attend(k0, width, masked):
        del masked
        chunks = range(width // KCH)
        cval = c_scr[...]
        per_chunk = KCH // LANES
        bias = [jnp.concatenate([jnp.where(key_scr[k0 // LANES + ch * per_chunk + u] >= cval, 0.0, MASKED)
                                 for u in range(per_chunk)], axis=1) for ch in chunks]
        bias = [jnp.concatenate([b] * DSA_GROUP, axis=0) for b in bias]
        for c in range(DSA_KV_HEADS):
            qg = qd_ref[c * DSA_GROUP:(c + 1) * DSA_GROUP, 0].reshape(rows_c, DSA_HEAD_DIM)
            cols = slice(c * DSA_HEAD_DIM, (c + 1) * DSA_HEAD_DIM)
            s = [_dot_nt(qg, kd_ref[0, pl.ds(k0 + ch * KCH, KCH), cols]) + bias[ch] for ch in chunks]
            alpha, probs = _softmax_probs(s, [_row_max(s_ch) for s_ch in s],
                                          m_scr.at[c], l_scr.at[c], scale)
            _accumulate_pv(alpha, probs, [vd_ref[0, pl.ds(k0 + ch * KCH, KCH), cols] for ch in chunks],
                           acc_scr.at[c])

    _direct_key_steps(q0, w, attend, wide=wide)
    for c in range(DSA_KV_HEADS):
        o = acc_scr[c] / l_scr[c]
        for g in range(DSA_GROUP):
            h = c * DSA_GROUP + g
            o_ref[0, :, h * DSA_HEAD_DIM:(h + 1) * DSA_HEAD_DIM] = o[g * tq:(g + 1) * tq].astype(o_ref.dtype)


def _dsa_attention(qd, qi, wi, kd, vd, kilo, kihi, q_pos0, s_real, tq, w, wide):
    _, b, t, _ = qd.shape
    s = kd.shape[1]
    assert w % tq == 0 and w % KCH == 0 and s % w == 0 and q_pos0 % tq == 0 and tq % CHUNK == 0
    kern = functools.partial(
        _dsa_kernel, tq=tq, w=w, wide=wide, q_pos0=q_pos0, s_real=s_real,
        topk=min(IDX_TOPK, s_real // 4), idx_bits=int(s).bit_length(), scale=DSA_HEAD_DIM ** -0.5)
    heads = lambda: pl.BlockSpec((8, 1, tq, 128), lambda bi, i: (0, bi, i, 0))
    keys = lambda width: _resident((1, s, width), lambda bi, i: (bi, 0, 0))
    return pl.pallas_call(
        kern, grid=(b, t // tq),
        in_specs=[heads(), heads(), pl.BlockSpec((1, tq, 128), lambda bi, i: (bi, i, 0)),
                  keys(256), keys(256), keys(128), keys(128)],
        out_specs=pl.BlockSpec((1, tq, DSA_HEADS * DSA_HEAD_DIM), lambda bi, i: (bi, i, 0)),
        out_shape=jax.ShapeDtypeStruct((b, t, DSA_HEADS * DSA_HEAD_DIM), BF16),
        scratch_shapes=[pltpu.VMEM((s // LANES, tq, LANES), jnp.int32), pltpu.VMEM((2, tq, LANES), F32),
                        pltpu.VMEM((tq, LANES), jnp.int32), pltpu.VMEM((tq, LANES), jnp.int32),
                        pltpu.VMEM((tq, LANES), F32),
                        pltpu.VMEM((tq, LANES), jnp.int32),
                        pltpu.VMEM((DSA_KV_HEADS, DSA_GROUP * tq, LANES), F32),
                        pltpu.VMEM((DSA_KV_HEADS, DSA_GROUP * tq, LANES), F32),
                        pltpu.VMEM((DSA_KV_HEADS, DSA_GROUP * tq, DSA_HEAD_DIM), F32)],
        compiler_params=_params(2), name="dsa_attn",
    )(qd, qi, wi, kd, vd, kilo, kihi)


def _layer_norm(v, g, b):
    mu = jnp.mean(v, axis=-1, keepdims=True)
    d = v - mu
    var = jnp.mean(d * d, axis=-1, keepdims=True)
    return d * lax.rsqrt(var + LN_EPS) * g + b


def _outln_kernel(mla_ref, dsa_ref, x_ref, wo_ref, g_ref, b_ref, o_ref, *, alpha, half):
    a = _dot(mla_ref[...], wo_ref[0:half, :]) + _dot(dsa_ref[...], wo_ref[half:, :])
    o_ref[...] = _layer_norm(alpha * x_ref[...] + a, g_ref[...], b_ref[...])


def _out_ln(mla_o, dsa_o, x2d, wp, alpha, tm):
    n, d = x2d.shape
    half = mla_o.shape[1]
    row = lambda w: pl.BlockSpec((tm, w), lambda i: (i, 0))
    return pl.pallas_call(
        functools.partial(_outln_kernel, alpha=alpha, half=half), grid=(n // tm,),
        in_specs=[row(half), row(dsa_o.shape[1]), row(d),
                  _resident(wp["w_o"].shape, lambda i: (0, 0)),
                  _resident((1, d), lambda i: (0, 0)), _resident((1, d), lambda i: (0, 0))],
        out_specs=row(d), out_shape=jax.ShapeDtypeStruct((n, d), F32),
        compiler_params=_params(1), name="out_ln",
    )(mla_o, dsa_o, x2d, wp["w_o"], wp["ln1_g"], wp["ln1_b"])


def _ffn_kernel(x_ref, wg_ref, wu_ref, wd_ref, g_ref, b_ref, o_ref, xb_scr, acc_scr, *, alpha):
    j = pl.program_id(1)

    @pl.when(j == 0)
    def _():
        xb_scr[...] = x_ref[...].astype(BF16)
        acc_scr[...] = jnp.zeros(acc_scr.shape, F32)

    xb = xb_scr[...]
    gate = _dot(xb, wg_ref[...])
    up = _dot(xb, wu_ref[...])
    hidden = gate * (1.0 / (1.0 + jnp.exp(-gate))) * up
    acc_scr[...] += _dot(hidden.astype(BF16), wd_ref[...])

    @pl.when(j == pl.num_programs(1) - 1)
    def _():
        o_ref[...] = _layer_norm(alpha * x_ref[...] + acc_scr[...], g_ref[...], b_ref[...])


def _ffn_ln(x2d, wp, alpha, tm, tf):
    n, d = x2d.shape
    dff = wp["w_gate"].shape[1]
    return pl.pallas_call(
        functools.partial(_ffn_kernel, alpha=alpha), grid=(n // tm, dff // tf),
        in_specs=[pl.BlockSpec((tm, d), lambda i, j: (i, 0)),
                  pl.BlockSpec((d, tf), lambda i, j: (0, j)),
                  pl.BlockSpec((d, tf), lambda i, j: (0, j)),
                  pl.BlockSpec((tf, d), lambda i, j: (j, 0)),
                  _resident((1, d), lambda i, j: (0, 0)), _resident((1, d), lambda i, j: (0, 0))],
        out_specs=pl.BlockSpec((tm, d), lambda i, j: (i, 0)),
        out_shape=jax.ShapeDtypeStruct((n, d), F32),
        scratch_shapes=[pltpu.VMEM((tm, d), BF16), pltpu.VMEM((tm, d), F32)],
        compiler_params=_params(2), name="ffn_ln",
    )(x2d, wp["w_gate"], wp["w_up"], wp["w_down"], wp["ln2_g"], wp["ln2_b"])


def _pack_weights(w_in, w_uq, g_q, w_ukv, g_kv, w_o, ln1_g, ln1_b, w_gate, w_up, w_down, ln2_g, ln2_b):
    d = w_in.shape[0]
    splits = (MLA_Q_LORA, MLA_KV_LORA, MLA_ROPE, DSA_HEADS * DSA_HEAD_DIM, DSA_KV_HEADS * DSA_HEAD_DIM,
              DSA_KV_HEADS * DSA_HEAD_DIM, IDX_HEADS * IDX_DIM, IDX_DIM, IDX_HEADS)
    offs = np.cumsum(splits)[:-1].tolist()
    c_q, c_kv, k_r, q_d, k_d, v_d, q_i, k_i, w_i = jnp.split(w_in, offs, axis=1)
    z = lambda n: jnp.zeros((d, n), w_in.dtype)
    w_in_p = jnp.concatenate(
        [c_q, c_kv, q_d, k_d, v_d, q_i, k_r, z(64), k_i, z(64), z(64), k_i, w_i, z(128 - IDX_HEADS)], axis=1)
    assert w_in_p.shape[1] == _IN_COLS_P
    w_uq_p = jnp.pad(w_uq, ((0, 0), (0, 0), (0, 256 - MLA_NOPE - MLA_ROPE)))
    return {
        "w_in": w_in_p.astype(BF16),
        "w_uq": w_uq_p.reshape(MLA_Q_LORA, MLA_HEADS * 256).astype(BF16),
        "g_q": g_q.reshape(1, -1), "g_kv": g_kv.reshape(1, -1),
        "w_uk": w_ukv[:, :, :MLA_NOPE].reshape(MLA_KV_LORA, MLA_HEADS * MLA_NOPE).astype(BF16),
        "w_uv": w_ukv[:, :, MLA_NOPE:].reshape(MLA_KV_LORA, MLA_HEADS * MLA_V).astype(BF16),
        "w_o": w_o.astype(BF16),
        "ln1_g": ln1_g.reshape(1, -1), "ln1_b": ln1_b.reshape(1, -1),
        "w_gate": w_gate.astype(BF16), "w_up": w_up.astype(BF16), "w_down": w_down.astype(BF16),
        "ln2_g": ln2_g.reshape(1, -1), "ln2_b": ln2_b.reshape(1, -1),
    }


def _trunk_layer(x, q_pos0, past, wp, alpha, cfg):
    b, t, d = x.shape
    n = b * t
    x2d = x.reshape(n, d)
    pos = jnp.tile(q_pos0 + jnp.arange(t, dtype=jnp.int32), b)
    (ckv, kpe, kd, vd, ki, q_mla, qd, qi, wi, kpeb, kdb, vdb, kilo, kihi) = _proj(x2d, pos, wp, cfg["tm_proj"])
    new_rows = (ckv.reshape(b, t, -1), kpe.reshape(b, t, -1),
                kd.reshape(b, t, DSA_KV_HEADS, DSA_HEAD_DIM), vd.reshape(b, t, DSA_KV_HEADS, DSA_HEAD_DIM),
                ki.reshape(b, t, -1))

    per_b = lambda a: a.reshape(b, t, a.shape[-1])
    ckv_all, kpeb_all, kdb_all, vdb_all, kilo_all, kihi_all = map(per_b, (ckv, kpeb, kdb, vdb, kilo, kihi))
    s_real = t
    if past is not None:
        p_ckv, p_kpe, p_kd, p_vd, p_ki = past
        s_real = p_ckv.shape[1] + t
        z64 = jnp.zeros(p_kpe.shape, BF16)
        cat = lambda c, nw: jnp.concatenate([c, nw], axis=1)
        ckv_all = cat(p_ckv, ckv_all)
        kpeb_all = cat(jnp.concatenate([p_kpe.astype(BF16), z64], axis=-1), kpeb_all)
        kdb_all = cat(p_kd.reshape(b, -1, 256).astype(BF16), kdb_all)
        vdb_all = cat(p_vd.reshape(b, -1, 256).astype(BF16), vdb_all)
        kilo_all = cat(jnp.concatenate([p_ki.astype(BF16), z64], axis=-1), kilo_all)
        kihi_all = cat(jnp.concatenate([z64, p_ki.astype(BF16)], axis=-1), kihi_all)
    s_pad = _round_up(s_real, max(cfg["w_mla"], cfg["w_dsa"]))
    if s_pad != s_real:
        padk = lambda a: jnp.pad(a, ((0, 0), (0, s_pad - s_real), (0, 0)))
        ckv_all, kpeb_all, kdb_all, vdb_all, kilo_all, kihi_all = map(
            padk, (ckv_all, kpeb_all, kdb_all, vdb_all, kilo_all, kihi_all))

    k_mla, v_mla = _kv_up(ckv_all.reshape(b * s_pad, -1), kpeb_all.reshape(b * s_pad, -1), wp, cfg["tm_kv"])
    k_mla = k_mla.reshape(MLA_HEADS, b, s_pad, 256)
    v_mla = v_mla.reshape(MLA_HEADS, b, s_pad, 128)
    mla_o = _mla_attention(q_mla.reshape(MLA_HEADS, b, t, 256), k_mla, v_mla, q_pos0,
                           cfg["tq_mla"], cfg["w_mla"])
    dsa_o = _dsa_attention(qd.reshape(DSA_HEADS, b, t, 128), qi.reshape(IDX_HEADS // 2, b, t, 128),
                           wi.reshape(b, t, 128), kdb_all, vdb_all, kilo_all, kihi_all,
                           q_pos0, s_real, cfg["tq_dsa"], cfg["w_dsa"], cfg["wide_dsa"])
    x1 = _out_ln(mla_o.reshape(n, -1), dsa_o.reshape(n, -1), x2d, wp, alpha, cfg["tm_out"])
    y = _ffn_ln(x1, wp, alpha, cfg["tm_ffn"], cfg["tf_ffn"])
    return y.reshape(b, t, d), new_rows


_PROMPT_CFG = dict(tm_proj=512, tm_kv=512, tq_mla=512, tq_dsa=128, w_mla=512, w_dsa=512, wide_dsa=4,
                   tm_out=512, tm_ffn=512, tf_ffn=512)
_SAMPLE_CFG = dict(tm_proj=256, tm_kv=256, tq_mla=64, tq_dsa=64, w_mla=256, w_dsa=256, wide_dsa=2,
                   tm_out=256, tm_ffn=512, tf_ffn=512)


def kernel(x_prompt, x_sample, cache_mla_ckv, cache_mla_kpe, cache_dsa_k, cache_dsa_v, cache_idx_k, w_in, w_uq, mla_q_norm_g, w_ukv, mla_kv_norm_g, w_o, ln1_g, ln1_b, w_gate, w_up, w_down, ln2_g, ln2_b):
    depth = w_in.shape[0]
    alpha = (2 * depth) ** 0.25
    past_len = cache_mla_ckv.shape[2]
    y_p, y_s = x_prompt, x_sample
    rows_p, rows_s = [], []
    for l in range(depth):
        wp = _pack_weights(w_in[l], w_uq[l], mla_q_norm_g[l], w_ukv[l], mla_kv_norm_g[l], w_o[l],
                           ln1_g[l], ln1_b[l], w_gate[l], w_up[l], w_down[l], ln2_g[l], ln2_b[l])
        y_p, r_p = _trunk_layer(y_p, 0, None, wp, alpha, _PROMPT_CFG)
        past = (cache_mla_ckv[l], cache_mla_kpe[l], cache_dsa_k[l], cache_dsa_v[l], cache_idx_k[l])
        y_s, r_s = _trunk_layer(y_s, past_len, past, wp, alpha, _SAMPLE_CFG)
        rows_p.append(r_p)
        rows_s.append(r_s)
    stack = lambda rows, i: jnp.stack([r[i] for r in rows], axis=0)
    return (y_p, y_s,
            stack(rows_p, 0), stack(rows_p, 1), stack(rows_p, 2), stack(rows_p, 3), stack(rows_p, 4),
            stack(rows_s, 0), stack(rows_s, 1), stack(rows_s, 2), stack(rows_s, 3), stack(rows_s, 4))
```

```python
import functools

import numpy as np
import jax
import jax.numpy as jnp
from jax import lax
from jax.experimental import pallas as pl
from jax.experimental.pallas import tpu as pltpu

CHUNK = 64
CHUNK_SHIFT = 6
ROPE_THETA = 10000.0
MLA_HEADS = 8
MLA_Q_LORA = 512
MLA_KV_LORA = 512
MLA_NOPE = 128
MLA_ROPE = 64
MLA_V = 128
DSA_HEADS = 8
DSA_KV_HEADS = 2
DSA_GROUP = DSA_HEADS // DSA_KV_HEADS
DSA_HEAD_DIM = 128
IDX_HEADS = 16
IDX_DIM = 64
IDX_TOPK = 256
LN_EPS = 1e-5
RMS_EPS = 1e-6

LANES = 128
KCH = 256
MASKED = -1e30
INT_MIN = -2 ** 31
LOG2E = 1.4426950408889634
VMEM_LIMIT = 56 * 1024 * 1024

_C_CQ = 0
_C_CKV = 512
_C_QD = 1024
_C_KD = 2048
_C_VD = 2304
_C_QI = 2560
_C_KR = 3584
_C_KIHI = 3840
_IN_COLS_P = 4096

F32 = jnp.float32
BF16 = jnp.bfloat16


def _dot(a, b):
    return jnp.dot(a, b, preferred_element_type=F32)


def _dot_nt(a, b):
    return lax.dot_general(a, b, (((1,), (1,)), ((), ())), preferred_element_type=F32)


def _params(n_axes, vmem=VMEM_LIMIT):
    return pltpu.CompilerParams(dimension_semantics=("arbitrary",) * n_axes, vmem_limit_bytes=vmem)


def _round_up(n, m):
    return -(-n // m) * m


def _resident(shape, index_map):
    return pl.BlockSpec(shape, index_map, pipeline_mode=pl.Buffered(1))


def _rope128(x, cos, sin_signed):
    return x * cos + pltpu.roll(x, 64, 1) * sin_signed


def _rope64(x, cos, sin_signed, first_half):
    rot = jnp.where(first_half, pltpu.roll(x, 96, 1), pltpu.roll(x, 32, 1))
    return x * cos + rot * sin_signed


def _rope_tables(pos):
    pos = pos.astype(F32)[:, None]
    lane = np.arange(LANES)

    def tables(dim):
        half = dim // 2
        inv = 1.0 / (ROPE_THETA ** (jnp.arange(half, dtype=F32) / half))
        ang = pos * inv[None, :]
        cos, sin = jnp.cos(ang), jnp.sin(ang)
        idx = lane % half
        sign = np.where((lane % dim) < half, -1.0, 1.0).astype(np.float32)
        return cos[:, idx], sin[:, idx] * sign[None, :]

    c128, s128 = tables(128)
    c64, s64 = tables(64)
    return c128, s128, c64, s64


def _proj_kernel(x_ref, w_ref, wuq_ref, gq_ref, gkv_ref, c128_ref, s128_ref, c64_ref, s64_ref,
                 ckv_ref, kpe_ref, kd_ref, vd_ref, ki_ref,
                 q_ref, qd_ref, qi_ref, wi_ref, kpeb_ref, kdb_ref, vdb_ref, kilo_ref, kihi_ref):
    xb = x_ref[...].astype(BF16)
    c128, s128 = c128_ref[...], s128_ref[...]
    c64, s64 = c64_ref[...], s64_ref[...]
    first_half = (lax.broadcasted_iota(jnp.int32, c64.shape, 1) % 64) < 32

    def seg(a, b):
        return _dot(xb, w_ref[:, a:b])

    def rms(v, g):
        return v * lax.rsqrt(jnp.mean(v * v, axis=-1, keepdims=True) + RMS_EPS) * g

    qn = rms(seg(_C_CQ, _C_CQ + MLA_Q_LORA), gq_ref[...]).astype(BF16)
    for h in range(MLA_HEADS):
        qh = _dot(qn, wuq_ref[:, h * 256:(h + 1) * 256])
        q_ref[h, :, 0:128] = qh[:, 0:128].astype(BF16)
        q_ref[h, :, 128:256] = _rope64(qh[:, 128:256], c64, s64, first_half).astype(BF16)

    ckv_ref[...] = rms(seg(_C_CKV, _C_CKV + MLA_KV_LORA), gkv_ref[...])

    def slab_pair(a):
        y = seg(a, a + 256)
        return y[:, 0:128], y[:, 128:256]

    for hp in range(DSA_HEADS // 2):
        for j, slab in enumerate(slab_pair(_C_QD + hp * 256)):
            qd_ref[2 * hp + j] = _rope128(slab, c128, s128).astype(BF16)
    for c, slab in enumerate(slab_pair(_C_KD)):
        kdc = _rope128(slab, c128, s128)
        kd_ref[:, c * 128:(c + 1) * 128] = kdc
        kdb_ref[:, c * 128:(c + 1) * 128] = kdc.astype(BF16)
    vd = seg(_C_VD, _C_VD + 256)
    vd_ref[...] = vd
    vdb_ref[...] = vd.astype(BF16)

    for hq in range(IDX_HEADS // 4):
        for j, slab in enumerate(slab_pair(_C_QI + hq * 256)):
            qi_ref[2 * hq + j] = _rope64(slab, c64, s64, first_half).astype(BF16)

    kr, kilo = slab_pair(_C_KR)
    kr = _rope64(kr, c64, s64, first_half)
    kpe_ref[...] = kr[:, 0:MLA_ROPE]
    kpeb_ref[...] = kr.astype(BF16)
    kilo = _rope64(kilo, c64, s64, first_half)
    ki_ref[...] = kilo[:, 0:IDX_DIM]
    kilo_ref[...] = kilo.astype(BF16)
    kihi, wi = slab_pair(_C_KIHI)
    kihi_ref[...] = _rope64(kihi, c64, s64, first_half).astype(BF16)
    wi_ref[...] = wi * (IDX_DIM ** -0.5 * IDX_HEADS ** -0.5)


def _proj(x2d, pos, wp, tm):
    n, d = x2d.shape
    c128, s128, c64, s64 = _rope_tables(pos)
    row = lambda w: pl.BlockSpec((tm, w), lambda i: (i, 0))
    heads = lambda nh, w: pl.BlockSpec((nh, tm, w), lambda i: (0, i, 0))
    out_shapes = (
        jax.ShapeDtypeStruct((n, MLA_KV_LORA), F32),
        jax.ShapeDtypeStruct((n, MLA_ROPE), F32),
        jax.ShapeDtypeStruct((n, 256), F32),
        jax.ShapeDtypeStruct((n, 256), F32),
        jax.ShapeDtypeStruct((n, IDX_DIM), F32),
        jax.ShapeDtypeStruct((MLA_HEADS, n, 256), BF16),
        jax.ShapeDtypeStruct((DSA_HEADS, n, 128), BF16),
        jax.ShapeDtypeStruct((IDX_HEADS // 2, n, 128), BF16),
        jax.ShapeDtypeStruct((n, 128), F32),
        jax.ShapeDtypeStruct((n, 128), BF16),
        jax.ShapeDtypeStruct((n, 256), BF16),
        jax.ShapeDtypeStruct((n, 256), BF16),
        jax.ShapeDtypeStruct((n, 128), BF16),
        jax.ShapeDtypeStruct((n, 128), BF16),
    )
    out_specs = (row(MLA_KV_LORA), row(MLA_ROPE), row(256), row(256), row(IDX_DIM),
                 heads(MLA_HEADS, 256), heads(DSA_HEADS, 128), heads(IDX_HEADS // 2, 128),
                 row(128), row(128), row(256), row(256), row(128), row(128))
    in_specs = [row(d),
                _resident((d, _IN_COLS_P), lambda i: (0, 0)),
                _resident((MLA_Q_LORA, MLA_HEADS * 256), lambda i: (0, 0)),
                _resident((1, MLA_Q_LORA), lambda i: (0, 0)),
                _resident((1, MLA_KV_LORA), lambda i: (0, 0)),
                row(128), row(128), row(128), row(128)]
    return pl.pallas_call(
        _proj_kernel, grid=(n // tm,), in_specs=in_specs, out_specs=out_specs, out_shape=out_shapes,
        compiler_params=_params(1), name="proj",
    )(x2d, wp["w_in"], wp["w_uq"], wp["g_q"], wp["g_kv"], c128, s128, c64, s64)


def _kvup_kernel(ckv_ref, kpeb_ref, wk_ref, wv_ref, k_ref, v_ref):
    cb = ckv_ref[...].astype(BF16)
    kpe = kpeb_ref[...]
    for hp in range(MLA_HEADS // 2):
        k2 = _dot(cb, wk_ref[:, hp * 256:(hp + 1) * 256]).astype(BF16)
        v2 = _dot(cb, wv_ref[:, hp * 256:(hp + 1) * 256]).astype(BF16)
        for j in range(2):
            h = 2 * hp + j
            k_ref[h, :, 0:128] = k2[:, j * 128:(j + 1) * 128]
            k_ref[h, :, 128:256] = kpe
            v_ref[h] = v2[:, j * 128:(j + 1) * 128]


def _kv_up(ckv2d, kpeb2d, wp, tm):
    n = ckv2d.shape[0]
    return pl.pallas_call(
        _kvup_kernel, grid=(n // tm,),
        in_specs=[pl.BlockSpec((tm, MLA_KV_LORA), lambda i: (i, 0)),
                  pl.BlockSpec((tm, 128), lambda i: (i, 0)),
                  _resident((MLA_KV_LORA, MLA_HEADS * MLA_NOPE), lambda i: (0, 0)),
                  _resident((MLA_KV_LORA, MLA_HEADS * MLA_V), lambda i: (0, 0))],
        out_specs=(pl.BlockSpec((MLA_HEADS, tm, 256), lambda i: (0, i, 0)),
                   pl.BlockSpec((MLA_HEADS, tm, 128), lambda i: (0, i, 0))),
        out_shape=(jax.ShapeDtypeStruct((MLA_HEADS, n, 256), BF16),
                   jax.ShapeDtypeStruct((MLA_HEADS, n, 128), BF16)),
        compiler_params=_params(1), name="kv_up",
    )(ckv2d, kpeb2d, wp["w_uk"], wp["w_uv"])


def _row_max(s):
    return jnp.broadcast_to(jnp.max(s, axis=1, keepdims=True), (s.shape[0], LANES))


def _softmax_probs(s, smax, m_scr, l_scr, scale):
    coef = scale * LOG2E
    m_prev = m_scr[...]
    m_next = m_prev
    for sm in smax:
        m_next = jnp.maximum(m_next, sm)
    m_wide = jnp.concatenate([m_next] * (KCH // LANES), axis=1)
    alpha = jnp.exp2((m_prev - m_next) * coef)
    l_sum, probs = None, []
    for s_ch in s:
        p = jnp.exp2((s_ch - m_wide) * coef)
        p_sum = jnp.sum(p, axis=1, keepdims=True)
        l_sum = p_sum if l_sum is None else l_sum + p_sum
        probs.append(p.astype(BF16))
    l_scr[...] = alpha * l_scr[...] + l_sum
    m_scr[...] = m_next
    return alpha, probs


def _accumulate_pv(alpha, probs, v, acc_scr):
    pv = None
    for p_ch, v_ch in zip(probs, v):
        pv_ch = _dot(p_ch, v_ch)
        pv = pv_ch if pv is None else pv + pv_ch
    acc_scr[...] = acc_scr[...] * alpha + pv


def _chunk_mask(q0, kb0, tq, tk):
    qch = (q0 + lax.broadcasted_iota(jnp.int32, (tq, 1), 0)) >> CHUNK_SHIFT
    kch = (kb0 + lax.broadcasted_iota(jnp.int32, (1, tk), 1)) >> CHUNK_SHIFT
    return kch <= qch


def _three_stage_key_steps(q0, w, qk, sm, pv):
    n = q0 // w
    at = lambda i: pl.multiple_of(i * w, w)

    @pl.when(n == 0)
    def _():
        qk(0, 0)
        sm(0, 0, True)
        pv(0, 0)

    @pl.when(n >= 1)
    def _():
        qk(0, 0)
        qk(w, 1)
        sm(0, 0, False)
        pairs = (n - 1) // 2

        def body(t, carry):
            i = 2 * t
            qk(at(i + 2), 0)
            sm(1, at(i + 1), False)
            pv(0, at(i))
            qk(at(i + 3), 1)
            sm(0, at(i + 2), False)
            pv(1, at(i + 1))
            return carry

        lax.fori_loop(0, pairs, body, 0)
        i = 2 * pairs

        @pl.when(n - i == 1)
        def _():
            sm(1, at(i + 1), True)
            pv(0, at(i))
            pv(1, at(i + 1))

        @pl.when(n - i == 2)
        def _():
            qk(at(i + 2), 0)
            sm(1, at(i + 1), False)
            pv(0, at(i))
            sm(0, at(i + 2), True)
            pv(1, at(i + 1))
            pv(0, at(i + 2))

    return (n + 1) * w


def _direct_key_steps(q0, w, fn, wide=2):
    n = q0 // w
    ww = wide * w
    lax.fori_loop(0, n // wide, lambda i, c: (fn(pl.multiple_of(i * ww, ww), ww, False), c)[1], 0)
    lax.fori_loop((n // wide) * wide, n, lambda i, c: (fn(pl.multiple_of(i * w, w), w, False), c)[1], 0)
    fn(pl.multiple_of(n * w, w), w, True)
    return (n + 1) * w


def _mla_kernel(q_ref, k_ref, v_ref, o_ref, s_scr, smax_scr, p_scr, alpha_scr, m_scr, l_scr, acc_scr,
                *, tq, w, q_pos0, scale):
    q0 = q_pos0 + pl.program_id(2) * tq
    m_scr[...] = jnp.full(m_scr.shape, MASKED, F32)
    l_scr[...] = jnp.zeros(l_scr.shape, F32)
    acc_scr[...] = jnp.zeros(acc_scr.shape, F32)
    q = q_ref[0, 0]

    nch = w // KCH

    def qk(k0, buf):
        joint = None
        for ch in range(nch):
            s = _dot_nt(q, k_ref[0, 0, pl.ds(k0 + ch * KCH, KCH), :])
            s_scr[buf, ch] = s
            joint = s if joint is None else jnp.maximum(joint, s)
        smax_scr[buf] = _row_max(joint)

    def sm(buf, k0, masked):
        s = [s_scr[buf, ch] for ch in range(nch)]
        if masked:
            s = [jnp.where(_chunk_mask(q0, k0 + ch * KCH, tq, KCH), s[ch], MASKED) for ch in range(nch)]
            smax = [_row_max(s_ch) for s_ch in s]
        else:
            smax = [smax_scr[buf]]
        alpha, probs = _softmax_probs(s, smax, m_scr, l_scr, scale)
        alpha_scr[buf] = alpha
        for ch in range(nch):
            p_scr[buf, ch] = probs[ch]

    def pv(buf, k0):
        _accumulate_pv(alpha_scr[buf], [p_scr[buf, ch] for ch in range(nch)],
                       [v_ref[0, 0, pl.ds(k0 + ch * KCH, KCH), :] for ch in range(nch)], acc_scr)

    _three_stage_key_steps(q0, w, qk, sm, pv)
    o_ref[0] = (acc_scr[...] / l_scr[...]).astype(o_ref.dtype)


def _mla_attention(q, k, v, q_pos0, tq, w):
    nh, b, t, _ = q.shape
    s = k.shape[2]
    assert w % tq == 0 and w % KCH == 0 and s % w == 0 and q_pos0 % tq == 0 and tq % CHUNK == 0
    kern = functools.partial(_mla_kernel, tq=tq, w=w, q_pos0=q_pos0,
                             scale=(MLA_NOPE + MLA_ROPE) ** -0.5)
    return pl.pallas_call(
        kern, grid=(b, nh, t // tq),
        in_specs=[pl.BlockSpec((1, 1, tq, 256), lambda bi, h, i: (h, bi, i, 0)),
                  pl.BlockSpec((1, 1, s, 256), lambda bi, h, i: (h, bi, 0, 0)),
                  pl.BlockSpec((1, 1, s, 128), lambda bi, h, i: (h, bi, 0, 0))],
        out_specs=pl.BlockSpec((1, tq, 128), lambda bi, h, i: (bi, i, h)),
        out_shape=jax.ShapeDtypeStruct((b, t, nh * MLA_V), BF16),
        scratch_shapes=[pltpu.VMEM((2, w // KCH, tq, KCH), F32), pltpu.VMEM((2, tq, LANES), F32),
                        pltpu.VMEM((2, w // KCH, tq, KCH), BF16), pltpu.VMEM((2, tq, LANES), F32),
                        pltpu.VMEM((tq, LANES), F32), pltpu.VMEM((tq, LANES), F32),
                        pltpu.VMEM((tq, MLA_V), F32)],
        compiler_params=_params(3), name="mla_attn",
    )(q, k, v)


def _dsa_kernel(qd_ref, qi_ref, wi_ref, kd_ref, vd_ref, kilo_ref, kihi_ref, o_ref,
                key_scr, top_scr, c_scr, hi_scr, cnt_scr, cand_scr, m_scr, l_scr, acc_scr,
                *, tq, w, wide, q_pos0, s_real, topk, idx_bits, scale):
    q0 = q_pos0 + pl.program_id(1) * tq
    sub = w // LANES

    qi_all = qi_ref[:, 0].reshape(IDX_HEADS // 2 * tq, LANES)
    wi = wi_ref[0]

    def to_key(f):
        bits = pltpu.bitcast(f, jnp.int32)
        return bits ^ ((bits >> 31) & 0x7FFFFFFF)

    def score_step(k0, width, masked):
        top1, top2 = top_scr[0], top_scr[1]
        for ch in range(width // KCH):
            kc = k0 + ch * KCH
            lo = _dot_nt(qi_all, kilo_ref[0, pl.ds(kc, KCH), :])
            hi = _dot_nt(qi_all, kihi_ref[0, pl.ds(kc, KCH), :])
            score = jnp.zeros((tq, KCH), F32)
            for hp in range(IDX_HEADS // 2):
                rows = slice(hp * tq, (hp + 1) * tq)
                score = score + wi[:, 2 * hp:2 * hp + 1] * jnp.maximum(lo[rows], 0.0)
                score = score + wi[:, 2 * hp + 1:2 * hp + 2] * jnp.maximum(hi[rows], 0.0)
            key = to_key(score)
            if masked:
                visible_here = _chunk_mask(q0, kc, tq, KCH)
                key = jnp.where(visible_here, key, INT_MIN)
                score = jnp.where(visible_here, score, -jnp.inf)
            for u in range(KCH // LANES):
                key_scr[kc // LANES + u] = key[:, u * LANES:(u + 1) * LANES]
                x = score[:, u * LANES:(u + 1) * LANES]
                top2 = jnp.maximum(top2, jnp.minimum(top1, x))
                top1 = jnp.maximum(top1, x)
        top_scr[0] = top1
        top_scr[1] = top2

    top_scr[...] = jnp.full(top_scr.shape, -jnp.inf, F32)
    k_end = _direct_key_steps(q0, w, score_step, wide=wide)
    n_sb = k_end // w

    qpos = q0 + lax.broadcasted_iota(jnp.int32, (tq, LANES), 0)
    visible = jnp.minimum(((qpos >> CHUNK_SHIFT) + 1) * CHUNK, s_real)
    k_target = jnp.minimum(visible, topk).astype(F32)

    def count(mode):
        cand = cand_scr[...]
        cval = c_scr[...]

        def span(k0, nblk, acc):
            for u in range(nblk):
                blk = key_scr[k0 // LANES + u]
                if mode == "ge":
                    hit = blk >= cand
                else:
                    idx = k0 + u * LANES + lax.broadcasted_iota(jnp.int32, (tq, LANES), 1)
                    hit = jnp.where(blk == cval, idx, jnp.int32(2 ** 30)) < cand
                acc = acc + jnp.where(hit, 1.0, 0.0)
            return acc

        acc = lax.fori_loop(0, n_sb // 4,
                            lambda i, a: span(pl.multiple_of(i * 4 * w, 4 * w), 4 * sub, a),
                            jnp.zeros((tq, LANES), F32))
        acc = lax.fori_loop(4 * (n_sb // 4), n_sb,
                            lambda i, a: span(pl.multiple_of(i * w, w), sub, a), acc)
        cnt_scr[...] = jnp.broadcast_to(jnp.sum(acc, axis=1, keepdims=True), (tq, LANES))

    lane_min = lambda a: jnp.broadcast_to(jnp.min(a, axis=1, keepdims=True), (tq, LANES))
    lane_max = lambda a: jnp.broadcast_to(jnp.max(a, axis=1, keepdims=True), (tq, LANES))
    lo_f = lane_min(top_scr[1])
    hi_f = jnp.where(k_target > float(LANES), lane_max(top_scr[1]), lane_max(top_scr[0]))
    c_lo = jnp.where(lo_f == -jnp.inf, INT_MIN, jnp.where(lo_f == 0.0, -1, to_key(lo_f)))
    c_hi = jnp.where(hi_f == 0.0, 0, to_key(hi_f))

    c_scr[...] = c_lo
    hi_scr[...] = c_hi + 1

    def midpoint(lo, hi):
        return (lo >> 1) + (hi >> 1) + (lo & hi & 1)

    def bisect(carry):
        it, _, cnt_lo = carry
        lo, hi = c_scr[...], hi_scr[...]
        mid = midpoint(lo, hi)
        cand = jnp.where(cnt_lo == k_target, lo, mid)
        cand_scr[...] = cand
        count("ge")
        cnt = cnt_scr[...]
        take = cnt >= k_target
        lo, hi = jnp.where(take, cand, lo), jnp.where(take, hi, cand)
        cnt_lo = jnp.where(take, cnt, cnt_lo)
        c_scr[...] = lo
        hi_scr[...] = hi
        open_rows = jnp.logical_and(cnt_lo != k_target, midpoint(lo, hi) != lo)
        any_open = jnp.max(jnp.where(open_rows, 1.0, 0.0), axis=0, keepdims=True)[0, 0] > 0.0
        return it + 1, any_open, cnt_lo

    _, _, cnt_lo = lax.while_loop(lambda carry: jnp.logical_and(carry[1], carry[0] < 40), bisect,
                                  (jnp.int32(0), jnp.bool_(True), jnp.full((tq, LANES), -1.0, F32)))
    ties = jnp.max(jnp.abs(cnt_lo - k_target)) > 0.0

    @pl.when(ties)
    def _():
        cand_scr[...] = c_scr[...] + 1
        count("ge")
        need = k_target - cnt_scr[...]
        x = jnp.zeros((tq, LANES), jnp.int32)
        for bit in range(idx_bits - 1, -1, -1):
            cand_scr[...] = x + (1 << bit)
            count("eq_lt")
            x = jnp.where(cnt_scr[...] < need, x + (1 << bit), x)
        cand_scr[...] = x

        def demote(kb, carry):
            cval, last = c_scr[...], cand_scr[...]
            idx = kb * LANES + lax.broadcasted_iota(jnp.int32, (tq, LANES), 1)
            blk = key_scr[kb]
            drop = jnp.where(blk == cval, idx, jnp.int32(-1)) > last
            key_scr[kb] = jnp.where(drop, cval - 1, blk)
            return carry

        lax.fori_loop(0, k_end // LANES, demote, 0)

    m_scr[...] = jnp.full(m_scr.shape, MASKED, F32)
    l_scr[...] = jnp.zeros(l_scr.shape, F32)
    acc_scr[...] = jnp.zeros(acc_scr.shape, F32)

    rows_c = DSA_GROUP * tq

    def attend(k0, width, masked):
        del masked
        chunks = range(width // KCH)
        cval = c_scr[...]
        per_chunk = KCH // LANES
        bias = [jnp.concatenate([jnp.where(key_scr[k0 // LANES + ch * per_chunk + u] >= cval, 0.0, MASKED)
                                 for u in range(per_chunk)], axis=1) for ch in chunks]
        bias = [jnp.concatenate([b] * DSA_GROUP, axis=0) for b in bias]
        for c in range(DSA_KV_HEADS):
            qg = qd_ref[c * DSA_GROUP:(c + 1) * DSA_GROUP, 0].reshape(rows_c, DSA_HEAD_DIM)
            cols = slice(c * DSA_HEAD_DIM, (c + 1) * DSA_HEAD_DIM)
            s = [_dot_nt(qg, kd_ref[0, pl.ds(k0 + ch * KCH, KCH), cols]) + bias[ch] for ch in chunks]
            alpha, probs = _softmax_probs(s, [_row_max(s_ch) for s_ch in s],
                                          m_scr.at[c], l_scr.at[c], scale)
            _accumulate_pv(alpha, probs, [vd_ref[0, pl.ds(k0 + ch * KCH, KCH), cols] for ch in chunks],
                           acc_scr.at[c])

    _direct_key_steps(q0, w, attend, wide=wide)
    for c in range(DSA_KV_HEADS):
        o = acc_scr[c] / l_scr[c]
        for g in range(DSA_GROUP):
            h = c * DSA_GROUP + g
            o_ref[0, :, h * DSA_HEAD_DIM:(h + 1) * DSA_HEAD_DIM] = o[g * tq:(g + 1) * tq].astype(o_ref.dtype)


def _dsa_attention(qd, qi, wi, kd, vd, kilo, kihi, q_pos0, s_real, tq, w, wide):
    _, b, t, _ = qd.shape
    s = kd.shape[1]
    assert w % tq == 0 and w % KCH == 0 and s % w == 0 and q_pos0 % tq == 0 and tq % CHUNK == 0
    kern = functools.partial(
        _dsa_kernel, tq=tq, w=w, wide=wide, q_pos0=q_pos0, s_real=s_real,
        topk=min(IDX_TOPK, s_real // 4), idx_bits=int(s).bit_length(), scale=DSA_HEAD_DIM ** -0.5)
    heads = lambda: pl.BlockSpec((8, 1, tq, 128), lambda bi, i: (0, bi, i, 0))
    keys = lambda width: _resident((1, s, width), lambda bi, i: (bi, 0, 0))
    return pl.pallas_call(
        kern, grid=(b, t // tq),
        in_specs=[heads(), heads(), pl.BlockSpec((1, tq, 128), lambda bi, i: (bi, i, 0)),
                  keys(256), keys(256), keys(128), keys(128)],
        out_specs=pl.BlockSpec((1, tq, DSA_HEADS * DSA_HEAD_DIM), lambda bi, i: (bi, i, 0)),
        out_shape=jax.ShapeDtypeStruct((b, t, DSA_HEADS * DSA_HEAD_DIM), BF16),
        scratch_shapes=[pltpu.VMEM((s // LANES, tq, LANES), jnp.int32), pltpu.VMEM((2, tq, LANES), F32),
                        pltpu.VMEM((tq, LANES), jnp.int32), pltpu.VMEM((tq, LANES), jnp.int32),
                        pltpu.VMEM((tq, LANES), F32),
                        pltpu.VMEM((tq, LANES), jnp.int32),
                        pltpu.VMEM((DSA_KV_HEADS, DSA_GROUP * tq, LANES), F32),
                        pltpu.VMEM((DSA_KV_HEADS, DSA_GROUP * tq, LANES), F32),
                        pltpu.VMEM((DSA_KV_HEADS, DSA_GROUP * tq, DSA_HEAD_DIM), F32)],
        compiler_params=_params(2), name="dsa_attn",
    )(qd, qi, wi, kd, vd, kilo, kihi)


def _layer_norm(v, g, b):
    mu = jnp.mean(v, axis=-1, keepdims=True)
    d = v - mu
    var = jnp.mean(d * d, axis=-1, keepdims=True)
    return d * lax.rsqrt(var + LN_EPS) * g + b


def _outln_kernel(mla_ref, dsa_ref, x_ref, wo_ref, g_ref, b_ref, o_ref, *, alpha, half):
    a = _dot(mla_ref[...], wo_ref[0:half, :]) + _dot(dsa_ref[...], wo_ref[half:, :])
    o_ref[...] = _layer_norm(alpha * x_ref[...] + a, g_ref[...], b_ref[...])


def _out_ln(mla_o, dsa_o, x2d, wp, alpha, tm):
    n, d = x2d.shape
    half = mla_o.shape[1]
    row = lambda w: pl.BlockSpec((tm, w), lambda i: (i, 0))
    return pl.pallas_call(
        functools.partial(_outln_kernel, alpha=alpha, half=half), grid=(n // tm,),
        in_specs=[row(half), row(dsa_o.shape[1]), row(d),
                  _resident(wp["w_o"].shape, lambda i: (0, 0)),
                  _resident((1, d), lambda i: (0, 0)), _resident((1, d), lambda i: (0, 0))],
        out_specs=row(d), out_shape=jax.ShapeDtypeStruct((n, d), F32),
        compiler_params=_params(1), name="out_ln",
    )(mla_o, dsa_o, x2d, wp["w_o"], wp["ln1_g"], wp["ln1_b"])


def _ffn_kernel(x_ref, wg_ref, wu_ref, wd_ref, g_ref, b_ref, o_ref, xb_scr, acc_scr, *, alpha):
    j = pl.program_id(1)

    @pl.when(j == 0)
    def _():
        xb_scr[...] = x_ref[...].astype(BF16)
        acc_scr[...] = jnp.zeros(acc_scr.shape, F32)

    xb = xb_scr[...]
    gate = _dot(xb, wg_ref[...])
    up = _dot(xb, wu_ref[...])
    hidden = gate * (1.0 / (1.0 + jnp.exp(-gate))) * up
    acc_scr[...] += _dot(hidden.astype(BF16), wd_ref[...])

    @pl.when(j == pl.num_programs(1) - 1)
    def _():
        o_ref[...] = _layer_norm(alpha * x_ref[...] + acc_scr[...], g_ref[...], b_ref[...])


def _ffn_ln(x2d, wp, alpha, tm, tf):
    n, d = x2d.shape
    dff = wp["w_gate"].shape[1]
    return pl.pallas_call(
        functools.partial(_ffn_kernel, alpha=alpha), grid=(n // tm, dff // tf),
        in_specs=[pl.BlockSpec((tm, d), lambda i, j: (i, 0)),
                  pl.BlockSpec((d, tf), lambda i, j: (0, j)),
                  pl.BlockSpec((d, tf), lambda i, j: (0, j)),
                  pl.BlockSpec((tf, d), lambda i, j: (j, 0)),
                  _resident((1, d), lambda i, j: (0, 0)), _resident((1, d), lambda i, j: (0, 0))],
        out_specs=pl.BlockSpec((tm, d), lambda i, j: (i, 0)),
        out_shape=jax.ShapeDtypeStruct((n, d), F32),
        scratch_shapes=[pltpu.VMEM((tm, d), BF16), pltpu.VMEM((tm, d), F32)],
        compiler_params=_params(2), name="ffn_ln",
    )(x2d, wp["w_gate"], wp["w_up"], wp["w_down"], wp["ln2_g"], wp["ln2_b"])


def _pack_weights(w_in, w_uq, g_q, w_ukv, g_kv, w_o, ln1_g, ln1_b, w_gate, w_up, w_down, ln2_g, ln2_b):
    d = w_in.shape[0]
    splits = (MLA_Q_LORA, MLA_KV_LORA, MLA_ROPE, DSA_HEADS * DSA_HEAD_DIM, DSA_KV_HEADS * DSA_HEAD_DIM,
              DSA_KV_HEADS * DSA_HEAD_DIM, IDX_HEADS * IDX_DIM, IDX_DIM, IDX_HEADS)
    offs = np.cumsum(splits)[:-1].tolist()
    c_q, c_kv, k_r, q_d, k_d, v_d, q_i, k_i, w_i = jnp.split(w_in, offs, axis=1)
    z = lambda n: jnp.zeros((d, n), w_in.dtype)
    w_in_p = jnp.concatenate(
        [c_q, c_kv, q_d, k_d, v_d, q_i, k_r, z(64), k_i, z(64), z(64), k_i, w_i, z(128 - IDX_HEADS)], axis=1)
    assert w_in_p.shape[1] == _IN_COLS_P
    w_uq_p = jnp.pad(w_uq, ((0, 0), (0, 0), (0, 256 - MLA_NOPE - MLA_ROPE)))
    return {
        "w_in": w_in_p.astype(BF16),
        "w_uq": w_uq_p.reshape(MLA_Q_LORA, MLA_HEADS * 256).astype(BF16),
        "g_q": g_q.reshape(1, -1), "g_kv": g_kv.reshape(1, -1),
        "w_uk": w_ukv[:, :, :MLA_NOPE].reshape(MLA_KV_LORA, MLA_HEADS * MLA_NOPE).astype(BF16),
        "w_uv": w_ukv[:, :, MLA_NOPE:].reshape(MLA_KV_LORA, MLA_HEADS * MLA_V).astype(BF16),
        "w_o": w_o.astype(BF16),
        "ln1_g": ln1_g.reshape(1, -1), "ln1_b": ln1_b.reshape(1, -1),
        "w_gate": w_gate.astype(BF16), "w_up": w_up.astype(BF16), "w_down": w_down.astype(BF16),
        "ln2_g": ln2_g.reshape(1, -1), "ln2_b": ln2_b.reshape(1, -1),
    }


def _trunk_layer(x, q_pos0, past, wp, alpha, cfg):
    b, t, d = x.shape
    n = b * t
    x2d = x.reshape(n, d)
    pos = jnp.tile(q_pos0 + jnp.arange(t, dtype=jnp.int32), b)
    (ckv, kpe, kd, vd, ki, q_mla, qd, qi, wi, kpeb, kdb, vdb, kilo, kihi) = _proj(x2d, pos, wp, cfg["tm_proj"])
    new_rows = (ckv.reshape(b, t, -1), kpe.reshape(b, t, -1),
                kd.reshape(b, t, DSA_KV_HEADS, DSA_HEAD_DIM), vd.reshape(b, t, DSA_KV_HEADS, DSA_HEAD_DIM),
                ki.reshape(b, t, -1))

    per_b = lambda a: a.reshape(b, t, a.shape[-1])
    ckv_all, kpeb_all, kdb_all, vdb_all, kilo_all, kihi_all = map(per_b, (ckv, kpeb, kdb, vdb, kilo, kihi))
    s_real = t
    if past is not None:
        p_ckv, p_kpe, p_kd, p_vd, p_ki = past
        s_real = p_ckv.shape[1] + t
        z64 = jnp.zeros(p_kpe.shape, BF16)
        cat = lambda c, nw: jnp.concatenate([c, nw], axis=1)
        ckv_all = cat(p_ckv, ckv_all)
        kpeb_all = cat(jnp.concatenate([p_kpe.astype(BF16), z64], axis=-1), kpeb_all)
        kdb_all = cat(p_kd.reshape(b, -1, 256).astype(BF16), kdb_all)
        vdb_all = cat(p_vd.reshape(b, -1, 256).astype(BF16), vdb_all)
        kilo_all = cat(jnp.concatenate([p_ki.astype(BF16), z64], axis=-1), kilo_all)
        kihi_all = cat(jnp.concatenate([z64, p_ki.astype(BF16)], axis=-1), kihi_all)
    s_pad = _round_up(s_real, max(cfg["w_mla"], cfg["w_dsa"]))
    if s_pad != s_real:
        padk = lambda a: jnp.pad(a, ((0, 0), (0, s_pad - s_real), (0, 0)))
        ckv_all, kpeb_all, kdb_all, vdb_all, kilo_all, kihi_all = map(
            padk, (ckv_all, kpeb_all, kdb_all, vdb_all, kilo_all, kihi_all))

    k_mla, v_mla = _kv_up(ckv_all.reshape(b * s_pad, -1), kpeb_all.reshape(b * s_pad, -1), wp, cfg["tm_kv"])
    k_mla = k_mla.reshape(MLA_HEADS, b, s_pad, 256)
    v_mla = v_mla.reshape(MLA_HEADS, b, s_pad, 128)
    mla_o = _mla_attention(q_mla.reshape(MLA_HEADS, b, t, 256), k_mla, v_mla, q_pos0,
                           cfg["tq_mla"], cfg["w_mla"])
    dsa_o = _dsa_attention(qd.reshape(DSA_HEADS, b, t, 128), qi.reshape(IDX_HEADS // 2, b, t, 128),
                           wi.reshape(b, t, 128), kdb_all, vdb_all, kilo_all, kihi_all,
                           q_pos0, s_real, cfg["tq_dsa"], cfg["w_dsa"], cfg["wide_dsa"])
    x1 = _out_ln(mla_o.reshape(n, -1), dsa_o.reshape(n, -1), x2d, wp, alpha, cfg["tm_out"])
    y = _ffn_ln(x1, wp, alpha, cfg["tm_ffn"], cfg["tf_ffn"])
    return y.reshape(b, t, d), new_rows


_PROMPT_CFG = dict(tm_proj=512, tm_kv=512, tq_mla=512, tq_dsa=128, w_mla=512, w_dsa=512, wide_dsa=4,
                   tm_out=512, tm_ffn=512, tf_ffn=512)
_SAMPLE_CFG = dict(tm_proj=256, tm_kv=256, tq_mla=64, tq_dsa=64, w_mla=256, w_dsa=256, wide_dsa=2,
                   tm_out=256, tm_ffn=512, tf_ffn=512)


def kernel(x_prompt, x_sample, cache_mla_ckv, cache_mla_kpe, cache_dsa_k, cache_dsa_v, cache_idx_k, w_in, w_uq, mla_q_norm_g, w_ukv, mla_kv_norm_g, w_o, ln1_g, ln1_b, w_gate, w_up, w_down, ln2_g, ln2_b):
    depth = w_in.shape[0]
    alpha = (2 * depth) ** 0.25
    past_len = cache_mla_ckv.shape[2]
    y_p, y_s = x_prompt, x_sample
    rows_p, rows_s = [], []
    for l in range(depth):
        wp = _pack_weights(w_in[l], w_uq[l], mla_q_norm_g[l], w_ukv[l], mla_kv_norm_g[l], w_o[l],
                           ln1_g[l], ln1_b[l], w_gate[l], w_up[l], w_down[l], ln2_g[l], ln2_b[l])
        y_p, r_p = _trunk_layer(y_p, 0, None, wp, alpha, _PROMPT_CFG)
        past = (cache_mla_ckv[l], cache_mla_kpe[l], cache_dsa_k[l], cache_dsa_v[l], cache_idx_k[l])
        y_s, r_s = _trunk_layer(y_s, past_len, past, wp, alpha, _SAMPLE_CFG)
        rows_p.append(r_p)
        rows_s.append(r_s)
    stack = lambda rows, i: jnp.stack([r[i] for r in rows], axis=0)
    return (y_p, y_s,
            stack(rows_p, 0), stack(rows_p, 1), stack(rows_p, 2), stack(rows_p, 3), stack(rows_p, 4),
            stack(rows_s, 0), stack(rows_s, 1), stack(rows_s, 2), stack(rows_s, 3), stack(rows_s, 4))
```

```python
import functools

import numpy as np
import jax
import jax.numpy as jnp
from jax import lax
from jax.experimental import pallas as pl
from jax.experimental.pallas import tpu as pltpu

CHUNK = 64
CHUNK_SHIFT = 6
ROPE_THETA = 10000.0
MLA_HEADS = 8
MLA_Q_LORA = 512
MLA_KV_LORA = 512
MLA_NOPE = 128
MLA_ROPE = 64
MLA_V = 128
DSA_HEADS = 8
DSA_KV_HEADS = 2
DSA_GROUP = DSA_HEADS // DSA_KV_HEADS
DSA_HEAD_DIM = 128
IDX_HEADS = 16
IDX_DIM = 64
IDX_TOPK = 256
LN_EPS = 1e-5
RMS_EPS = 1e-6

LANES = 128
KCH = 256
MASKED = -1e30
INT_MIN = -2 ** 31
LOG2E = 1.4426950408889634
VMEM_LIMIT = 56 * 1024 * 1024

_C_CQ = 0
_C_CKV = 512
_C_QD = 1024
_C_KD = 2048
_C_VD = 2304
_C_QI = 2560
_C_KR = 3584
_C_KIHI = 3840
_IN_COLS_P = 4096

F32 = jnp.float32
BF16 = jnp.bfloat16


def _dot(a, b):
    return jnp.dot(a, b, preferred_element_type=F32)


def _dot_nt(a, b):
    return lax.dot_general(a, b, (((1,), (1,)), ((), ())), preferred_element_type=F32)


def _params(n_axes, vmem=VMEM_LIMIT):
    return pltpu.CompilerParams(dimension_semantics=("arbitrary",) * n_axes, vmem_limit_bytes=vmem)


def _round_up(n, m):
    return -(-n // m) * m


def _resident(shape, index_map):
    return pl.BlockSpec(shape, index_map, pipeline_mode=pl.Buffered(1))


def _rope128(x, cos, sin_signed):
    return x * cos + pltpu.roll(x, 64, 1) * sin_signed


def _rope64(x, cos, sin_signed, first_half):
    rot = jnp.where(first_half, pltpu.roll(x, 96, 1), pltpu.roll(x, 32, 1))
    return x * cos + rot * sin_signed


def _rope_tables(pos):
    pos = pos.astype(F32)[:, None]
    lane = np.arange(LANES)

    def tables(dim):
        half = dim // 2
        inv = 1.0 / (ROPE_THETA ** (jnp.arange(half, dtype=F32) / half))
        ang = pos * inv[None, :]
        cos, sin = jnp.cos(ang), jnp.sin(ang)
        idx = lane % half
        sign = np.where((lane % dim) < half, -1.0, 1.0).astype(np.float32)
        return cos[:, idx], sin[:, idx] * sign[None, :]

    c128, s128 = tables(128)
    c64, s64 = tables(64)
    return c128, s128, c64, s64


def _proj_kernel(x_ref, w_ref, wuq_ref, gq_ref, gkv_ref, c128_ref, s128_ref, c64_ref, s64_ref,
                 ckv_ref, kpe_ref, kd_ref, vd_ref, ki_ref,
                 q_ref, qd_ref, qi_ref, wi_ref, kpeb_ref, kdb_ref, vdb_ref, kilo_ref, kihi_ref):
    xb = x_ref[...].astype(BF16)
    c128, s128 = c128_ref[...], s128_ref[...]
    c64, s64 = c64_ref[...], s64_ref[...]
    first_half = (lax.broadcasted_iota(jnp.int32, c64.shape, 1) % 64) < 32

    def seg(a, b):
        return _dot(xb, w_ref[:, a:b])

    def rms(v, g):
        return v * lax.rsqrt(jnp.mean(v * v, axis=-1, keepdims=True) + RMS_EPS) * g

    qn = rms(seg(_C_CQ, _C_CQ + MLA_Q_LORA), gq_ref[...]).astype(BF16)
    for h in range(MLA_HEADS):
        qh = _dot(qn, wuq_ref[:, h * 256:(h + 1) * 256])
        q_ref[h, :, 0:128] = qh[:, 0:128].astype(BF16)
        q_ref[h, :, 128:256] = _rope64(qh[:, 128:256], c64, s64, first_half).astype(BF16)

    ckv_ref[...] = rms(seg(_C_CKV, _C_CKV + MLA_KV_LORA), gkv_ref[...])

    def slab_pair(a):
        y = seg(a, a + 256)
        return y[:, 0:128], y[:, 128:256]

    for hp in range(DSA_HEADS // 2):
        for j, slab in enumerate(slab_pair(_C_QD + hp * 256)):
            qd_ref[2 * hp + j] = _rope128(slab, c128, s128).astype(BF16)
    for c, slab in enumerate(slab_pair(_C_KD)):
        kdc = _rope128(slab, c128, s128)
        kd_ref[:, c * 128:(c + 1) * 128] = kdc
        kdb_ref[:, c * 128:(c + 1) * 128] = kdc.astype(BF16)
    vd = seg(_C_VD, _C_VD + 256)
    vd_ref[...] = vd
    vdb_ref[...] = vd.astype(BF16)

    for hq in range(IDX_HEADS // 4):
        for j, slab in enumerate(slab_pair(_C_QI + hq * 256)):
            qi_ref[2 * hq + j] = _rope64(slab, c64, s64, first_half).astype(BF16)

    kr, kilo = slab_pair(_C_KR)
    kr = _rope64(kr, c64, s64, first_half)
    kpe_ref[...] = kr[:, 0:MLA_ROPE]
    kpeb_ref[...] = kr.astype(BF16)
    kilo = _rope64(kilo, c64, s64, first_half)
    ki_ref[...] = kilo[:, 0:IDX_DIM]
    kilo_ref[...] = kilo.astype(BF16)
    kihi, wi = slab_pair(_C_KIHI)
    kihi_ref[...] = _rope64(kihi, c64, s64, first_half).astype(BF16)
    wi_ref[...] = wi * (IDX_DIM ** -0.5 * IDX_HEADS ** -0.5)


def _proj(x2d, pos, wp, tm):
    n, d = x2d.shape
    c128, s128, c64, s64 = _rope_tables(pos)
    row = lambda w: pl.BlockSpec((tm, w), lambda i: (i, 0))
    heads = lambda nh, w: pl.BlockSpec((nh, tm, w), lambda i: (0, i, 0))
    out_shapes = (
        jax.ShapeDtypeStruct((n, MLA_KV_LORA), F32),
        jax.ShapeDtypeStruct((n, MLA_ROPE), F32),
        jax.ShapeDtypeStruct((n, 256), F32),
        jax.ShapeDtypeStruct((n, 256), F32),
        jax.ShapeDtypeStruct((n, IDX_DIM), F32),
        jax.ShapeDtypeStruct((MLA_HEADS, n, 256), BF16),
        jax.ShapeDtypeStruct((DSA_HEADS, n, 128), BF16),
        jax.ShapeDtypeStruct((IDX_HEADS // 2, n, 128), BF16),
        jax.ShapeDtypeStruct((n, 128), F32),
        jax.ShapeDtypeStruct((n, 128), BF16),
        jax.ShapeDtypeStruct((n, 256), BF16),
        jax.ShapeDtypeStruct((n, 256), BF16),
        jax.ShapeDtypeStruct((n, 128), BF16),
        jax.ShapeDtypeStruct((n, 128), BF16),
    )
    out_specs = (row(MLA_KV_LORA), row(MLA_ROPE), row(256), row(256), row(IDX_DIM),
                 heads(MLA_HEADS, 256), heads(DSA_HEADS, 128), heads(IDX_HEADS // 2, 128),
                 row(128), row(128), row(256), row(256), row(128), row(128))
    in_specs = [row(d),
                _resident((d, _IN_COLS_P), lambda i: (0, 0)),
                _resident((MLA_Q_LORA, MLA_HEADS * 256), lambda i: (0, 0)),
                _resident((1, MLA_Q_LORA), lambda i: (0, 0)),
                _resident((1, MLA_KV_LORA), lambda i: (0, 0)),
                row(128), row(128), row(128), row(128)]
    return pl.pallas_call(
        _proj_kernel, grid=(n // tm,), in_specs=in_specs, out_specs=out_specs, out_shape=out_shapes,
        compiler_params=_params(1), name="proj",
    )(x2d, wp["w_in"], wp["w_uq"], wp["g_q"], wp["g_kv"], c128, s128, c64, s64)


def _kvup_kernel(ckv_ref, kpeb_ref, wk_ref, wv_ref, k_ref, v_ref):
    cb = ckv_ref[...].astype(BF16)
    kpe = kpeb_ref[...]
    for hp in range(MLA_HEADS // 2):
        k2 = _dot(cb, wk_ref[:, hp * 256:(hp + 1) * 256]).astype(BF16)
        v2 = _dot(cb, wv_ref[:, hp * 256:(hp + 1) * 256]).astype(BF16)
        for j in range(2):
            h = 2 * hp + j
            k_ref[h, :, 0:128] = k2[:, j * 128:(j + 1) * 128]
            k_ref[h, :, 128:256] = kpe
            v_ref[h] = v2[:, j * 128:(j + 1) * 128]


def _kv_up(ckv2d, kpeb2d, wp, tm):
    n = ckv2d.shape[0]
    return pl.pallas_call(
        _kvup_kernel, grid=(n // tm,),
        in_specs=[pl.BlockSpec((tm, MLA_KV_LORA), lambda i: (i, 0)),
                  pl.BlockSpec((tm, 128), lambda i: (i, 0)),
                  _resident((MLA_KV_LORA, MLA_HEADS * MLA_NOPE), lambda i: (0, 0)),
                  _resident((MLA_KV_LORA, MLA_HEADS * MLA_V), lambda i: (0, 0))],
        out_specs=(pl.BlockSpec((MLA_HEADS, tm, 256), lambda i: (0, i, 0)),
                   pl.BlockSpec((MLA_HEADS, tm, 128), lambda i: (0, i, 0))),
        out_shape=(jax.ShapeDtypeStruct((MLA_HEADS, n, 256), BF16),
                   jax.ShapeDtypeStruct((MLA_HEADS, n, 128), BF16)),
        compiler_params=_params(1), name="kv_up",
    )(ckv2d, kpeb2d, wp["w_uk"], wp["w_uv"])


def _row_max(s):
    return jnp.broadcast_to(jnp.max(s, axis=1, keepdims=True), (s.shape[0], LANES))


def _softmax_probs(s, smax, m_scr, l_scr, scale):
    coef = scale * LOG2E
    m_prev = m_scr[...]
    m_next = m_prev
    for sm in smax:
        m_next = jnp.maximum(m_next, sm)
    m_wide = jnp.concatenate([m_next] * (KCH // LANES), axis=1)
    alpha = jnp.exp2((m_prev - m_next) * coef)
    l_sum, probs = None, []
    for s_ch in s:
        p = jnp.exp2((s_ch - m_wide) * coef)
        p_sum = jnp.sum(p, axis=1, keepdims=True)
        l_sum = p_sum if l_sum is None else l_sum + p_sum
        probs.append(p.astype(BF16))
    l_scr[...] = alpha * l_scr[...] + l_sum
    m_scr[...] = m_next
    return alpha, probs


def _accumulate_pv(alpha, probs, v, acc_scr):
    pv = None
    for p_ch, v_ch in zip(probs, v):
        pv_ch = _dot(p_ch, v_ch)
        pv = pv_ch if pv is None else pv + pv_ch
    acc_scr[...] = acc_scr[...] * alpha + pv


def _chunk_mask(q0, kb0, tq, tk):
    qch = (q0 + lax.broadcasted_iota(jnp.int32, (tq, 1), 0)) >> CHUNK_SHIFT
    kch = (kb0 + lax.broadcasted_iota(jnp.int32, (1, tk), 1)) >> CHUNK_SHIFT
    return kch <= qch


def _three_stage_key_steps(q0, w, qk, sm, pv):
    n = q0 // w
    at = lambda i: pl.multiple_of(i * w, w)

    @pl.when(n == 0)
    def _():
        qk(0, 0)
        sm(0, 0, True)
        pv(0, 0)

    @pl.when(n >= 1)
    def _():
        qk(0, 0)
        qk(w, 1)
        sm(0, 0, False)
        def steps(i, count):
            for j in range(count):
                qk(at(i + 2 + j), j % 2)
                sm((j + 1) % 2, at(i + 1 + j), False)
                pv(j % 2, at(i + j))

        quads = (n - 1) // 4
        lax.fori_loop(0, quads, lambda t, c: (steps(4 * t, 4), c)[1], 0)
        pairs = (n - 1) // 2
        lax.fori_loop(2 * quads, pairs, lambda t, c: (steps(2 * t, 2), c)[1], 0)
        i = 2 * pairs

        @pl.when(n - i == 1)
        def _():
            sm(1, at(i + 1), True)
            pv(0, at(i))
            pv(1, at(i + 1))

        @pl.when(n - i == 2)
        def _():
            qk(at(i + 2), 0)
            sm(1, at(i + 1), False)
            pv(0, at(i))
            sm(0, at(i + 2), True)
            pv(1, at(i + 1))
            pv(0, at(i + 2))

    return (n + 1) * w


def _direct_key_steps(q0, w, fn, wide=2):
    n = q0 // w
    ww = wide * w
    lax.fori_loop(0, n // wide, lambda i, c: (fn(pl.multiple_of(i * ww, ww), ww, False), c)[1], 0)
    lax.fori_loop((n // wide) * wide, n, lambda i, c: (fn(pl.multiple_of(i * w, w), w, False), c)[1], 0)
    fn(pl.multiple_of(n * w, w), w, True)
    return (n + 1) * w


def _mla_kernel(q_ref, k_ref, v_ref, o_ref, s_scr, smax_scr, p_scr, alpha_scr, m_scr, l_scr, acc_scr,
                *, tq, w, q_pos0, scale):
    q0 = q_pos0 + pl.program_id(2) * tq
    m_scr[...] = jnp.full(m_scr.shape, MASKED, F32)
    l_scr[...] = jnp.zeros(l_scr.shape, F32)
    acc_scr[...] = jnp.zeros(acc_scr.shape, F32)
    q = q_ref[0, 0]

    nch = w // KCH

    def qk(k0, buf):
        joint = None
        for ch in range(nch):
            s = _dot_nt(q, k_ref[0, 0, pl.ds(k0 + ch * KCH, KCH), :])
            s_scr[buf, ch] = s
            joint = s if joint is None else jnp.maximum(joint, s)
        smax_scr[buf] = _row_max(joint)

    def sm(buf, k0, masked):
        s = [s_scr[buf, ch] for ch in range(nch)]
        if masked:
            s = [jnp.where(_chunk_mask(q0, k0 + ch * KCH, tq, KCH), s[ch], MASKED) for ch in range(nch)]
            smax = [_row_max(s_ch) for s_ch in s]
        else:
            smax = [smax_scr[buf]]
        alpha, probs = _softmax_probs(s, smax, m_scr, l_scr, scale)
        alpha_scr[buf] = alpha
        for ch in range(nch):
            p_scr[buf, ch] = probs[ch]

    def pv(buf, k0):
        _accumulate_pv(alpha_scr[buf], [p_scr[buf, ch] for ch in range(nch)],
                       [v_ref[0, 0, pl.ds(k0 + ch * KCH, KCH), :] for ch in range(nch)], acc_scr)

    _three_stage_key_steps(q0, w, qk, sm, pv)
    o_ref[0] = (acc_scr[...] / l_scr[...]).astype(o_ref.dtype)


def _mla_attention(q, k, v, q_pos0, tq, w):
    nh, b, t, _ = q.shape
    s = k.shape[2]
    assert w % tq == 0 and w % KCH == 0 and s % w == 0 and q_pos0 % tq == 0 and tq % CHUNK == 0
    kern = functools.partial(_mla_kernel, tq=tq, w=w, q_pos0=q_pos0,
                             scale=(MLA_NOPE + MLA_ROPE) ** -0.5)
    return pl.pallas_call(
        kern, grid=(b, nh, t // tq),
        in_specs=[pl.BlockSpec((1, 1, tq, 256), lambda bi, h, i: (h, bi, i, 0)),
                  pl.BlockSpec((1, 1, s, 256), lambda bi, h, i: (h, bi, 0, 0)),
                  pl.BlockSpec((1, 1, s, 128), lambda bi, h, i: (h, bi, 0, 0))],
        out_specs=pl.BlockSpec((1, tq, 128), lambda bi, h, i: (bi, i, h)),
        out_shape=jax.ShapeDtypeStruct((b, t, nh * MLA_V), BF16),
        scratch_shapes=[pltpu.VMEM((2, w // KCH, tq, KCH), F32), pltpu.VMEM((2, tq, LANES), F32),
                        pltpu.VMEM((2, w // KCH, tq, KCH), BF16), pltpu.VMEM((2, tq, LANES), F32),
                        pltpu.VMEM((tq, LANES), F32), pltpu.VMEM((tq, LANES), F32),
                        pltpu.VMEM((tq, MLA_V), F32)],
        compiler_params=_params(3), name="mla_attn",
    )(q, k, v)


def _dsa_kernel(qd_ref, qi_ref, wi_ref, kd_ref, vd_ref, kilo_ref, kihi_ref, o_ref,
                key_scr, top_scr, c_scr, hi_scr, cnt_scr, cand_scr, m_scr, l_scr, acc_scr,
                *, tq, w, wide, q_pos0, s_real, topk, idx_bits, scale):
    q0 = q_pos0 + pl.program_id(1) * tq
    sub = w // LANES

    qi_all = qi_ref[:, 0].reshape(IDX_HEADS // 2 * tq, LANES)
    wi = wi_ref[0]

    def to_key(f):
        bits = pltpu.bitcast(f, jnp.int32)
        return bits ^ ((bits >> 31) & 0x7FFFFFFF)

    def score_step(k0, width, masked):
        top1, top2 = top_scr[0], top_scr[1]
        for ch in range(width // KCH):
            kc = k0 + ch * KCH
            lo = _dot_nt(qi_all, kilo_ref[0, pl.ds(kc, KCH), :])
            hi = _dot_nt(qi_all, kihi_ref[0, pl.ds(kc, KCH), :])
            score = jnp.zeros((tq, KCH), F32)
            for hp in range(IDX_HEADS // 2):
                rows = slice(hp * tq, (hp + 1) * tq)
                score = score + wi[:, 2 * hp:2 * hp + 1] * jnp.maximum(lo[rows], 0.0)
                score = score + wi[:, 2 * hp + 1:2 * hp + 2] * jnp.maximum(hi[rows], 0.0)
            key = to_key(score)
            if masked:
                visible_here = _chunk_mask(q0, kc, tq, KCH)
                key = jnp.where(visible_here, key, INT_MIN)
                score = jnp.where(visible_here, score, -jnp.inf)
            for u in range(KCH // LANES):
                key_scr[kc // LANES + u] = key[:, u * LANES:(u + 1) * LANES]
                x = score[:, u * LANES:(u + 1) * LANES]
                top2 = jnp.maximum(top2, jnp.minimum(top1, x))
                top1 = jnp.maximum(top1, x)
        top_scr[0] = top1
        top_scr[1] = top2

    top_scr[...] = jnp.full(top_scr.shape, -jnp.inf, F32)
    k_end = _direct_key_steps(q0, w, score_step, wide=wide)
    n_sb = k_end // w

    qpos = q0 + lax.broadcasted_iota(jnp.int32, (tq, LANES), 0)
    visible = jnp.minimum(((qpos >> CHUNK_SHIFT) + 1) * CHUNK, s_real)
    k_target = jnp.minimum(visible, topk).astype(F32)

    def count(mode):
        cand = cand_scr[...]
        cval = c_scr[...]

        def span(k0, nblk, acc):
            for u in range(nblk):
                blk = key_scr[k0 // LANES + u]
                if mode == "ge":
                    hit = blk >= cand
                else:
                    idx = k0 + u * LANES + lax.broadcasted_iota(jnp.int32, (tq, LANES), 1)
                    hit = jnp.where(blk == cval, idx, jnp.int32(2 ** 30)) < cand
                acc = acc + jnp.where(hit, 1.0, 0.0)
            return acc

        acc = lax.fori_loop(0, n_sb // 4,
                            lambda i, a: span(pl.multiple_of(i * 4 * w, 4 * w), 4 * sub, a),
                            jnp.zeros((tq, LANES), F32))
        acc = lax.fori_loop(4 * (n_sb // 4), n_sb,
                            lambda i, a: span(pl.multiple_of(i * w, w), sub, a), acc)
        cnt_scr[...] = jnp.broadcast_to(jnp.sum(acc, axis=1, keepdims=True), (tq, LANES))

    lane_min = lambda a: jnp.broadcast_to(jnp.min(a, axis=1, keepdims=True), (tq, LANES))
    lane_max = lambda a: jnp.broadcast_to(jnp.max(a, axis=1, keepdims=True), (tq, LANES))
    lo_f = lane_min(top_scr[1])
    hi_f = jnp.where(k_target > float(LANES), lane_max(top_scr[1]), lane_max(top_scr[0]))
    c_lo = jnp.where(lo_f == -jnp.inf, INT_MIN, jnp.where(lo_f == 0.0, -1, to_key(lo_f)))
    c_hi = jnp.where(hi_f == 0.0, 0, to_key(hi_f))

    c_scr[...] = c_lo
    hi_scr[...] = c_hi + 1

    def midpoint(lo, hi):
        return (lo >> 1) + (hi >> 1) + (lo & hi & 1)

    def bisect(carry):
        it, _, cnt_lo = carry
        lo, hi = c_scr[...], hi_scr[...]
        mid = midpoint(lo, hi)
        cand = jnp.where(cnt_lo == k_target, lo, mid)
        cand_scr[...] = cand
        count("ge")
        cnt = cnt_scr[...]
        take = cnt >= k_target
        lo, hi = jnp.where(take, cand, lo), jnp.where(take, hi, cand)
        cnt_lo = jnp.where(take, cnt, cnt_lo)
        c_scr[...] = lo
        hi_scr[...] = hi
        open_rows = jnp.logical_and(cnt_lo != k_target, midpoint(lo, hi) != lo)
        any_open = jnp.max(jnp.where(open_rows, 1.0, 0.0), axis=0, keepdims=True)[0, 0] > 0.0
        return it + 1, any_open, cnt_lo

    _, _, cnt_lo = lax.while_loop(lambda carry: jnp.logical_and(carry[1], carry[0] < 40), bisect,
                                  (jnp.int32(0), jnp.bool_(True), jnp.full((tq, LANES), -1.0, F32)))
    ties = jnp.max(jnp.abs(cnt_lo - k_target)) > 0.0

    @pl.when(ties)
    def _():
        cand_scr[...] = c_scr[...] + 1
        count("ge")
        need = k_target - cnt_scr[...]
        x = jnp.zeros((tq, LANES), jnp.int32)
        for bit in range(idx_bits - 1, -1, -1):
            cand_scr[...] = x + (1 << bit)
            count("eq_lt")
            x = jnp.where(cnt_scr[...] < need, x + (1 << bit), x)
        cand_scr[...] = x

        def demote(kb, carry):
            cval, last = c_scr[...], cand_scr[...]
            idx = kb * LANES + lax.broadcasted_iota(jnp.int32, (tq, LANES), 1)
            blk = key_scr[kb]
            drop = jnp.where(blk == cval, idx, jnp.int32(-1)) > last
            key_scr[kb] = jnp.where(drop, cval - 1, blk)
            return carry

        lax.fori_loop(0, k_end // LANES, demote, 0)

    m_scr[...] = jnp.full(m_scr.shape, MASKED, F32)
    l_scr[...] = jnp.zeros(l_scr.shape, F32)
    acc_scr[...] = jnp.zeros(acc_scr.shape, F32)

    rows_c = DSA_GROUP * tq

    def attend(k0, width, masked):
        del masked
        chunks = range(width // KCH)
        cval = c_scr[...]
        per_chunk = KCH // LANES
        bias = [jnp.concatenate([jnp.where(key_scr[k0 // LANES + ch * per_chunk + u] >= cval, 0.0, MASKED)
                                 for u in range(per_chunk)], axis=1) for ch in chunks]
        bias = [jnp.concatenate([b] * DSA_GROUP, axis=0) for b in bias]
        for c in range(DSA_KV_HEADS):
            qg = qd_ref[c * DSA_GROUP:(c + 1) * DSA_GROUP, 0].reshape(rows_c, DSA_HEAD_DIM)
            cols = slice(c * DSA_HEAD_DIM, (c + 1) * DSA_HEAD_DIM)
            s = [_dot_nt(qg, kd_ref[0, pl.ds(k0 + ch * KCH, KCH), cols]) + bias[ch] for ch in chunks]
            alpha, probs = _softmax_probs(s, [_row_max(s_ch) for s_ch in s],
                                          m_scr.at[c], l_scr.at[c], scale)
            _accumulate_pv(alpha, probs, [vd_ref[0, pl.ds(k0 + ch * KCH, KCH), cols] for ch in chunks],
                           acc_scr.at[c])

    _direct_key_steps(q0, w, attend, wide=wide)
    for c in range(DSA_KV_HEADS):
        o = acc_scr[c] / l_scr[c]
        for g in range(DSA_GROUP):
            h = c * DSA_GROUP + g
            o_ref[0, :, h * DSA_HEAD_DIM:(h + 1) * DSA_HEAD_DIM] = o[g * tq:(g + 1) * tq].astype(o_ref.dtype)


def _dsa_attention(qd, qi, wi, kd, vd, kilo, kihi, q_pos0, s_real, tq, w, wide):
    _, b, t, _ = qd.shape
    s = kd.shape[1]
    assert w % tq == 0 and w % KCH == 0 and s % w == 0 and q_pos0 % tq == 0 and tq % CHUNK == 0
    kern = functools.partial(
        _dsa_kernel, tq=tq, w=w, wide=wide, q_pos0=q_pos0, s_real=s_real,
        topk=min(IDX_TOPK, s_real // 4), idx_bits=int(s).bit_length(), scale=DSA_HEAD_DIM ** -0.5)
    heads = lambda: pl.BlockSpec((8, 1, tq, 128), lambda bi, i: (0, bi, i, 0))
    keys = lambda width: _resident((1, s, width), lambda bi, i: (bi, 0, 0))
    return pl.pallas_call(
        kern, grid=(b, t // tq),
        in_specs=[heads(), heads(), pl.BlockSpec((1, tq, 128), lambda bi, i: (bi, i, 0)),
                  keys(256), keys(256), keys(128), keys(128)],
        out_specs=pl.BlockSpec((1, tq, DSA_HEADS * DSA_HEAD_DIM), lambda bi, i: (bi, i, 0)),
        out_shape=jax.ShapeDtypeStruct((b, t, DSA_HEADS * DSA_HEAD_DIM), BF16),
        scratch_shapes=[pltpu.VMEM((s // LANES, tq, LANES), jnp.int32), pltpu.VMEM((2, tq, LANES), F32),
                        pltpu.VMEM((tq, LANES), jnp.int32), pltpu.VMEM((tq, LANES), jnp.int32),
                        pltpu.VMEM((tq, LANES), F32),
                        pltpu.VMEM((tq, LANES), jnp.int32),
                        pltpu.VMEM((DSA_KV_HEADS, DSA_GROUP * tq, LANES), F32),
                        pltpu.VMEM((DSA_KV_HEADS, DSA_GROUP * tq, LANES), F32),
                        pltpu.VMEM((DSA_KV_HEADS, DSA_GROUP * tq, DSA_HEAD_DIM), F32)],
        compiler_params=_params(2), name="dsa_attn",
    )(qd, qi, wi, kd, vd, kilo, kihi)


def _layer_norm(v, g, b):
    mu = jnp.mean(v, axis=-1, keepdims=True)
    d = v - mu
    var = jnp.mean(d * d, axis=-1, keepdims=True)
    return d * lax.rsqrt(var + LN_EPS) * g + b


def _outln_kernel(mla_ref, dsa_ref, x_ref, wo_ref, g_ref, b_ref, o_ref, *, alpha, half):
    a = _dot(mla_ref[...], wo_ref[0:half, :]) + _dot(dsa_ref[...], wo_ref[half:, :])
    o_ref[...] = _layer_norm(alpha * x_ref[...] + a, g_ref[...], b_ref[...])


def _out_ln(mla_o, dsa_o, x2d, wp, alpha, tm):
    n, d = x2d.shape
    half = mla_o.shape[1]
    row = lambda w: pl.BlockSpec((tm, w), lambda i: (i, 0))
    return pl.pallas_call(
        functools.partial(_outln_kernel, alpha=alpha, half=half), grid=(n // tm,),
        in_specs=[row(half), row(dsa_o.shape[1]), row(d),
                  _resident(wp["w_o"].shape, lambda i: (0, 0)),
                  _resident((1, d), lambda i: (0, 0)), _resident((1, d), lambda i: (0, 0))],
        out_specs=row(d), out_shape=jax.ShapeDtypeStruct((n, d), F32),
        compiler_params=_params(1), name="out_ln",
    )(mla_o, dsa_o, x2d, wp["w_o"], wp["ln1_g"], wp["ln1_b"])


def _ffn_kernel(x_ref, wg_ref, wu_ref, wd_ref, g_ref, b_ref, o_ref, xb_scr, acc_scr, *, alpha):
    j = pl.program_id(1)

    @pl.when(j == 0)
    def _():
        xb_scr[...] = x_ref[...].astype(BF16)
        acc_scr[...] = jnp.zeros(acc_scr.shape, F32)

    xb = xb_scr[...]
    gate = _dot(xb, wg_ref[...])
    up = _dot(xb, wu_ref[...])
    hidden = gate * (1.0 / (1.0 + jnp.exp(-gate))) * up
    acc_scr[...] += _dot(hidden.astype(BF16), wd_ref[...])

    @pl.when(j == pl.num_programs(1) - 1)
    def _():
        o_ref[...] = _layer_norm(alpha * x_ref[...] + acc_scr[...], g_ref[...], b_ref[...])


def _ffn_ln(x2d, wp, alpha, tm, tf):
    n, d = x2d.shape
    dff = wp["w_gate"].shape[1]
    return pl.pallas_call(
        functools.partial(_ffn_kernel, alpha=alpha), grid=(n // tm, dff // tf),
        in_specs=[pl.BlockSpec((tm, d), lambda i, j: (i, 0)),
                  pl.BlockSpec((d, tf), lambda i, j: (0, j)),
                  pl.BlockSpec((d, tf), lambda i, j: (0, j)),
                  pl.BlockSpec((tf, d), lambda i, j: (j, 0)),
                  _resident((1, d), lambda i, j: (0, 0)), _resident((1, d), lambda i, j: (0, 0))],
        out_specs=pl.BlockSpec((tm, d), lambda i, j: (i, 0)),
        out_shape=jax.ShapeDtypeStruct((n, d), F32),
        scratch_shapes=[pltpu.VMEM((tm, d), BF16), pltpu.VMEM((tm, d), F32)],
        compiler_params=_params(2), name="ffn_ln",
    )(x2d, wp["w_gate"], wp["w_up"], wp["w_down"], wp["ln2_g"], wp["ln2_b"])


def _pack_weights(w_in, w_uq, g_q, w_ukv, g_kv, w_o, ln1_g, ln1_b, w_gate, w_up, w_down, ln2_g, ln2_b):
    d = w_in.shape[0]
    splits = (MLA_Q_LORA, MLA_KV_LORA, MLA_ROPE, DSA_HEADS * DSA_HEAD_DIM, DSA_KV_HEADS * DSA_HEAD_DIM,
              DSA_KV_HEADS * DSA_HEAD_DIM, IDX_HEADS * IDX_DIM, IDX_DIM, IDX_HEADS)
    offs = np.cumsum(splits)[:-1].tolist()
    c_q, c_kv, k_r, q_d, k_d, v_d, q_i, k_i, w_i = jnp.split(w_in, offs, axis=1)
    z = lambda n: jnp.zeros((d, n), w_in.dtype)
    w_in_p = jnp.concatenate(
        [c_q, c_kv, q_d, k_d, v_d, q_i, k_r, z(64), k_i, z(64), z(64), k_i, w_i, z(128 - IDX_HEADS)], axis=1)
    assert w_in_p.shape[1] == _IN_COLS_P
    w_uq_p = jnp.pad(w_uq, ((0, 0), (0, 0), (0, 256 - MLA_NOPE - MLA_ROPE)))
    return {
        "w_in": w_in_p.astype(BF16),
        "w_uq": w_uq_p.reshape(MLA_Q_LORA, MLA_HEADS * 256).astype(BF16),
        "g_q": g_q.reshape(1, -1), "g_kv": g_kv.reshape(1, -1),
        "w_uk": w_ukv[:, :, :MLA_NOPE].reshape(MLA_KV_LORA, MLA_HEADS * MLA_NOPE).astype(BF16),
        "w_uv": w_ukv[:, :, MLA_NOPE:].reshape(MLA_KV_LORA, MLA_HEADS * MLA_V).astype(BF16),
        "w_o": w_o.astype(BF16),
        "ln1_g": ln1_g.reshape(1, -1), "ln1_b": ln1_b.reshape(1, -1),
        "w_gate": w_gate.astype(BF16), "w_up": w_up.astype(BF16), "w_down": w_down.astype(BF16),
        "ln2_g": ln2_g.reshape(1, -1), "ln2_b": ln2_b.reshape(1, -1),
    }


def _trunk_layer(x, q_pos0, past, wp, alpha, cfg):
    b, t, d = x.shape
    n = b * t
    x2d = x.reshape(n, d)
    pos = jnp.tile(q_pos0 + jnp.arange(t, dtype=jnp.int32), b)
    (ckv, kpe, kd, vd, ki, q_mla, qd, qi, wi, kpeb, kdb, vdb, kilo, kihi) = _proj(x2d, pos, wp, cfg["tm_proj"])
    new_rows = (ckv.reshape(b, t, -1), kpe.reshape(b, t, -1),
                kd.reshape(b, t, DSA_KV_HEADS, DSA_HEAD_DIM), vd.reshape(b, t, DSA_KV_HEADS, DSA_HEAD_DIM),
                ki.reshape(b, t, -1))

    per_b = lambda a: a.reshape(b, t, a.shape[-1])
    ckv_all, kpeb_all, kdb_all, vdb_all, kilo_all, kihi_all = map(per_b, (ckv, kpeb, kdb, vdb, kilo, kihi))
    s_real = t
    if past is not None:
        p_ckv, p_kpe, p_kd, p_vd, p_ki = past
        s_real = p_ckv.shape[1] + t
        z64 = jnp.zeros(p_kpe.shape, BF16)
        cat = lambda c, nw: jnp.concatenate([c, nw], axis=1)
        ckv_all = cat(p_ckv, ckv_all)
        kpeb_all = cat(jnp.concatenate([p_kpe.astype(BF16), z64], axis=-1), kpeb_all)
        kdb_all = cat(p_kd.reshape(b, -1, 256).astype(BF16), kdb_all)
        vdb_all = cat(p_vd.reshape(b, -1, 256).astype(BF16), vdb_all)
        kilo_all = cat(jnp.concatenate([p_ki.astype(BF16), z64], axis=-1), kilo_all)
        kihi_all = cat(jnp.concatenate([z64, p_ki.astype(BF16)], axis=-1), kihi_all)
    s_pad = _round_up(s_real, max(cfg["w_mla"], cfg["w_dsa"]))
    if s_pad != s_real:
        padk = lambda a: jnp.pad(a, ((0, 0), (0, s_pad - s_real), (0, 0)))
        ckv_all, kpeb_all, kdb_all, vdb_all, kilo_all, kihi_all = map(
            padk, (ckv_all, kpeb_all, kdb_all, vdb_all, kilo_all, kihi_all))

    k_mla, v_mla = _kv_up(ckv_all.reshape(b * s_pad, -1), kpeb_all.reshape(b * s_pad, -1), wp, cfg["tm_kv"])
    k_mla = k_mla.reshape(MLA_HEADS, b, s_pad, 256)
    v_mla = v_mla.reshape(MLA_HEADS, b, s_pad, 128)
    mla_o = _mla_attention(q_mla.reshape(MLA_HEADS, b, t, 256), k_mla, v_mla, q_pos0,
                           cfg["tq_mla"], cfg["w_mla"])
    dsa_o = _dsa_attention(qd.reshape(DSA_HEADS, b, t, 128), qi.reshape(IDX_HEADS // 2, b, t, 128),
                           wi.reshape(b, t, 128), kdb_all, vdb_all, kilo_all, kihi_all,
                           q_pos0, s_real, cfg["tq_dsa"], cfg["w_dsa"], cfg["wide_dsa"])
    x1 = _out_ln(mla_o.reshape(n, -1), dsa_o.reshape(n, -1), x2d, wp, alpha, cfg["tm_out"])
    y = _ffn_ln(x1, wp, alpha, cfg["tm_ffn"], cfg["tf_ffn"])
    return y.reshape(b, t, d), new_rows


_PROMPT_CFG = dict(tm_proj=512, tm_kv=512, tq_mla=512, tq_dsa=128, w_mla=512, w_dsa=512, wide_dsa=4,
                   tm_out=512, tm_ffn=512, tf_ffn=512)
_SAMPLE_CFG = dict(tm_proj=256, tm_kv=256, tq_mla=64, tq_dsa=64, w_mla=256, w_dsa=256, wide_dsa=2,
                   tm_out=256, tm_ffn=512, tf_ffn=512)


def kernel(x_prompt, x_sample, cache_mla_ckv, cache_mla_kpe, cache_dsa_k, cache_dsa_v, cache_idx_k, w_in, w_uq, mla_q_norm_g, w_ukv, mla_kv_norm_g, w_o, ln1_g, ln1_b, w_gate, w_up, w_down, ln2_g, ln2_b):
    depth = w_in.shape[0]
    alpha = (2 * depth) ** 0.25
    past_len = cache_mla_ckv.shape[2]
    y_p, y_s = x_prompt, x_sample
    rows_p, rows_s = [], []
    for l in range(depth):
        wp = _pack_weights(w_in[l], w_uq[l], mla_q_norm_g[l], w_ukv[l], mla_kv_norm_g[l], w_o[l],
                           ln1_g[l], ln1_b[l], w_gate[l], w_up[l], w_down[l], ln2_g[l], ln2_b[l])
        y_p, r_p = _trunk_layer(y_p, 0, None, wp, alpha, _PROMPT_CFG)
        past = (cache_mla_ckv[l], cache_mla_kpe[l], cache_dsa_k[l], cache_dsa_v[l], cache_idx_k[l])
        y_s, r_s = _trunk_layer(y_s, past_len, past, wp, alpha, _SAMPLE_CFG)
        rows_p.append(r_p)
        rows_s.append(r_s)
    stack = lambda rows, i: jnp.stack([r[i] for r in rows], axis=0)
    return (y_p, y_s,
            stack(rows_p, 0), stack(rows_p, 1), stack(rows_p, 2), stack(rows_p, 3), stack(rows_p, 4),
            stack(rows_s, 0), stack(rows_s, 1), stack(rows_s, 2), stack(rows_s, 3), stack(rows_s, 4))
```
